```python
import math
import jax, jax.numpy as jnp
from jax import lax
import numpy as np


D_MODEL = 1024
BATCH = 8
SEQ = 4096
DEPTH = 2
DEC_BATCH = 32
DEC_SEQ = 8
PAST_LEN = 16384
PAGE_SIZE = 128

HEAD_DIM = 64
MOBA_HEADS = D_MODEL // HEAD_DIM
MOBA_KV_HEADS = 4
MOBA_GROUP = MOBA_HEADS // MOBA_KV_HEADS
MOBA_BLOCK = 256
MOBA_TOPK = 3
DIFF_HEADS = D_MODEL // (2 * HEAD_DIM)
N_GROUPS = 4
EXPERTS_PER_GROUP = 4
N_EXPERTS = N_GROUPS * EXPERTS_PER_GROUP
D_EXPERT = 256
TOP_EXPERTS = 2
Q_BLOCK = 128
NORM_EPS = 1e-6
SUBLN_EPS = 1e-5
NEG_INF = -1e30
N_MOBA_LAYERS = (DEPTH + 1) // 2
N_DIFF_LAYERS = DEPTH // 2

kernel_name = 'moba_diffattn_hmoe_decode_step'


def _alibi_slopes(n_heads):
    start = 2.0 ** (-8.0 / n_heads)
    return jnp.asarray(np.array([start ** (i + 1) for i in range(n_heads)], dtype=np.float32))


def _rmsnorm(x, w, eps):
    xf = x.astype(jnp.float32)
    y = xf * lax.rsqrt(jnp.mean(xf * xf, axis=-1, keepdims=True) + eps)
    return (y * w.astype(jnp.float32)).astype(x.dtype)


def _moba_sequence(q, k, v, pos0, slopes):
    lq, lk = q.shape[0], k.shape[0]
    G, R = MOBA_KV_HEADS, MOBA_GROUP
    nb = max(-(-lk // MOBA_BLOCK), MOBA_TOPK)
    pad = nb * MOBA_BLOCK - lk
    kp = jnp.pad(k, ((0, pad), (0, 0), (0, 0)))
    vp = jnp.pad(v, ((0, pad), (0, 0), (0, 0)))
    kb = kp.reshape(nb, MOBA_BLOCK, G, HEAD_DIM).transpose(2, 0, 1, 3)
    vb = vp.reshape(nb, MOBA_BLOCK, G, HEAD_DIM).transpose(2, 0, 1, 3)
    kmean = jnp.mean(kb.astype(jnp.float32), axis=2)
    qc = min(Q_BLOCK, lq)
    n_chunks = lq // qc
    qs = q.reshape(n_chunks, qc, G, R, HEAD_DIM)
    starts = pos0 + jnp.arange(n_chunks, dtype=jnp.int32) * qc
    sl = slopes.reshape(G, R)
    scale = HEAD_DIM ** -0.5
    offs = jnp.arange(MOBA_BLOCK, dtype=jnp.int32)
    g_idx = jnp.arange(G)[None, :, None]
    blk_ids = jnp.arange(nb, dtype=jnp.int32)

    def one_chunk(args):
        qi, start = args
        qpos = start + jnp.arange(qc, dtype=jnp.int32)
        own = start // MOBA_BLOCK
        gate = jnp.einsum('qgrd,gbd->qgb', qi.astype(jnp.float32), kmean)
        past = blk_ids < own
        gate = jnp.where(past[None, None, :], gate, -jnp.inf)
        _, sel = lax.top_k(gate, MOBA_TOPK)
        sel_ok = sel < own
        ksel = kb[g_idx, sel]
        vsel = vb[g_idx, sel]
        kown = lax.dynamic_slice_in_dim(kp, own * MOBA_BLOCK, MOBA_BLOCK, axis=0)
        vown = lax.dynamic_slice_in_dim(vp, own * MOBA_BLOCK, MOBA_BLOCK, axis=0)
        s_sel = jnp.einsum('qgrd,qgtkd->qgrtk', qi, ksel).astype(jnp.float32) * scale
        kpos_sel = sel[..., None] * MOBA_BLOCK + offs
        dist_sel = (qpos[:, None, None, None] - kpos_sel).astype(jnp.float32)
        s_sel = s_sel - sl[None, :, :, None, None] * dist_sel[:, :, None, :, :]
        s_sel = jnp.where(sel_ok[:, :, None, :, None], s_sel, NEG_INF)
        s_own = jnp.einsum('qgrd,kgd->qgrk', qi, kown).astype(jnp.float32) * scale
        dist_own = (qpos[:, None] - (own * MOBA_BLOCK + offs)[None, :]).astype(jnp.float32)
        s_own = s_own - sl[None, :, :, None] * dist_own[:, None, None, :]
        s_own = jnp.where((dist_own >= 0)[:, None, None, :], s_own, NEG_INF)
        s_all = jnp.concatenate([s_sel.reshape(qc, G, R, MOBA_TOPK * MOBA_BLOCK), s_own], axis=-1)
        p = jax.nn.softmax(s_all, axis=-1).astype(v.dtype)
        p_sel = p[..., :MOBA_TOPK * MOBA_BLOCK].reshape(qc, G, R, MOBA_TOPK, MOBA_BLOCK)
        p_own = p[..., MOBA_TOPK * MOBA_BLOCK:]
        o = jnp.einsum('qgrtk,qgtkd->qgrd', p_sel, vsel) + jnp.einsum('qgrk,kgd->qgrd', p_own, vown)
        return o.reshape(qc, MOBA_HEADS, HEAD_DIM)

    out = lax.map(one_chunk, (qs, starts))
    return out.reshape(lq, MOBA_HEADS, HEAD_DIM)


def _moba_mixer(h, w_qkv, w_o, past_k, past_v, pos0, slopes):
    b, l, _ = h.shape
    kvw = MOBA_KV_HEADS * HEAD_DIM
    qkv = jnp.einsum('bld,de->ble', h, w_qkv)
    q = qkv[..., :D_MODEL].reshape(b, l, MOBA_HEADS, HEAD_DIM)
    k = qkv[..., D_MODEL:D_MODEL + kvw].reshape(b, l, MOBA_KV_HEADS, HEAD_DIM)
    v = qkv[..., D_MODEL + kvw:].reshape(b, l, MOBA_KV_HEADS, HEAD_DIM)
    if past_k is None:
        k_all, v_all = k, v
    else:
        k_all = jnp.concatenate([past_k, k], axis=1)
        v_all = jnp.concatenate([past_v, v], axis=1)
    o = lax.map(lambda a: _moba_sequence(a[0], a[1], a[2], pos0, slopes), (q, k_all, v_all))
    y = jnp.einsum('ble,ed->bld', o.reshape(b, l, D_MODEL), w_o)
    return y, k, v


def _diff_mixer(h, w_qkv, w_o, lq1, lk1, lq2, lk2, subln_w, lambda_init, past_k, past_v, pos0, slopes):
    b, l, _ = h.shape
    qkv = jnp.einsum('bld,de->ble', h, w_qkv)
    q = qkv[..., :D_MODEL].reshape(b, l, DIFF_HEADS, 2, HEAD_DIM)
    k = qkv[..., D_MODEL:2 * D_MODEL].reshape(b, l, DIFF_HEADS, 2 * HEAD_DIM)
    v = qkv[..., 2 * D_MODEL:].reshape(b, l, DIFF_HEADS, 2 * HEAD_DIM)
    if past_k is None:
        k_all, v_all = k, v
    else:
        k_all = jnp.concatenate([past_k, k], axis=1)
        v_all = jnp.concatenate([past_v, v], axis=1)
    lam = (jnp.exp(jnp.sum(lq1.astype(jnp.float32) * lk1.astype(jnp.float32)))
           - jnp.exp(jnp.sum(lq2.astype(jnp.float32) * lk2.astype(jnp.float32))) + lambda_init)
    lk_ = k_all.shape[1]
    k4 = k_all.reshape(b, lk_, DIFF_HEADS, 2, HEAD_DIM)
    qc = min(Q_BLOCK, l)
    nblk = l // qc
    qs = q.reshape(b, nblk, qc, DIFF_HEADS, 2, HEAD_DIM).transpose(1, 0, 2, 3, 4, 5)
    starts = pos0 + jnp.arange(nblk, dtype=jnp.int32) * qc
    kpos = jnp.arange(lk_, dtype=jnp.int32)
    scale = HEAD_DIM ** -0.5

    def one_block(args):
        qi, start = args
        qpos = start + jnp.arange(qc, dtype=jnp.int32)
        dist = (qpos[:, None] - kpos[None, :]).astype(jnp.float32)
        s = jnp.einsum('bqhmd,bkhmd->bhmqk', qi, k4).astype(jnp.float32) * scale
        s = s - slopes[None, :, None, None, None] * dist
        s = jnp.where(dist >= 0, s, NEG_INF)
        p = jax.nn.softmax(s, axis=-1)
        a = (p[:, :, 0] - lam * p[:, :, 1]).astype(v_all.dtype)
        return jnp.einsum('bhqk,bkhe->bqhe', a, v_all)

    o = lax.map(one_block, (qs, starts))
    o = o.transpose(1, 0, 2, 3, 4).reshape(b, l, DIFF_HEADS, 2 * HEAD_DIM)
    o = _rmsnorm(o, subln_w, SUBLN_EPS) * (1.0 - lambda_init)
    y = jnp.einsum('ble,ed->bld', o.reshape(b, l, D_MODEL), w_o)
    return y, k, v


def _hier_moe(h, w_group, b_group, w_expert, b_expert, w_gate, w_up, w_down):
    shp = h.shape
    t = h.reshape(-1, D_MODEL)
    n = t.shape[0]
    g_logits = jnp.einsum('nd,dg->ng', t, w_group).astype(jnp.float32) + b_group.astype(jnp.float32)
    g_prob = jax.nn.softmax(g_logits, axis=-1)
    g_sel = jnp.argmax(g_logits, axis=-1)
    g_w = jnp.take_along_axis(g_prob, g_sel[:, None], axis=-1)
    e_logits = jnp.einsum('nd,de->ne', t, w_expert).astype(jnp.float32) + b_expert.astype(jnp.float32)
    e_logits = e_logits.reshape(n, N_GROUPS, EXPERTS_PER_GROUP)
    e_in = jnp.take_along_axis(e_logits, g_sel[:, None, None], axis=1)[:, 0]
    top_v, top_i = lax.top_k(e_in, TOP_EXPERTS)
    weights = g_w * jax.nn.softmax(top_v, axis=-1)
    expert_id = g_sel[:, None] * EXPERTS_PER_GROUP + top_i
    combine = jnp.sum(jax.nn.one_hot(expert_id, N_EXPERTS, dtype=jnp.float32) * weights[..., None], axis=1)
    hid = jax.nn.silu(jnp.einsum('nd,edf->nef', t, w_gate)) * jnp.einsum('nd,edf->nef', t, w_up)
    hid = hid * combine.astype(hid.dtype)[..., None]
    out = jnp.einsum('nef,efd->nd', hid, w_down)
    return out.reshape(shp)


def setup_inputs(seed: int = 0) -> dict:
    key = jax.random.key(seed)
    ks = jax.random.split(key, 26)
    n_pages = PAST_LEN // PAGE_SIZE
    n_pool = (DEC_BATCH * n_pages * 5) // 4

    def nrm(k, shape, scale):
        return jax.random.normal(k, shape, jnp.float32) * scale

    perm = jax.random.permutation(ks[6], n_pool)
    page_table = perm[:DEC_BATCH * n_pages].reshape(DEC_BATCH, n_pages).astype(jnp.int32)
    d_in = D_MODEL ** -0.5
    return {
        'x_prompt': nrm(ks[0], (BATCH, SEQ, D_MODEL), 1.0),
        'x_sample': nrm(ks[1], (DEC_BATCH, DEC_SEQ, D_MODEL), 1.0),
        'cache_k_moba': nrm(ks[2], (N_MOBA_LAYERS, n_pool, PAGE_SIZE, MOBA_KV_HEADS, HEAD_DIM), 1.0),
        'cache_v_moba': nrm(ks[3], (N_MOBA_LAYERS, n_pool, PAGE_SIZE, MOBA_KV_HEADS, HEAD_DIM), 1.0),
        'cache_k_diff': nrm(ks[4], (N_DIFF_LAYERS, n_pool, PAGE_SIZE, DIFF_HEADS, 2 * HEAD_DIM), 1.0),
        'cache_v_diff': nrm(ks[5], (N_DIFF_LAYERS, n_pool, PAGE_SIZE, DIFF_HEADS, 2 * HEAD_DIM), 1.0),
        'page_table': page_table,
        'norm_attn': 1.0 + nrm(ks[7], (DEPTH, D_MODEL), 0.02),
        'norm_ffn': 1.0 + nrm(ks[8], (DEPTH, D_MODEL), 0.02),
        'norm_final': 1.0 + nrm(ks[9], (D_MODEL,), 0.02),
        'moba_w_qkv': nrm(ks[10], (N_MOBA_LAYERS, D_MODEL, D_MODEL + 2 * MOBA_KV_HEADS * HEAD_DIM), d_in),
        'moba_w_o': nrm(ks[11], (N_MOBA_LAYERS, D_MODEL, D_MODEL), d_in),
        'diff_w_qkv': nrm(ks[12], (N_DIFF_LAYERS, D_MODEL, 3 * D_MODEL), d_in),
        'diff_w_o': nrm(ks[13], (N_DIFF_LAYERS, D_MODEL, D_MODEL), d_in),
        'diff_lambda_q1': nrm(ks[14], (N_DIFF_LAYERS, HEAD_DIM), 0.1),
        'diff_lambda_k1': nrm(ks[15], (N_DIFF_LAYERS, HEAD_DIM), 0.1),
        'diff_lambda_q2': nrm(ks[16], (N_DIFF_LAYERS, HEAD_DIM), 0.1),
        'diff_lambda_k2': nrm(ks[17], (N_DIFF_LAYERS, HEAD_DIM), 0.1),
        'diff_subln': 1.0 + nrm(ks[18], (N_DIFF_LAYERS, 2 * HEAD_DIM), 0.02),
        'moe_w_group': nrm(ks[19], (DEPTH, D_MODEL, N_GROUPS), d_in),
        'moe_b_group': nrm(ks[20], (DEPTH, N_GROUPS), 0.01),
        'moe_w_expert': nrm(ks[21], (DEPTH, D_MODEL, N_EXPERTS), d_in),
        'moe_b_expert': nrm(ks[22], (DEPTH, N_EXPERTS), 0.01),
        'moe_w_gate': nrm(ks[23], (DEPTH, N_EXPERTS, D_MODEL, D_EXPERT), d_in),
        'moe_w_up': nrm(ks[24], (DEPTH, N_EXPERTS, D_MODEL, D_EXPERT), d_in),
        'moe_w_down': nrm(ks[25], (DEPTH, N_EXPERTS, D_EXPERT, D_MODEL), D_EXPERT ** -0.5),
    }


def reference(x_prompt, x_sample, cache_k_moba, cache_v_moba, cache_k_diff, cache_v_diff, page_table,
              norm_attn, norm_ffn, norm_final, moba_w_qkv, moba_w_o, diff_w_qkv, diff_w_o,
              diff_lambda_q1, diff_lambda_k1, diff_lambda_q2, diff_lambda_k2, diff_subln,
              moe_w_group, moe_b_group, moe_w_expert, moe_b_expert, moe_w_gate, moe_w_up, moe_w_down):
    slopes_moba = _alibi_slopes(MOBA_HEADS)
    slopes_diff = _alibi_slopes(DIFF_HEADS)
    n_pages = PAST_LEN // PAGE_SIZE

    def gather_past(pool):
        rows = pool[page_table]
        return rows.reshape(DEC_BATCH, n_pages * PAGE_SIZE, pool.shape[2], pool.shape[3])

    yp, ys = x_prompt, x_sample
    km_p, vm_p, kd_p, vd_p = [], [], [], []
    km_s, vm_s, kd_s, vd_s = [], [], [], []
    for i in range(DEPTH):
        j = i // 2
        hp = _rmsnorm(yp, norm_attn[i], NORM_EPS)
        hs = _rmsnorm(ys, norm_attn[i], NORM_EPS)
        if i % 2 == 0:
            op, kp_, vp_ = _moba_mixer(hp, moba_w_qkv[j], moba_w_o[j], None, None, 0, slopes_moba)
            os_, ks_, vs_ = _moba_mixer(hs, moba_w_qkv[j], moba_w_o[j], gather_past(cache_k_moba[j]),
                                        gather_past(cache_v_moba[j]), PAST_LEN, slopes_moba)
            km_p.append(kp_); vm_p.append(vp_); km_s.append(ks_); vm_s.append(vs_)
        else:
            lambda_init = 0.8 - 0.6 * math.exp(-0.3 * i)
            op, kp_, vp_ = _diff_mixer(hp, diff_w_qkv[j], diff_w_o[j], diff_lambda_q1[j], diff_lambda_k1[j],
                                       diff_lambda_q2[j], diff_lambda_k2[j], diff_subln[j], lambda_init,
                                       None, None, 0, slopes_diff)
            os_, ks_, vs_ = _diff_mixer(hs, diff_w_qkv[j], diff_w_o[j], diff_lambda_q1[j], diff_lambda_k1[j],
                                        diff_lambda_q2[j], diff_lambda_k2[j], diff_subln[j], lambda_init,
                                        gather_past(cache_k_diff[j]), gather_past(cache_v_diff[j]),
                                        PAST_LEN, slopes_diff)
            kd_p.append(kp_); vd_p.append(vp_); kd_s.append(ks_); vd_s.append(vs_)
        yp = yp + op
        ys = ys + os_
        yp = yp + _hier_moe(_rmsnorm(yp, norm_ffn[i], NORM_EPS), moe_w_group[i], moe_b_group[i],
                            moe_w_expert[i], moe_b_expert[i], moe_w_gate[i], moe_w_up[i], moe_w_down[i])
        ys = ys + _hier_moe(_rmsnorm(ys, norm_ffn[i], NORM_EPS), moe_w_group[i], moe_b_group[i],
                            moe_w_expert[i], moe_b_expert[i], moe_w_gate[i], moe_w_up[i], moe_w_down[i])
    y_prompt = _rmsnorm(yp, norm_final, NORM_EPS)
    y_sample = _rmsnorm(ys, norm_final, NORM_EPS)
    k_moba_prompt = jnp.stack(km_p)
    v_moba_prompt = jnp.stack(vm_p)
    k_diff_prompt = jnp.stack(kd_p)
    v_diff_prompt = jnp.stack(vd_p)
    k_moba_sample = jnp.stack(km_s)
    v_moba_sample = jnp.stack(vm_s)
    k_diff_sample = jnp.stack(kd_s)
    v_diff_sample = jnp.stack(vd_s)
    return (y_prompt, y_sample, k_moba_prompt, v_moba_prompt, k_diff_prompt, v_diff_prompt,
            k_moba_sample, v_moba_sample, k_diff_sample, v_diff_sample)
```

```python
import functools
import math

import jax
import jax.numpy as jnp
import numpy as np
from jax import lax
from jax.experimental import pallas as pl
from jax.experimental.pallas import tpu as pltpu

D_MODEL = 1024
HEAD_DIM = 64
HEAD_DIM_LOG2 = 6
MOBA_HEADS = D_MODEL // HEAD_DIM
MOBA_KV_HEADS = 4
MOBA_GROUP = MOBA_HEADS // MOBA_KV_HEADS
MOBA_BLOCK = 256
MOBA_BLOCK_LOG2 = 8
MOBA_TOPK = 3
MOBA_KVW = MOBA_KV_HEADS * HEAD_DIM
DIFF_HEADS = D_MODEL // (2 * HEAD_DIM)
N_GROUPS = 4
EXPERTS_PER_GROUP = 4
N_EXPERTS = N_GROUPS * EXPERTS_PER_GROUP
D_EXPERT = 256
PAGE_SIZE = 128
NORM_EPS = 1e-6
SUBLN_EPS = 1e-5
NEG_INF = -1e30
SCALE = HEAD_DIM ** -0.5

LANES = 128
ROUTER_LANES = LANES
VMEM_LIMIT = 56 * 1024 * 1024

F32 = jnp.float32
BF16 = jnp.bfloat16
NT_DIMS = (((1,), (1,)), ((), ()))


def _alibi_slopes(n_heads):
    start = 2.0 ** (-8.0 / n_heads)
    return np.array([start ** (i + 1) for i in range(n_heads)], dtype=np.float32)


def _params(*sem):
    return pltpu.CompilerParams(dimension_semantics=sem, vmem_limit_bytes=VMEM_LIMIT)


def _resident(shape):
    nd = len(shape)
    return pl.BlockSpec(shape, lambda *_: (0,) * nd, pipeline_mode=pl.Buffered(1))


def _rms(x, w, eps):
    ms = jnp.mean(x * x, axis=-1, keepdims=True)
    return x * lax.rsqrt(ms + eps) * w


def _norm_qkv_kernel(x_ref, nw_ref, w_ref, *refs, mode, kvw, tm):
    h = _rms(x_ref[...], nw_ref[...], NORM_EPS)
    if w_ref.dtype == F32:
        qkv = jnp.dot(h, w_ref[...], precision=lax.Precision.HIGHEST, preferred_element_type=F32)
    else:
        h = h.astype(BF16)
        qkv = jnp.dot(h, w_ref[...], preferred_element_type=F32)
    q = qkv[:, :D_MODEL]
    k = qkv[:, D_MODEL:D_MODEL + kvw]
    v = qkv[:, D_MODEL + kvw:]
    if mode == "plain":
        q_ref, k_ref, v_ref = refs
    elif mode == "moba":
        wkt_ref, q_ref, k_ref, v_ref, kt_ref, vh_ref, km_ref = refs
        kt = lax.dot_general(wkt_ref[...], h, NT_DIMS, preferred_element_type=F32)
        vb = v.astype(BF16)
        for jj in range(tm // MOBA_BLOCK):
            sl = slice(jj * MOBA_BLOCK, (jj + 1) * MOBA_BLOCK)
            kt_ref[0, jj] = kt[:, sl].astype(BF16)
            km_ref[jj] = jnp.mean(k[sl], axis=0, keepdims=True)
        for g in range(MOBA_KV_HEADS):
            vh_ref[0, g] = vb[:, g * HEAD_DIM:(g + 1) * HEAD_DIM]
    else:
        q_ref, k_ref, v_ref, kb_ref, vb_ref = refs
        kb_ref[...] = k.astype(BF16)
        vb_ref[...] = v.astype(BF16)
    q_ref[...] = q
    k_ref[...] = k
    v_ref[...] = v


def _norm_qkv(x, nw, w_bf, *, mode, seq_len=None, wkt_bf=None):
    n = x.shape[0]
    e = w_bf.shape[1]
    kvw = (e - D_MODEL) // 2
    tm = min(512, n)
    nt = n // tm
    row = lambda t: (t, 0)
    in_specs = [pl.BlockSpec((tm, D_MODEL), row), _resident((1, D_MODEL)), _resident((D_MODEL, e))]
    args = [x, nw.reshape(1, D_MODEL), w_bf]
    out_shape = [jax.ShapeDtypeStruct((n, D_MODEL), F32),
                 jax.ShapeDtypeStruct((n, kvw), F32),
                 jax.ShapeDtypeStruct((n, kvw), F32)]
    out_specs = [pl.BlockSpec((tm, D_MODEL), row), pl.BlockSpec((tm, kvw), row),
                 pl.BlockSpec((tm, kvw), row)]
    if mode == "moba":
        b = n // seq_len
        tps = seq_len // tm
        bpt = tm // MOBA_BLOCK
        nb = seq_len // MOBA_BLOCK
        in_specs.append(_resident((kvw, D_MODEL)))
        args.append(wkt_bf)
        out_shape += [jax.ShapeDtypeStruct((b, nb, kvw, MOBA_BLOCK), BF16),
                      jax.ShapeDtypeStruct((b, MOBA_KV_HEADS, seq_len, HEAD_DIM), BF16),
                      jax.ShapeDtypeStruct((b * nb, 1, kvw), F32)]
        out_specs += [pl.BlockSpec((1, bpt, kvw, MOBA_BLOCK), lambda t: (t // tps, t % tps, 0, 0)),
                      pl.BlockSpec((1, MOBA_KV_HEADS, tm, HEAD_DIM), lambda t: (t // tps, 0, t % tps, 0)),
                      pl.BlockSpec((bpt, 1, kvw), lambda t: (t, 0, 0))]
    elif mode == "diff":
        out_shape += [jax.ShapeDtypeStruct((n, kvw), BF16), jax.ShapeDtypeStruct((n, kvw), BF16)]
        out_specs += [pl.BlockSpec((tm, kvw), row), pl.BlockSpec((tm, kvw), row)]
    return pl.pallas_call(
        functools.partial(_norm_qkv_kernel, mode=mode, kvw=kvw, tm=tm),
        grid=(nt,), in_specs=in_specs, out_specs=out_specs, out_shape=out_shape,
        compiler_params=_params("parallel"), name="norm_qkv_" + mode,
    )(*args)


def _softmax_update(s, m_ref, l_ref, rows):
    width = s.shape[1]
    m_prev = m_ref[rows, :]
    m_new = jnp.maximum(m_prev, jnp.max(s, axis=1, keepdims=True))
    alpha = jnp.exp(m_prev - m_new)
    p = jnp.exp(s - jnp.concatenate([m_new] * (width // LANES), axis=1))
    l_ref[rows, :] = alpha * l_ref[rows, :] + jnp.sum(p, axis=1, keepdims=True)
    m_ref[rows, :] = m_new
    return p, alpha


def _top3_mask(gate, lane, n_valid):
    gt = jnp.where(lane < n_valid, gate, -jnp.inf)
    sel = jnp.zeros(gate.shape, F32)
    for _ in range(MOBA_TOPK):
        mx = jnp.max(gt, axis=1, keepdims=True)
        idx = jnp.min(jnp.where(gt == mx, lane, gate.shape[1]), axis=1, keepdims=True)
        hit = (lane == idx) & (mx > -jnp.inf)
        sel = jnp.where(hit, 1.0, sel)
        gt = jnp.where(hit, -jnp.inf, gt)
    return sel


def _moba_prompt_kernel(slopes_ref, q_ref, km_ref, kt_ref, v_ref, o_ref,
                        qz_ref, sel_ref, m_ref, l_ref, acc_ref, p_ref):
    tq = MOBA_BLOCK
    g = pl.program_id(1)
    i = pl.program_id(2)
    q = q_ref[...]
    gate = jnp.dot(q, km_ref[0, 0], precision=lax.Precision.HIGHEST,
                   preferred_element_type=F32)
    lane = lax.broadcasted_iota(jnp.int32, (tq, LANES), 1)
    sel_ref[...] = _top3_mask(gate, lane, i)
    qs = q * SCALE
    qlane = lax.shift_right_logical(lax.broadcasted_iota(jnp.int32, qs.shape, 1), HEAD_DIM_LOG2)
    for r in range(MOBA_GROUP):
        qz_ref[r * tq:(r + 1) * tq, :] = jnp.where(qlane == r, qs, 0.0).astype(BF16)
    m_ref[...] = jnp.full(m_ref.shape, -jnp.inf, F32)
    l_ref[...] = jnp.zeros(l_ref.shape, F32)
    acc_ref[...] = jnp.zeros(acc_ref.shape, F32)
    rel = (lax.broadcasted_iota(jnp.int32, (tq, tq), 1)
           - lax.broadcasted_iota(jnp.int32, (tq, tq), 0)).astype(F32)

    def attend(j, ok):
        kt4 = jnp.concatenate([kt_ref[0, j]] * MOBA_GROUP, axis=0)
        s = jnp.dot(qz_ref[...], kt4, preferred_element_type=F32)
        dist = rel + ((j - i) * tq).astype(F32)
        for r in range(MOBA_GROUP):
            rows = slice(r * tq, (r + 1) * tq)
            sr = s[rows] + slopes_ref[g * MOBA_GROUP + r] * dist
            sr = jnp.where(ok, sr, NEG_INF)
            p, alpha = _softmax_update(sr, m_ref, l_ref, rows)
            acc_ref[rows, :] = acc_ref[rows, :] * alpha[:, :HEAD_DIM]
            p_ref[rows, :] = p.astype(BF16)
        vj = v_ref[0, 0, pl.ds(pl.multiple_of(j * tq, tq), tq), :]
        acc_ref[...] += jnp.dot(p_ref[...], vj, preferred_element_type=F32)

    def past_block(j, carry):
        selcol = jnp.sum(jnp.where(lane == j, sel_ref[...], 0.0), axis=1, keepdims=True)

        @pl.when(jnp.max(selcol) > 0.0)
        def _():
            attend(j, selcol > 0.0)
        return carry

    lax.fori_loop(0, i, past_block, 0)
    attend(i, rel <= 0.0)
    out = [acc_ref[r * tq:(r + 1) * tq, :] / l_ref[r * tq:(r + 1) * tq, :HEAD_DIM]
           for r in range(MOBA_GROUP)]
    o_ref[...] = jnp.concatenate(out, axis=1).astype(BF16)


def _moba_prompt(q, km_rep, kt, vh, slopes):
    b, nb = kt.shape[0], kt.shape[1]
    seq_len = nb * MOBA_BLOCK
    tq = MOBA_BLOCK
    gw = MOBA_GROUP * HEAD_DIM
    return pl.pallas_call(
        _moba_prompt_kernel,
        grid=(b, MOBA_KV_HEADS, nb),
        in_specs=[pl.BlockSpec(memory_space=pltpu.SMEM),
                  pl.BlockSpec((tq, gw), lambda bi, g, i: (bi * nb + i, g)),
                  pl.BlockSpec((1, 1, gw, LANES), lambda bi, g, i: (bi, g, 0, 0)),
                  pl.BlockSpec((1, nb, HEAD_DIM, MOBA_BLOCK), lambda bi, g, i: (bi, 0, g, 0)),
                  pl.BlockSpec((1, 1, seq_len, HEAD_DIM), lambda bi, g, i: (bi, g, 0, 0))],
        out_specs=pl.BlockSpec((tq, gw), lambda bi, g, i: (bi * nb + i, g)),
        out_shape=jax.ShapeDtypeStruct((b * seq_len, D_MODEL), BF16),
        scratch_shapes=[pltpu.VMEM((MOBA_GROUP * tq, gw), BF16),
                        pltpu.VMEM((tq, LANES), F32),
                        pltpu.VMEM((MOBA_GROUP * tq, LANES), F32),
                        pltpu.VMEM((MOBA_GROUP * tq, LANES), F32),
                        pltpu.VMEM((MOBA_GROUP * tq, HEAD_DIM), F32),
                        pltpu.VMEM((MOBA_GROUP * tq, MOBA_BLOCK), BF16)],
        compiler_params=_params("parallel", "parallel", "arbitrary"), name="moba_prompt",
    )(slopes, q, km_rep, kt, vh)


def _lambda_full(lq1_ref, lk1_ref, lq2_ref, lk2_ref, lambda_init):
    a = jnp.sum(lq1_ref[...] * lk1_ref[...], axis=1, keepdims=True)
    b = jnp.sum(lq2_ref[...] * lk2_ref[...], axis=1, keepdims=True)
    return jnp.exp(a) - jnp.exp(b) + lambda_init


def _subln(o, w, lambda_init):
    return _rms(o, w, SUBLN_EPS) * (1.0 - lambda_init)


def _diff_prompt_kernel(slopes_ref, q_ref, k_ref, v_ref, lq1_ref, lk1_ref, lq2_ref, lk2_ref,
                        sw_ref, o_ref, qz_ref, m_ref, l_ref, acc_ref, *, tq, lambda_init):
    hd = pl.program_id(1)
    i = pl.program_id(2)
    qs = q_ref[...] * SCALE
    qlane = lax.shift_right_logical(lax.broadcasted_iota(jnp.int32, qs.shape, 1), HEAD_DIM_LOG2)
    for mp in range(2):
        qz_ref[mp * tq:(mp + 1) * tq, :] = jnp.where(qlane == mp, qs, 0.0).astype(BF16)
    m_ref[...] = jnp.full(m_ref.shape, -jnp.inf, F32)
    l_ref[...] = jnp.zeros(l_ref.shape, F32)
    acc_ref[...] = jnp.zeros(acc_ref.shape, F32)
    rows2 = lax.broadcasted_iota(jnp.int32, (2 * tq, tq), 0)
    rel = (lax.broadcasted_iota(jnp.int32, (2 * tq, tq), 1)
           - jnp.where(rows2 >= tq, rows2 - tq, rows2)).astype(F32)
    slope = slopes_ref[hd]
    everything = slice(None)

    def attend(j, causal):
        ks = pl.ds(pl.multiple_of(j * tq, tq), tq)
        s = lax.dot_general(qz_ref[...], k_ref[ks, :], NT_DIMS, preferred_element_type=F32)
        s = s + slope * (rel + ((j - i) * tq).astype(F32))
        if causal:
            s = jnp.where(rel <= 0.0, s, NEG_INF)
        p, alpha = _softmax_update(s, m_ref, l_ref, everything)
        acc_ref[...] = acc_ref[...] * alpha + jnp.dot(p.astype(BF16), v_ref[ks, :],
                                                      preferred_element_type=F32)

    def past_block(j, carry):
        attend(j, False)
        return carry

    lax.fori_loop(0, i, past_block, 0)
    attend(i, True)
    o1 = acc_ref[:tq, :] / l_ref[:tq, :]
    o2 = acc_ref[tq:, :] / l_ref[tq:, :]
    lam = _lambda_full(lq1_ref, lk1_ref, lq2_ref, lk2_ref, lambda_init)
    o_ref[...] = _subln(o1 - lam * o2, sw_ref[...], lambda_init).astype(BF16)


def _diff_prompt(q, kb, vb, lams, subln_w, slopes, *, batch, lambda_init):
    n = q.shape[0]
    seq_len = n // batch
    tq = 256
    nq = seq_len // tq
    hw = 2 * HEAD_DIM
    lam_specs = [_resident((1, HEAD_DIM))] * 4
    return pl.pallas_call(
        functools.partial(_diff_prompt_kernel, tq=tq, lambda_init=lambda_init),
        grid=(batch, DIFF_HEADS, nq),
        in_specs=[pl.BlockSpec(memory_space=pltpu.SMEM),
                  pl.BlockSpec((tq, hw), lambda bi, h, i: (bi * nq + i, h)),
                  pl.BlockSpec((seq_len, hw), lambda bi, h, i: (bi, h)),
                  pl.BlockSpec((seq_len, hw), lambda bi, h, i: (bi, h)),
                  *lam_specs, _resident((1, hw))],
        out_specs=pl.BlockSpec((tq, hw), lambda bi, h, i: (bi * nq + i, h)),
        out_shape=jax.ShapeDtypeStruct((n, D_MODEL), BF16),
        scratch_shapes=[pltpu.VMEM((2 * tq, hw), BF16),
                        pltpu.VMEM((2 * tq, LANES), F32),
                        pltpu.VMEM((2 * tq, LANES), F32),
                        pltpu.VMEM((2 * tq, hw), F32)],
        compiler_params=_params("parallel", "parallel", "arbitrary"), name="diff_prompt",
    )(slopes, q, kb, vb, *lams, subln_w)


def _page_specs(width, pages_per_step):
    def spec(t):
        return pl.BlockSpec((1, PAGE_SIZE, width),
                            lambda b, s, pt: (pt[b, s * pages_per_step + t], 0, 0))
    return [spec(t) for t in range(pages_per_step)]


def _kmean_pages_kernel(pt_ref, *refs):
    k_refs, o_ref = refs[:-1], refs[-1]
    rows = []
    for t in range(0, len(k_refs), MOBA_BLOCK // PAGE_SIZE):
        tot = sum(jnp.sum(k_refs[t + u][0], axis=0, keepdims=True)
                  for u in range(MOBA_BLOCK // PAGE_SIZE))
        rows.append(tot / MOBA_BLOCK)
    o_ref[0] = jnp.concatenate(rows, axis=0)


def _kmean_pages(pool, page_table):
    db, n_pages = page_table.shape
    pps = 16
    ppb = MOBA_BLOCK // PAGE_SIZE
    width = pool.shape[-1]
    grid_spec = pltpu.PrefetchScalarGridSpec(
        num_scalar_prefetch=1, grid=(db, n_pages // pps),
        in_specs=_page_specs(width, pps),
        out_specs=pl.BlockSpec((1, pps // ppb, width), lambda b, s, pt: (b, s, 0)))
    return pl.pallas_call(
        _kmean_pages_kernel, grid_spec=grid_spec,
        out_shape=jax.ShapeDtypeStruct((db, n_pages // ppb, width), F32),
        compiler_params=_params("parallel", "arbitrary"), name="kmean_pages",
    )(page_table, *([pool] * pps))


def _decode_attn_kernel(pt_ref, *refs, pps, mode, past_len, lambda_init):
    n_kv = 2 * pps
    if mode == "moba":
        (qbd_ref, slope_ref, qpos_ref, knew_ref, vnew_ref, qf_ref, km_ref), refs = refs[:7], refs[7:]
    else:
        (qbd_ref, slope_ref, qpos_ref, knew_ref, vnew_ref,
         lq1_ref, lk1_ref, lq2_ref, lk2_ref, sw_ref), refs = refs[:10], refs[10:]
    k_refs, v_refs = refs[:pps], refs[pps:n_kv]
    o_ref = refs[n_kv]
    if mode == "moba":
        m_ref, l_ref, acc_ref, sel_ref = refs[n_kv + 1:]
    else:
        m_ref, l_ref, acc_ref = refs[n_kv + 1:]
    step = pl.program_id(1)
    n_rows = qbd_ref.shape[1]
    chunk = pps * PAGE_SIZE
    everything = slice(None)

    @pl.when(step == 0)
    def _():
        m_ref[...] = jnp.full(m_ref.shape, -jnp.inf, F32)
        l_ref[...] = jnp.zeros(l_ref.shape, F32)
        acc_ref[...] = jnp.zeros(acc_ref.shape, F32)
        if mode == "moba":
            gr = lax.dot_general(qf_ref[0], km_ref[0], NT_DIMS, precision=lax.Precision.HIGHEST,
                                 preferred_element_type=F32)
            n_blocks = past_len // MOBA_BLOCK
            rpg = n_rows // MOBA_KV_HEADS
            dq = rpg // MOBA_GROUP
            lane = lax.broadcasted_iota(jnp.int32, (dq, LANES), 1)
            for g in range(MOBA_KV_HEADS):
                gs = sum(gr[g * rpg + r * dq:g * rpg + (r + 1) * dq] for r in range(MOBA_GROUP))
                sel = _top3_mask(gs, lane, n_blocks)
                sel_ref[g * rpg:(g + 1) * rpg, :] = jnp.concatenate([sel] * MOBA_GROUP, axis=0)

    precise = mode == "moba"

    def nt_dot(a, b):
        return lax.dot_general(a, b, NT_DIMS, preferred_element_type=F32)

    def qk(k32):
        if not precise:
            return nt_dot(qbd_ref[0], k32.astype(BF16))
        qh, ql = _split_bf16(qbd_ref[0])
        kh, kl = _split_bf16(k32)
        s2 = nt_dot(jnp.concatenate([qh, ql], axis=0), kh)
        return s2[:n_rows] + s2[n_rows:] + nt_dot(qh, kl)

    def pv(p, v32):
        if not precise:
            return jnp.dot(p.astype(BF16), v32.astype(BF16), preferred_element_type=F32)
        ph, plo = _split_bf16(p)
        vh, vl = _split_bf16(v32)
        o2 = jnp.dot(jnp.concatenate([ph, plo], axis=0), vh, preferred_element_type=F32)
        return o2[:n_rows] + o2[n_rows:] + jnp.dot(ph, vl, preferred_element_type=F32)

    def update(s, ok, v32):
        if ok is not None:
            s = jnp.where(ok, s, NEG_INF)
        p, alpha = _softmax_update(s, m_ref, l_ref, everything)
        width = acc_ref.shape[1]
        acc_ref[...] = (acc_ref[...] * jnp.concatenate([alpha] * (width // LANES), axis=1)
                        + pv(p, v32))

    def scores(k32, kpos):
        return qk(k32) - slope_ref[...] * (qpos_ref[...] - kpos)

    kc = jnp.concatenate([r[0] for r in k_refs], axis=0)
    vc = jnp.concatenate([r[0] for r in v_refs], axis=0)
    key = step * chunk + lax.broadcasted_iota(jnp.int32, (1, chunk), 1)
    s = scores(kc, key.astype(F32))
    if mode == "moba":
        blk = lax.broadcasted_iota(jnp.int32, (LANES, chunk), 0)
        key2 = step * chunk + lax.broadcasted_iota(jnp.int32, (LANES, chunk), 1)
        expand = jnp.where(blk == lax.shift_right_logical(key2, MOBA_BLOCK_LOG2), 1.0, 0.0).astype(BF16)
        ok = jnp.dot(sel_ref[...].astype(BF16), expand, preferred_element_type=F32) > 0.5
    else:
        ok = None
    update(s, ok, vc)

    @pl.when(step == pl.num_programs(1) - 1)
    def _():
        kj = past_len + lax.broadcasted_iota(jnp.int32, (1, PAGE_SIZE), 1)
        kjf = kj.astype(F32)
        s_new = scores(knew_ref[0], kjf)
        update(s_new, kjf <= qpos_ref[...], vnew_ref[0])
        width = acc_ref.shape[1]
        o = acc_ref[...] / jnp.concatenate([l_ref[...]] * (width // LANES), axis=1)
        if mode == "moba":
            o_ref[0] = o
        else:
            dq = n_rows // (2 * DIFF_HEADS)
            hw = 2 * HEAD_DIM
            lam = _lambda_full(lq1_ref, lk1_ref, lq2_ref, lk2_ref, lambda_init)
            outs = []
            for hd in range(DIFF_HEADS):
                o1 = o[(2 * hd) * dq:(2 * hd + 1) * dq, hd * hw:(hd + 1) * hw]
                o2 = o[(2 * hd + 1) * dq:(2 * hd + 2) * dq, hd * hw:(hd + 1) * hw]
                outs.append(_subln(o1 - lam * o2, sw_ref[...], lambda_init))
            o_ref[0] = jnp.concatenate(outs, axis=1)


def _decode_attn(qbd, slope_rows, qpos_rows, k_new, v_new, extras, k_pool, v_pool, page_table,
                 *, mode, pps, lambda_init=0.0):
    db, n_pages = page_table.shape
    n_rows, width = qbd.shape[1], qbd.shape[2]
    per_seq = lambda shape: pl.BlockSpec((1,) + shape, lambda b, s, pt: (b, 0, 0))
    const = lambda shape: pl.BlockSpec(shape, lambda b, s, pt: (0, 0))
    in_specs = [per_seq((n_rows, width)), const((n_rows, 1)), const((n_rows, 1)),
                per_seq((PAGE_SIZE, width)), per_seq((PAGE_SIZE, width))]
    scratch = [pltpu.VMEM((n_rows, LANES), F32), pltpu.VMEM((n_rows, LANES), F32),
               pltpu.VMEM((n_rows, width), F32)]
    if mode == "moba":
        in_specs += [per_seq((n_rows, width)), per_seq((LANES, width))]
        out_shape = jax.ShapeDtypeStruct((db, n_rows, width), F32)
        out_spec = per_seq((n_rows, width))
        scratch.append(pltpu.VMEM((n_rows, LANES), F32))
    else:
        in_specs += [const((1, HEAD_DIM))] * 4 + [const((1, 2 * HEAD_DIM))]
        dq = n_rows // (2 * DIFF_HEADS)
        out_shape = jax.ShapeDtypeStruct((db, dq, D_MODEL), F32)
        out_spec = per_seq((dq, D_MODEL))
    in_specs += _page_specs(width, pps) * 2
    grid_spec = pltpu.PrefetchScalarGridSpec(
        num_scalar_prefetch=1, grid=(db, n_pages // pps),
        in_specs=in_specs, out_specs=out_spec, scratch_shapes=scratch)
    return pl.pallas_call(
        functools.partial(_decode_attn_kernel, pps=pps, mode=mode,
                          past_len=n_pages * PAGE_SIZE, lambda_init=lambda_init),
        grid_spec=grid_spec, out_shape=out_shape,
        compiler_params=_params("parallel", "arbitrary"), name="decode_attn_" + mode,
    )(page_table, qbd, slope_rows, qpos_rows, k_new, v_new, *extras,
      *([k_pool] * pps), *([v_pool] * pps))


def _split_bf16(x):
    hi = x.astype(BF16)
    return hi, (x - hi.astype(F32)).astype(BF16)


def _router_logits(t, wr_hi_ref, wr_lo_ref, br_ref):
    t_hi, t_lo = _split_bf16(t)
    return (jnp.dot(t_hi, wr_hi_ref[...], preferred_element_type=F32)
            + jnp.dot(t_lo, wr_hi_ref[...], preferred_element_type=F32)
            + jnp.dot(t_hi, wr_lo_ref[...], preferred_element_type=F32)) + br_ref[...]


def _router_combine(logits):
    lane = lax.broadcasted_iota(jnp.int32, logits.shape, 1)
    width = logits.shape[1]
    first = lambda hit: jnp.min(jnp.where(hit, lane, width), axis=1, keepdims=True)
    is_g = lane < N_GROUPS
    gl = jnp.where(is_g, logits, -jnp.inf)
    g_max = jnp.max(gl, axis=1, keepdims=True)
    g_sel = first(gl == g_max)
    g_w = 1.0 / jnp.sum(jnp.where(is_g, jnp.exp(logits - g_max), 0.0), axis=1, keepdims=True)
    e_lo = N_GROUPS + g_sel * EXPERTS_PER_GROUP
    el = jnp.where((lane >= e_lo) & (lane < e_lo + EXPERTS_PER_GROUP), logits, -jnp.inf)
    v1 = jnp.max(el, axis=1, keepdims=True)
    i1 = first(el == v1)
    el2 = jnp.where(lane == i1, -jnp.inf, el)
    v2 = jnp.max(el2, axis=1, keepdims=True)
    i2 = first(el2 == v2)
    e2 = jnp.exp(v2 - v1)
    w1 = 1.0 / (1.0 + e2)
    w2 = e2 / (1.0 + e2)
    return jnp.where(lane == i1, g_w * w1, 0.0) + jnp.where(lane == i2, g_w * w2, 0.0)


def _wo_moe_kernel(x_ref, a_ref, wo_ref, nw_ref, wr_hi_ref, wr_lo_ref, br_ref, wgu_ref, wd_ref,
                   fw_ref, o_ref, hid_ref, *, final):
    y1 = x_ref[...] + jnp.dot(a_ref[...], wo_ref[...], preferred_element_type=F32)
    t = _rms(y1, nw_ref[...], NORM_EPS)
    combine = _router_combine(_router_logits(t, wr_hi_ref, wr_lo_ref, br_ref))
    tb = t.astype(BF16)
    for e in range(N_EXPERTS):
        gu = jnp.dot(tb, wgu_ref[e], preferred_element_type=F32)
        gate, up = gu[:, :D_EXPERT], gu[:, D_EXPERT:]
        c = combine[:, N_GROUPS + e:N_GROUPS + e + 1]
        hid = gate * (1.0 / (1.0 + jnp.exp(-gate))) * up * c
        hid_ref[:, e * D_EXPERT:(e + 1) * D_EXPERT] = hid.astype(BF16)
    y2 = y1 + jnp.dot(hid_ref[...], wd_ref[...], preferred_element_type=F32)
    if final:
        y2 = _rms(y2, fw_ref[...], NORM_EPS)
    o_ref[...] = y2


def _wo_moe(x, attn, wo_bf, nw, wr_hi, wr_lo, br, wgu_bf, wd_bf, fw, *, final):
    n = x.shape[0]
    tm = min(512, n)
    row = lambda t: (t, 0)
    return pl.pallas_call(
        functools.partial(_wo_moe_kernel, final=final),
        grid=(n // tm,),
        in_specs=[pl.BlockSpec((tm, D_MODEL), row), pl.BlockSpec((tm, D_MODEL), row),
                  _resident(wo_bf.shape), _resident((1, D_MODEL)),
                  _resident(wr_hi.shape), _resident(wr_lo.shape), _resident(br.shape),
                  _resident(wgu_bf.shape), _resident(wd_bf.shape), _resident((1, D_MODEL))],
        out_specs=pl.BlockSpec((tm, D_MODEL), row),
        out_shape=jax.ShapeDtypeStruct((n, D_MODEL), F32),
        scratch_shapes=[pltpu.VMEM((tm, N_EXPERTS * D_EXPERT), BF16)],
        compiler_params=_params("parallel"), name="wo_moe",
    )(x, attn, wo_bf, nw.reshape(1, D_MODEL), wr_hi, wr_lo, br, wgu_bf, wd_bf,
      fw.reshape(1, D_MODEL))


def _dot_f32(a, b):
    return jnp.dot(a, b, precision=lax.Precision.HIGHEST, preferred_element_type=F32)


def _wo_moe_f32_kernel(x_ref, a_ref, wo_ref, nw_ref, wr_ref, br_ref, wg_ref, wu_ref, wd_ref,
                       fw_ref, o_ref, y1_ref, t_ref, comb_ref, acc_ref, *, final):
    e = pl.program_id(0)

    @pl.when(e == 0)
    def _():
        y1 = x_ref[...] + _dot_f32(a_ref[...], wo_ref[...])
        t = _rms(y1, nw_ref[...], NORM_EPS)
        y1_ref[...] = y1
        t_ref[...] = t
        comb_ref[...] = _router_combine(_dot_f32(t, wr_ref[...]) + br_ref[...])
        acc_ref[...] = jnp.zeros(acc_ref.shape, F32)

    t = t_ref[...]
    gate = _dot_f32(t, wg_ref[0])
    up = _dot_f32(t, wu_ref[0])
    lane = lax.broadcasted_iota(jnp.int32, comb_ref.shape, 1)
    c = jnp.sum(jnp.where(lane == N_GROUPS + e, comb_ref[...], 0.0), axis=1, keepdims=True)
    hid = gate * (1.0 / (1.0 + jnp.exp(-gate))) * up * c
    acc_ref[...] += _dot_f32(hid, wd_ref[0])

    @pl.when(e == pl.num_programs(0) - 1)
    def _():
        y2 = y1_ref[...] + acc_ref[...]
        if final:
            y2 = _rms(y2, fw_ref[...], NORM_EPS)
        o_ref[...] = y2


def _wo_moe_f32(x, attn, wo, nw, wr, br, w_gate, w_up, w_down, fw, *, final):
    n = x.shape[0]
    whole = lambda shape: pl.BlockSpec(shape, lambda e: (0,) * len(shape))
    per_expert = lambda shape: pl.BlockSpec((1,) + shape, lambda e: (e, 0, 0))
    return pl.pallas_call(
        functools.partial(_wo_moe_f32_kernel, final=final),
        grid=(N_EXPERTS,),
        in_specs=[whole((n, D_MODEL)), whole((n, D_MODEL)), whole(wo.shape), whole((1, D_MODEL)),
                  whole(wr.shape), whole(br.shape), per_expert((D_MODEL, D_EXPERT)),
                  per_expert((D_MODEL, D_EXPERT)), per_expert((D_EXPERT, D_MODEL)),
                  whole((1, D_MODEL))],
        out_specs=whole((n, D_MODEL)),
        out_shape=jax.ShapeDtypeStruct((n, D_MODEL), F32),
        scratch_shapes=[pltpu.VMEM((n, D_MODEL), F32), pltpu.VMEM((n, D_MODEL), F32),
                        pltpu.VMEM((n, ROUTER_LANES), F32), pltpu.VMEM((n, D_MODEL), F32)],
        compiler_params=_params("arbitrary"), name="wo_moe_f32",
    )(x, attn, wo, nw.reshape(1, D_MODEL), wr, br, w_gate, w_up, w_down, fw.reshape(1, D_MODEL))


def _moe_weights(w_group, b_group, w_expert, b_expert, w_gate, w_up, w_down):
    n_r = N_GROUPS + N_EXPERTS
    wr = jnp.pad(jnp.concatenate([w_group, w_expert], axis=1), ((0, 0), (0, ROUTER_LANES - n_r)))
    wr_hi = wr.astype(BF16)
    wr_lo = (wr - wr_hi.astype(F32)).astype(BF16)
    br = jnp.pad(jnp.concatenate([b_group, b_expert]), (0, ROUTER_LANES - n_r)).reshape(1, -1)
    wgu = jnp.concatenate([w_gate, w_up], axis=2).astype(BF16)
    wd = w_down.reshape(N_EXPERTS * D_EXPERT, D_MODEL).astype(BF16)
    return (wr_hi, wr_lo, br, wgu, wd), (wr, br)


def _block_diag_rows(q, n_slots):
    db, _, m, dq, w = q.shape
    eye = jnp.eye(n_slots, dtype=q.dtype)
    out = q[:, :, :, :, None, :] * eye[None, :, None, None, :, None]
    return out.reshape(db, n_slots * m * dq, n_slots * w)


def _pad_page(x):
    return jnp.pad(x, ((0, 0), (0, PAGE_SIZE - x.shape[1]), (0, 0)))


def kernel(x_prompt, x_sample, cache_k_moba, cache_v_moba, cache_k_diff, cache_v_diff, page_table,
           norm_attn, norm_ffn, norm_final, moba_w_qkv, moba_w_o, diff_w_qkv, diff_w_o,
           diff_lambda_q1, diff_lambda_k1, diff_lambda_q2, diff_lambda_k2, diff_subln,
           moe_w_group, moe_b_group, moe_w_expert, moe_b_expert, moe_w_gate, moe_w_up, moe_w_down):
    batch, seq_len, _ = x_prompt.shape
    db, dq, _ = x_sample.shape
    n_pages = page_table.shape[1]
    past_len = n_pages * PAGE_SIZE
    depth = norm_attn.shape[0]
    nb = seq_len // MOBA_BLOCK
    slopes_moba = _alibi_slopes(MOBA_HEADS)
    slopes_diff = _alibi_slopes(DIFF_HEADS)
    qpos_new = (past_len + np.arange(dq)).astype(np.float32)

    yp = x_prompt.reshape(batch * seq_len, D_MODEL)
    ys = x_sample.reshape(db * dq, D_MODEL)
    outs = {name: [] for name in ("km_p", "vm_p", "kd_p", "vd_p", "km_s", "vm_s", "kd_s", "vd_s")}
    for i in range(depth):
        j = i // 2
        final = i == depth - 1
        if i % 2 == 0:
            w_bf = moba_w_qkv[j].astype(BF16)
            wkt_bf = moba_w_qkv[j][:, D_MODEL:D_MODEL + MOBA_KVW].T.astype(BF16)
            qp, kp, vp, kt, vh, kmean = _norm_qkv(yp, norm_attn[i], w_bf, mode="moba",
                                                  seq_len=seq_len, wkt_bf=wkt_bf)
            km = kmean.reshape(batch, nb, MOBA_KV_HEADS, HEAD_DIM).transpose(0, 2, 3, 1)
            km_rep = jnp.pad(jnp.tile(km, (1, 1, MOBA_GROUP, 1)),
                             ((0, 0), (0, 0), (0, 0), (0, LANES - nb)))
            attn_p = _moba_prompt(qp, km_rep, kt, vh, jnp.asarray(slopes_moba))

            qs, ks, vs = _norm_qkv(ys, norm_attn[i], moba_w_qkv[j], mode="plain")
            k_pool = cache_k_moba[j].reshape(-1, PAGE_SIZE, MOBA_KVW)
            v_pool = cache_v_moba[j].reshape(-1, PAGE_SIZE, MOBA_KVW)
            km_past = _kmean_pages(k_pool, page_table)
            km_past = jnp.pad(km_past, ((0, 0), (0, LANES - km_past.shape[1]), (0, 0)))
            q5 = qs.reshape(db, dq, MOBA_KV_HEADS, MOBA_GROUP, HEAD_DIM).transpose(0, 2, 3, 1, 4)
            qf = _block_diag_rows(q5, MOBA_KV_HEADS)
            slope_rows = np.repeat(slopes_moba, dq).reshape(-1, 1)
            qpos_rows = np.tile(qpos_new, MOBA_HEADS).reshape(-1, 1)
            o_rows = _decode_attn(qf * SCALE, jnp.asarray(slope_rows), jnp.asarray(qpos_rows),
                                  _pad_page(ks.reshape(db, dq, -1)), _pad_page(vs.reshape(db, dq, -1)),
                                  (qf, km_past), k_pool, v_pool, page_table, mode="moba", pps=16)
            o6 = o_rows.reshape(db, MOBA_KV_HEADS, MOBA_GROUP, dq, MOBA_KV_HEADS, HEAD_DIM)
            o5 = jnp.stack([o6[:, g, :, :, g, :] for g in range(MOBA_KV_HEADS)], axis=1)
            attn_s = o5.transpose(0, 3, 1, 2, 4).reshape(db * dq, D_MODEL)
            w_o = moba_w_o[j]
            kvh, kvd = MOBA_KV_HEADS, HEAD_DIM
            names = ("km_p", "vm_p", "km_s", "vm_s")
        else:
            lambda_init = 0.8 - 0.6 * math.exp(-0.3 * i)
            w_bf = diff_w_qkv[j].astype(BF16)
            lams = [a[j].reshape(1, HEAD_DIM) for a in
                    (diff_lambda_q1, diff_lambda_k1, diff_lambda_q2, diff_lambda_k2)]
            subln_w = diff_subln[j].reshape(1, 2 * HEAD_DIM)
            qp, kp, vp, kb, vb = _norm_qkv(yp, norm_attn[i], w_bf, mode="diff")
            attn_p = _diff_prompt(qp, kb, vb, lams, subln_w, jnp.asarray(slopes_diff),
                                  batch=batch, lambda_init=lambda_init)

            qs, ks, vs = _norm_qkv(ys, norm_attn[i], diff_w_qkv[j], mode="plain")
            k_pool = cache_k_diff[j].reshape(-1, PAGE_SIZE, D_MODEL)
            v_pool = cache_v_diff[j].reshape(-1, PAGE_SIZE, D_MODEL)
            q5 = qs.reshape(db, dq, DIFF_HEADS, 2, HEAD_DIM).transpose(0, 2, 3, 1, 4)
            q5 = q5.reshape(db, 2 * DIFF_HEADS, 1, dq, HEAD_DIM)
            qbd = (_block_diag_rows(q5, 2 * DIFF_HEADS) * SCALE).astype(BF16)
            slope_rows = np.repeat(slopes_diff, 2 * dq).reshape(-1, 1)
            qpos_rows = np.tile(qpos_new, 2 * DIFF_HEADS).reshape(-1, 1)
            attn_s = _decode_attn(qbd, jnp.asarray(slope_rows), jnp.asarray(qpos_rows),
                                  _pad_page(ks.reshape(db, dq, -1)), _pad_page(vs.reshape(db, dq, -1)),
                                  (*lams, subln_w), k_pool, v_pool, page_table, mode="diff", pps=8,
                                  lambda_init=lambda_init).reshape(db * dq, D_MODEL)
            w_o = diff_w_o[j]
            kvh, kvd = DIFF_HEADS, 2 * HEAD_DIM
            names = ("kd_p", "vd_p", "kd_s", "vd_s")
        outs[names[0]].append(kp.reshape(batch, seq_len, kvh, kvd))
        outs[names[1]].append(vp.reshape(batch, seq_len, kvh, kvd))
        outs[names[2]].append(ks.reshape(db, dq, kvh, kvd))
        outs[names[3]].append(vs.reshape(db, dq, kvh, kvd))

        moe_w, (wr, br) = _moe_weights(moe_w_group[i], moe_b_group[i], moe_w_expert[i],
                                       moe_b_expert[i], moe_w_gate[i], moe_w_up[i], moe_w_down[i])
        yp = _wo_moe(yp, attn_p, w_o.astype(BF16), norm_ffn[i], *moe_w, norm_final, final=final)
        ys = _wo_moe_f32(ys, attn_s, w_o, norm_ffn[i], wr, br, moe_w_gate[i], moe_w_up[i],
                         moe_w_down[i], norm_final, final=final)

    stack = lambda name: jnp.stack(outs[name])
    return (yp.reshape(batch, seq_len, D_MODEL), ys.reshape(db, dq, D_MODEL),
            stack("km_p"), stack("vm_p"), stack("kd_p"), stack("vd_p"),
            stack("km_s"), stack("vm_s"), stack("kd_s"), stack("vd_s"))
```

```python
import functools
import math

import jax
import jax.numpy as jnp
import numpy as np
from jax import lax
from jax.experimental import pallas as pl
from jax.experimental.pallas import tpu as pltpu

D_MODEL = 1024
HEAD_DIM = 64
HEAD_DIM_LOG2 = 6
MOBA_HEADS = D_MODEL // HEAD_DIM
MOBA_KV_HEADS = 4
MOBA_GROUP = MOBA_HEADS // MOBA_KV_HEADS
MOBA_BLOCK = 256
MOBA_BLOCK_LOG2 = 8
MOBA_TOPK = 3
MOBA_KVW = MOBA_KV_HEADS * HEAD_DIM
DIFF_HEADS = D_MODEL // (2 * HEAD_DIM)
DIFF_HEADS_LOG2 = 3
DIFF_HW = 2 * HEAD_DIM
N_GROUPS = 4
EXPERTS_PER_GROUP = 4
N_EXPERTS = N_GROUPS * EXPERTS_PER_GROUP
D_EXPERT = 256
PAGE_SIZE = 128
NORM_EPS = 1e-6
SUBLN_EPS = 1e-5
NEG_INF = -1e30
SCALE = HEAD_DIM ** -0.5

LANES = 128
ROUTER_LANES = LANES
VMEM_LIMIT = 56 * 1024 * 1024

F32 = jnp.float32
BF16 = jnp.bfloat16
NT_DIMS = (((1,), (1,)), ((), ()))


def _alibi_slopes(n_heads):
    start = 2.0 ** (-8.0 / n_heads)
    return np.array([start ** (i + 1) for i in range(n_heads)], dtype=np.float32)


def _params(*sem):
    return pltpu.CompilerParams(dimension_semantics=sem, vmem_limit_bytes=VMEM_LIMIT)


def _resident(shape):
    nd = len(shape)
    return pl.BlockSpec(shape, lambda *_: (0,) * nd, pipeline_mode=pl.Buffered(1))


def _rms(x, w, eps):
    ms = jnp.mean(x * x, axis=-1, keepdims=True)
    return x * lax.rsqrt(ms + eps) * w


def _split_bf16(x):
    hi = x.astype(BF16)
    return hi, (x - hi.astype(F32)).astype(BF16)


def _nt_dot(a, b):
    return lax.dot_general(a, b, NT_DIMS, preferred_element_type=F32)


def _dot_f32(a, b):
    return jnp.dot(a, b, precision=lax.Precision.HIGHEST, preferred_element_type=F32)


def _norm_qkv_kernel(x_ref, nw_ref, w_ref, *refs, mode, kvw, tm):
    h = _rms(x_ref[...], nw_ref[...], NORM_EPS)
    if w_ref.dtype == F32:
        qkv = _dot_f32(h, w_ref[...])
    else:
        h = h.astype(BF16)
        qkv = jnp.dot(h, w_ref[...], preferred_element_type=F32)
    q = qkv[:, :D_MODEL]
    k = qkv[:, D_MODEL:D_MODEL + kvw]
    v = qkv[:, D_MODEL + kvw:]
    if mode == "plain":
        q_ref, k_ref, v_ref = refs
    elif mode == "moba":
        wkt_ref, q_ref, k_ref, v_ref, kt_ref, vh_ref, km_ref = refs
        kt = _nt_dot(wkt_ref[...], h)
        vb = v.astype(BF16)
        for jj in range(tm // MOBA_BLOCK):
            sl = slice(jj * MOBA_BLOCK, (jj + 1) * MOBA_BLOCK)
            kt_ref[0, jj] = kt[:, sl].astype(BF16)
            km_ref[jj] = jnp.mean(k[sl], axis=0, keepdims=True)
        for g in range(MOBA_KV_HEADS):
            vh_ref[0, g] = vb[:, g * HEAD_DIM:(g + 1) * HEAD_DIM]
    else:
        q_ref, k_ref, v_ref, kb_ref, vb_ref = refs
        kb_ref[...] = k.astype(BF16)
        vb_ref[...] = v.astype(BF16)
    q_ref[...] = q
    k_ref[...] = k
    v_ref[...] = v


def _norm_qkv(x, nw, w, *, mode, seq_len=None, wkt_bf=None):
    n = x.shape[0]
    e = w.shape[1]
    kvw = (e - D_MODEL) // 2
    tm = min(512, n)
    nt = n // tm
    row = lambda t: (t, 0)
    in_specs = [pl.BlockSpec((tm, D_MODEL), row), _resident((1, D_MODEL)), _resident((D_MODEL, e))]
    args = [x, nw.reshape(1, D_MODEL), w]
    out_shape = [jax.ShapeDtypeStruct((n, D_MODEL), F32),
                 jax.ShapeDtypeStruct((n, kvw), F32),
                 jax.ShapeDtypeStruct((n, kvw), F32)]
    out_specs = [pl.BlockSpec((tm, D_MODEL), row), pl.BlockSpec((tm, kvw), row),
                 pl.BlockSpec((tm, kvw), row)]
    if mode == "moba":
        b = n // seq_len
        tps = seq_len // tm
        bpt = tm // MOBA_BLOCK
        nb = seq_len // MOBA_BLOCK
        in_specs.append(_resident((kvw, D_MODEL)))
        args.append(wkt_bf)
        out_shape += [jax.ShapeDtypeStruct((b, nb, kvw, MOBA_BLOCK), BF16),
                      jax.ShapeDtypeStruct((b, MOBA_KV_HEADS, seq_len, HEAD_DIM), BF16),
                      jax.ShapeDtypeStruct((b * nb, 1, kvw), F32)]
        out_specs += [pl.BlockSpec((1, bpt, kvw, MOBA_BLOCK), lambda t: (t // tps, t % tps, 0, 0)),
                      pl.BlockSpec((1, MOBA_KV_HEADS, tm, HEAD_DIM), lambda t: (t // tps, 0, t % tps, 0)),
                      pl.BlockSpec((bpt, 1, kvw), lambda t: (t, 0, 0))]
    elif mode == "diff":
        out_shape += [jax.ShapeDtypeStruct((n, kvw), BF16), jax.ShapeDtypeStruct((n, kvw), BF16)]
        out_specs += [pl.BlockSpec((tm, kvw), row), pl.BlockSpec((tm, kvw), row)]
    return pl.pallas_call(
        functools.partial(_norm_qkv_kernel, mode=mode, kvw=kvw, tm=tm),
        grid=(nt,), in_specs=in_specs, out_specs=out_specs, out_shape=out_shape,
        compiler_params=_params("parallel"), name="norm_qkv_" + mode,
    )(*args)


def _softmax_update(s, m_ref, l_ref, rows, shift=0.0):
    width = s.shape[1]
    m_prev = m_ref[rows, :]
    m_new = jnp.maximum(m_prev, jnp.max(s, axis=1, keepdims=True) + shift)
    alpha = jnp.exp(m_prev - m_new)
    p = jnp.exp(s - jnp.concatenate([m_new - shift] * (width // LANES), axis=1))
    l_ref[rows, :] = alpha * l_ref[rows, :] + jnp.sum(p, axis=1, keepdims=True)
    m_ref[rows, :] = m_new
    return p, alpha


def _init_softmax_state(m_ref, l_ref, acc_ref):
    m_ref[...] = jnp.full(m_ref.shape, -jnp.inf, F32)
    l_ref[...] = jnp.zeros(l_ref.shape, F32)
    acc_ref[...] = jnp.zeros(acc_ref.shape, F32)


def _widen(x, width):
    return jnp.concatenate([x] * (width // LANES), axis=1)


def _top3_mask(gate, lane, n_valid):
    gt = jnp.where(lane < n_valid, gate, -jnp.inf)
    sel = jnp.zeros(gate.shape, F32)
    for _ in range(MOBA_TOPK):
        mx = jnp.max(gt, axis=1, keepdims=True)
        idx = jnp.min(jnp.where(gt == mx, lane, gate.shape[1]), axis=1, keepdims=True)
        hit = (lane == idx) & (mx > -jnp.inf)
        sel = jnp.where(hit, 1.0, sel)
        gt = jnp.where(hit, -jnp.inf, gt)
    return sel


def _moba_prompt_kernel(slopes_ref, q_ref, km_ref, kt_ref, v_ref, o_ref,
                        qz_ref, sel_ref, bias_ref, m_ref, l_ref, acc_ref, p_ref):
    tq = MOBA_BLOCK
    g = pl.program_id(1)
    i = pl.program_id(2)
    q = q_ref[...]
    gate = _dot_f32(q, km_ref[0, 0])
    lane = lax.broadcasted_iota(jnp.int32, (tq, LANES), 1)
    sel_ref[...] = _top3_mask(gate, lane, i)
    qs = q * SCALE
    qlane = lax.shift_right_logical(lax.broadcasted_iota(jnp.int32, qs.shape, 1), HEAD_DIM_LOG2)
    for r in range(MOBA_GROUP):
        qz_ref[r * tq:(r + 1) * tq, :] = jnp.where(qlane == r, qs, 0.0).astype(BF16)
    _init_softmax_state(m_ref, l_ref, acc_ref)

    @pl.when(i == 0)
    def _():
        rel = (lax.broadcasted_iota(jnp.int32, (tq, 2 * tq), 1)
               - lax.broadcasted_iota(jnp.int32, (tq, 2 * tq), 0)).astype(F32)
        for r in range(MOBA_GROUP):
            bias_ref[r * tq:(r + 1) * tq, :] = slopes_ref[g * MOBA_GROUP + r] * rel

    def attend(blocks, ok):
        width = len(blocks) * tq
        start = blocks[0] * tq
        kt = jnp.concatenate([kt_ref[0, j] for j in blocks], axis=1)
        kt4 = jnp.concatenate([kt] * MOBA_GROUP, axis=0)
        s = jnp.dot(qz_ref[...], kt4, preferred_element_type=F32)
        off = (start - i * tq).astype(F32)
        for r in range(MOBA_GROUP):
            rows = slice(r * tq, (r + 1) * tq)
            sr = jnp.where(ok, s[rows] + bias_ref[rows, :width], NEG_INF)
            p, alpha = _softmax_update(sr, m_ref, l_ref, rows,
                                       shift=slopes_ref[g * MOBA_GROUP + r] * off)
            acc_ref[rows, :] = acc_ref[rows, :] * alpha[:, :HEAD_DIM]
            p_ref[rows, :width] = p.astype(BF16)
        vj = v_ref[0, 0, pl.ds(pl.multiple_of(start, tq), width), :]
        acc_ref[...] += jnp.dot(p_ref[:, :width], vj, preferred_element_type=F32)

    def selected(j):
        return jnp.sum(jnp.where(lane == j, sel_ref[...], 0.0), axis=1, keepdims=True)

    def past_pair(jj, carry):
        sel0, sel1 = selected(2 * jj), selected(2 * jj + 1)

        @pl.when(jnp.max(jnp.maximum(sel0, sel1)) > 0.0)
        def _():
            ok = jnp.concatenate([jnp.broadcast_to(sel0, (tq, tq)),
                                  jnp.broadcast_to(sel1, (tq, tq))], axis=1) > 0.0
            attend((2 * jj, 2 * jj + 1), ok)
        return carry

    lax.fori_loop(0, i // 2, past_pair, 0)

    @pl.when(i % 2 == 1)
    def _():
        sel0 = selected(i - 1)

        @pl.when(jnp.max(sel0) > 0.0)
        def _():
            attend((i - 1,), sel0 > 0.0)

    causal = (lax.broadcasted_iota(jnp.int32, (tq, tq), 1)
              <= lax.broadcasted_iota(jnp.int32, (tq, tq), 0))
    attend((i,), causal)
    out = [acc_ref[r * tq:(r + 1) * tq, :] / l_ref[r * tq:(r + 1) * tq, :HEAD_DIM]
           for r in range(MOBA_GROUP)]
    o_ref[...] = jnp.concatenate(out, axis=1).astype(BF16)


def _moba_prompt(q, km_rep, kt, vh, slopes):
    b, nb = kt.shape[0], kt.shape[1]
    seq_len = nb * MOBA_BLOCK
    tq = MOBA_BLOCK
    gw = MOBA_GROUP * HEAD_DIM
    return pl.pallas_call(
        _moba_prompt_kernel,
        grid=(b, MOBA_KV_HEADS, nb),
        in_specs=[pl.BlockSpec(memory_space=pltpu.SMEM),
                  pl.BlockSpec((tq, gw), lambda bi, g, i: (bi * nb + i, g)),
                  pl.BlockSpec((1, 1, gw, LANES), lambda bi, g, i: (bi, g, 0, 0)),
                  pl.BlockSpec((1, nb, HEAD_DIM, MOBA_BLOCK), lambda bi, g, i: (bi, 0, g, 0)),
                  pl.BlockSpec((1, 1, seq_len, HEAD_DIM), lambda bi, g, i: (bi, g, 0, 0))],
        out_specs=pl.BlockSpec((tq, gw), lambda bi, g, i: (bi * nb + i, g)),
        out_shape=jax.ShapeDtypeStruct((b * seq_len, D_MODEL), BF16),
        scratch_shapes=[pltpu.VMEM((MOBA_GROUP * tq, gw), BF16),
                        pltpu.VMEM((tq, LANES), F32),
                        pltpu.VMEM((MOBA_GROUP * tq, 2 * tq), F32),
                        pltpu.VMEM((MOBA_GROUP * tq, LANES), F32),
                        pltpu.VMEM((MOBA_GROUP * tq, LANES), F32),
                        pltpu.VMEM((MOBA_GROUP * tq, HEAD_DIM), F32),
                        pltpu.VMEM((MOBA_GROUP * tq, 2 * tq), BF16)],
        compiler_params=_params("parallel", "parallel", "arbitrary"), name="moba_prompt",
    )(slopes, q, km_rep, kt, vh)


def _lambda_full(lq1_ref, lk1_ref, lq2_ref, lk2_ref, lambda_init):
    a = jnp.sum(lq1_ref[...] * lk1_ref[...], axis=1, keepdims=True)
    b = jnp.sum(lq2_ref[...] * lk2_ref[...], axis=1, keepdims=True)
    return jnp.exp(a) - jnp.exp(b) + lambda_init


def _subln(o, w, lambda_init):
    return _rms(o, w, SUBLN_EPS) * (1.0 - lambda_init)


def _diff_prompt_kernel(slopes_ref, q_ref, k_ref, v_ref, lq1_ref, lk1_ref, lq2_ref, lk2_ref,
                        sw_ref, o_ref, qz_ref, bias_ref, m_ref, l_ref, acc_ref, *, tq, lambda_init):
    hd = pl.program_id(1)
    i = pl.program_id(2)
    qs = q_ref[...] * SCALE
    qlane = lax.shift_right_logical(lax.broadcasted_iota(jnp.int32, qs.shape, 1), HEAD_DIM_LOG2)
    for mp in range(2):
        qz_ref[mp * tq:(mp + 1) * tq, :] = jnp.where(qlane == mp, qs, 0.0).astype(BF16)
    _init_softmax_state(m_ref, l_ref, acc_ref)
    slope = slopes_ref[hd]
    everything = slice(None)

    def in_tile_offsets(width):
        rows2 = lax.broadcasted_iota(jnp.int32, (2 * tq, width), 0)
        return (lax.broadcasted_iota(jnp.int32, (2 * tq, width), 1)
                - jnp.where(rows2 >= tq, rows2 - tq, rows2))

    @pl.when(i == 0)
    def _():
        bias_ref[...] = slope * in_tile_offsets(2 * tq).astype(F32)

    def attend(start, width, causal):
        ks = pl.ds(pl.multiple_of(start, tq), width)
        s = _nt_dot(qz_ref[...], k_ref[ks, :]) + bias_ref[:, :width]
        if causal:
            s = jnp.where(in_tile_offsets(width) <= 0, s, NEG_INF)
        p, alpha = _softmax_update(s, m_ref, l_ref, everything,
                                   shift=slope * (start - i * tq).astype(F32))
        acc_ref[...] = acc_ref[...] * alpha + jnp.dot(p.astype(BF16), v_ref[ks, :],
                                                      preferred_element_type=F32)

    def past_pair(jj, carry):
        attend(jj * (2 * tq), 2 * tq, False)
        return carry

    lax.fori_loop(0, i // 2, past_pair, 0)

    @pl.when(i % 2 == 1)
    def _():
        attend((i - 1) * tq, tq, False)

    attend(i * tq, tq, True)
    o1 = acc_ref[:tq, :] / l_ref[:tq, :]
    o2 = acc_ref[tq:, :] / l_ref[tq:, :]
    lam = _lambda_full(lq1_ref, lk1_ref, lq2_ref, lk2_ref, lambda_init)
    o_ref[...] = _subln(o1 - lam * o2, sw_ref[...], lambda_init).astype(BF16)


def _diff_prompt(q, kb, vb, lams, subln_w, slopes, *, batch, lambda_init):
    n = q.shape[0]
    seq_len = n // batch
    tq = 256
    nq = seq_len // tq
    lam_specs = [_resident((1, HEAD_DIM))] * 4
    return pl.pallas_call(
        functools.partial(_diff_prompt_kernel, tq=tq, lambda_init=lambda_init),
        grid=(batch, DIFF_HEADS, nq),
        in_specs=[pl.BlockSpec(memory_space=pltpu.SMEM),
                  pl.BlockSpec((tq, DIFF_HW), lambda bi, h, i: (bi * nq + i, h)),
                  pl.BlockSpec((seq_len, DIFF_HW), lambda bi, h, i: (bi, h)),
                  pl.BlockSpec((seq_len, DIFF_HW), lambda bi, h, i: (bi, h)),
                  *lam_specs, _resident((1, DIFF_HW))],
        out_specs=pl.BlockSpec((tq, DIFF_HW), lambda bi, h, i: (bi * nq + i, h)),
        out_shape=jax.ShapeDtypeStruct((n, D_MODEL), BF16),
        scratch_shapes=[pltpu.VMEM((2 * tq, DIFF_HW), BF16),
                        pltpu.VMEM((2 * tq, 2 * tq), F32),
                        pltpu.VMEM((2 * tq, LANES), F32),
                        pltpu.VMEM((2 * tq, LANES), F32),
                        pltpu.VMEM((2 * tq, DIFF_HW), F32)],
        compiler_params=_params("parallel", "parallel", "arbitrary"), name="diff_prompt",
    )(slopes, q, kb, vb, *lams, subln_w)


def _page_specs(block_shape, pages_per_step, page_offset):
    def spec(t):
        return pl.BlockSpec((1,) + block_shape,
                            lambda b, s, pt: (pt[b, s * pages_per_step + t] + page_offset, 0, 0))
    return [spec(t) for t in range(pages_per_step)]


def _kmean_pages_kernel(pt_ref, *refs):
    k_refs, o_ref = refs[:-1], refs[-1]
    step = pl.program_id(1)
    ppb = MOBA_BLOCK // PAGE_SIZE
    blocks_per_step = len(k_refs) // ppb
    lane = lax.broadcasted_iota(jnp.int32, o_ref.shape[1:], 1)
    upd = jnp.zeros(o_ref.shape[1:], F32)
    for t in range(blocks_per_step):
        tot = sum(k_refs[t * ppb + u][0] for u in range(ppb))
        col = jnp.sum(tot, axis=1, keepdims=True) / MOBA_BLOCK
        upd = jnp.where(lane == step * blocks_per_step + t, col, upd)

    @pl.when(step == 0)
    def _():
        o_ref[0] = upd

    @pl.when(step > 0)
    def _():
        o_ref[0] += upd


def _kmean_pages(pool_t, page_table, page_offset):
    db, n_pages = page_table.shape
    pps = 16
    width = pool_t.shape[1]
    grid_spec = pltpu.PrefetchScalarGridSpec(
        num_scalar_prefetch=1, grid=(db, n_pages // pps),
        in_specs=_page_specs((width, PAGE_SIZE), pps, page_offset),
        out_specs=pl.BlockSpec((1, width, LANES), lambda b, s, pt: (b, 0, 0)))
    return pl.pallas_call(
        _kmean_pages_kernel, grid_spec=grid_spec,
        out_shape=jax.ShapeDtypeStruct((db, width, LANES), F32),
        compiler_params=_params("parallel", "arbitrary"), name="kmean_pages",
    )(page_table, *([pool_t] * pps))


def _moba_decode_kernel(pt_ref, qs_ref, qf_ref, kmt_ref, slope_ref, qi_ref, knew_ref, vnew_ref,
                        *refs, pps, past_len):
    k_refs, v_refs = refs[:pps], refs[pps:2 * pps]
    o_ref, m_ref, l_ref, acc_ref, sel_ref = refs[2 * pps:]
    step = pl.program_id(1)
    n_rows = qs_ref.shape[1]
    chunk = pps * PAGE_SIZE
    everything = slice(None)

    @pl.when(step == 0)
    def _():
        _init_softmax_state(m_ref, l_ref, acc_ref)
        gr = _dot_f32(qf_ref[0], kmt_ref[0])
        n_blocks = past_len // MOBA_BLOCK
        rpg = n_rows // MOBA_KV_HEADS
        dq = rpg // MOBA_GROUP
        lane = lax.broadcasted_iota(jnp.int32, (dq, LANES), 1)
        for g in range(MOBA_KV_HEADS):
            gs = sum(gr[g * rpg + r * dq:g * rpg + (r + 1) * dq] for r in range(MOBA_GROUP))
            sel = _top3_mask(gs, lane, n_blocks)
            sel_ref[g * rpg:(g + 1) * rpg, :] = jnp.concatenate([sel] * MOBA_GROUP, axis=0)

    qh, ql = _split_bf16(qs_ref[0])
    q2 = jnp.concatenate([qh, ql], axis=0)

    def qk(kt32):
        kh, kl = _split_bf16(kt32)
        s2 = jnp.dot(q2, kh, preferred_element_type=F32)
        return s2[:n_rows] + s2[n_rows:] + jnp.dot(qh, kl, preferred_element_type=F32)

    def pv(p, vt32):
        ph, plo = _split_bf16(p)
        vh, vl = _split_bf16(vt32)
        o2 = _nt_dot(jnp.concatenate([ph, plo], axis=0), vh)
        return o2[:n_rows] + o2[n_rows:] + _nt_dot(ph, vl)

    def update(kt32, vt32, tok, ok, shift):
        s = qk(kt32) + slope_ref[...] * (tok - qi_ref[...])
        s = jnp.where(ok, s, NEG_INF)
        p, alpha = _softmax_update(s, m_ref, l_ref, everything, shift=shift)
        acc_ref[...] = acc_ref[...] * _widen(alpha, acc_ref.shape[1]) + pv(p, vt32)

    kt = jnp.concatenate([r[0] for r in k_refs], axis=1)
    vt = jnp.concatenate([r[0] for r in v_refs], axis=1)
    tok = lax.broadcasted_iota(jnp.int32, (1, chunk), 1).astype(F32)
    blk = lax.broadcasted_iota(jnp.int32, (LANES, chunk), 0)
    key = step * chunk + lax.broadcasted_iota(jnp.int32, (LANES, chunk), 1)
    expand = jnp.where(blk == lax.shift_right_logical(key, MOBA_BLOCK_LOG2), 1.0, 0.0).astype(BF16)
    ok = jnp.dot(sel_ref[...].astype(BF16), expand, preferred_element_type=F32) > 0.5
    update(kt, vt, tok, ok, slope_ref[...] * (step * chunk - past_len).astype(F32))

    @pl.when(step == pl.num_programs(1) - 1)
    def _():
        tok_new = lax.broadcasted_iota(jnp.int32, (1, PAGE_SIZE), 1).astype(F32)
        update(knew_ref[0], vnew_ref[0], tok_new, tok_new <= qi_ref[...], 0.0)
        o_ref[0] = acc_ref[...] / _widen(l_ref[...], acc_ref.shape[1])


def _moba_decode(q_scaled, q_plain, kmt, slope_rows, qi_rows, knew_t, vnew_t, k_pool_t, v_pool_t,
                 page_table, page_offset):
    db, n_pages = page_table.shape
    n_rows, width = q_scaled.shape[1], q_scaled.shape[2]
    pps = 16
    per_seq = lambda shape: pl.BlockSpec((1,) + shape, lambda b, s, pt: (b, 0, 0))
    const = lambda shape: pl.BlockSpec(shape, lambda b, s, pt: (0, 0))
    grid_spec = pltpu.PrefetchScalarGridSpec(
        num_scalar_prefetch=1, grid=(db, n_pages // pps),
        in_specs=[per_seq((n_rows, width)), per_seq((n_rows, width)), per_seq((width, LANES)),
                  const((n_rows, 1)), const((n_rows, 1)),
                  per_seq((width, PAGE_SIZE)), per_seq((width, PAGE_SIZE)),
                  *(_page_specs((width, PAGE_SIZE), pps, page_offset) * 2)],
        out_specs=per_seq((n_rows, width)),
        scratch_shapes=[pltpu.VMEM((n_rows, LANES), F32), pltpu.VMEM((n_rows, LANES), F32),
                        pltpu.VMEM((n_rows, width), F32), pltpu.VMEM((n_rows, LANES), F32)])
    return pl.pallas_call(
        functools.partial(_moba_decode_kernel, pps=pps, past_len=n_pages * PAGE_SIZE),
        grid_spec=grid_spec, out_shape=jax.ShapeDtypeStruct((db, n_rows, width), F32),
        compiler_params=_params("parallel", "arbitrary"), name="moba_decode",
    )(page_table, q_scaled, q_plain, kmt, slope_rows, qi_rows, knew_t, vnew_t,
      *([k_pool_t] * pps), *([v_pool_t] * pps))


def _diff_decode_kernel(pt_ref, qx_ref, slope_ref, qi_ref, head_ref, knew_ref, vnew_ref,
                        lq1_ref, lk1_ref, lq2_ref, lk2_ref, sw_ref, *refs, pps, past_len, lambda_init):
    k_refs, v_refs = refs[:pps], refs[pps:2 * pps]
    o_ref, bias_ref, m_ref, l_ref, acc_ref = refs[2 * pps:]
    step = pl.program_id(1)
    n_rows = qx_ref.shape[1]
    chunk = pps * PAGE_SIZE
    everything = slice(None)

    def alibi_bias(n_cols):
        col = lax.broadcasted_iota(jnp.int32, (n_rows, n_cols), 1)
        tok = lax.shift_right_logical(col, DIFF_HEADS_LOG2)
        own_head = (col & (DIFF_HEADS - 1)) == head_ref[...]
        tokf = tok.astype(F32)
        return jnp.where(own_head, slope_ref[...] * (tokf - qi_ref[...]), NEG_INF), tokf

    @pl.when(step == 0)
    def _():
        _init_softmax_state(m_ref, l_ref, acc_ref)
        bias_ref[...] = alibi_bias(chunk * DIFF_HEADS)[0]

    def update(k32, v32, bias, shift):
        s = _nt_dot(qx_ref[0], k32.astype(BF16)) + bias
        p, alpha = _softmax_update(s, m_ref, l_ref, everything, shift=shift)
        acc_ref[...] = acc_ref[...] * alpha + jnp.dot(p.astype(BF16), v32.astype(BF16),
                                                      preferred_element_type=F32)

    kc = jnp.concatenate([r[0] for r in k_refs], axis=0)
    vc = jnp.concatenate([r[0] for r in v_refs], axis=0)
    update(kc, vc, bias_ref[...], slope_ref[...] * (step * chunk - past_len).astype(F32))

    @pl.when(step == pl.num_programs(1) - 1)
    def _():
        bias_new, tokf = alibi_bias(PAGE_SIZE * DIFF_HEADS)
        update(knew_ref[0], vnew_ref[0], jnp.where(tokf <= qi_ref[...], bias_new, NEG_INF), 0.0)
        o = acc_ref[...] / l_ref[...]
        dq = n_rows // (2 * DIFF_HEADS)
        lam = _lambda_full(lq1_ref, lk1_ref, lq2_ref, lk2_ref, lambda_init)
        outs = []
        for hd in range(DIFF_HEADS):
            o1 = o[(2 * hd) * dq:(2 * hd + 1) * dq]
            o2 = o[(2 * hd + 1) * dq:(2 * hd + 2) * dq]
            outs.append(_subln(o1 - lam * o2, sw_ref[...], lambda_init))
        o_ref[0] = jnp.concatenate(outs, axis=1)


def _diff_decode(qx, slope_rows, qi_rows, head_rows, knew, vnew, lams, subln_w, k_pool, v_pool,
                 page_table, page_offset, *, lambda_init):
    db, n_pages = page_table.shape
    n_rows = qx.shape[1]
    pps = 8
    page_rows = PAGE_SIZE * DIFF_HEADS
    dq = n_rows // (2 * DIFF_HEADS)
    per_seq = lambda shape: pl.BlockSpec((1,) + shape, lambda b, s, pt: (b, 0, 0))
    const = lambda shape: pl.BlockSpec(shape, lambda b, s, pt: (0, 0))
    grid_spec = pltpu.PrefetchScalarGridSpec(
        num_scalar_prefetch=1, grid=(db, n_pages // pps),
        in_specs=[per_seq((n_rows, DIFF_HW)), const((n_rows, 1)), const((n_rows, 1)),
                  const((n_rows, 1)), per_seq((page_rows, DIFF_HW)), per_seq((page_rows, DIFF_HW)),
                  *([const((1, HEAD_DIM))] * 4), const((1, DIFF_HW)),
                  *(_page_specs((page_rows, DIFF_HW), pps, page_offset) * 2)],
        out_specs=per_seq((dq, D_MODEL)),
        scratch_shapes=[pltpu.VMEM((n_rows, pps * page_rows), F32),
                        pltpu.VMEM((n_rows, LANES), F32), pltpu.VMEM((n_rows, LANES), F32),
                        pltpu.VMEM((n_rows, DIFF_HW), F32)])
    return pl.pallas_call(
        functools.partial(_diff_decode_kernel, pps=pps, past_len=n_pages * PAGE_SIZE,
                          lambda_init=lambda_init),
        grid_spec=grid_spec, out_shape=jax.ShapeDtypeStruct((db, dq, D_MODEL), F32),
        compiler_params=_params("parallel", "arbitrary"), name="diff_decode",
    )(page_table, qx, slope_rows, qi_rows, head_rows, knew, vnew, *lams, subln_w,
      *([k_pool] * pps), *([v_pool] * pps))


def _router_logits(t, wr_hi_ref, wr_lo_ref, br_ref):
    t_hi, t_lo = _split_bf16(t)
    return (jnp.dot(t_hi, wr_hi_ref[...], preferred_element_type=F32)
            + jnp.dot(t_lo, wr_hi_ref[...], preferred_element_type=F32)
            + jnp.dot(t_hi, wr_lo_ref[...], preferred_element_type=F32)) + br_ref[...]


def _router_combine(logits):
    lane = lax.broadcasted_iota(jnp.int32, logits.shape, 1)
    width = logits.shape[1]
    first = lambda hit: jnp.min(jnp.where(hit, lane, width), axis=1, keepdims=True)
    is_g = lane < N_GROUPS
    gl = jnp.where(is_g, logits, -jnp.inf)
    g_max = jnp.max(gl, axis=1, keepdims=True)
    g_sel = first(gl == g_max)
    g_w = 1.0 / jnp.sum(jnp.where(is_g, jnp.exp(logits - g_max), 0.0), axis=1, keepdims=True)
    e_lo = N_GROUPS + g_sel * EXPERTS_PER_GROUP
    el = jnp.where((lane >= e_lo) & (lane < e_lo + EXPERTS_PER_GROUP), logits, -jnp.inf)
    v1 = jnp.max(el, axis=1, keepdims=True)
    i1 = first(el == v1)
    el2 = jnp.where(lane == i1, -jnp.inf, el)
    v2 = jnp.max(el2, axis=1, keepdims=True)
    i2 = first(el2 == v2)
    e2 = jnp.exp(v2 - v1)
    w1 = 1.0 / (1.0 + e2)
    w2 = e2 / (1.0 + e2)
    return jnp.where(lane == i1, g_w * w1, 0.0) + jnp.where(lane == i2, g_w * w2, 0.0)


def _wo_moe_kernel(x_ref, a_ref, wo_ref, nw_ref, wr_hi_ref, wr_lo_ref, br_ref, wgu_ref, wd_ref,
                   fw_ref, o_ref, hid_ref, *, final):
    y1 = x_ref[...] + jnp.dot(a_ref[...], wo_ref[...], preferred_element_type=F32)
    t = _rms(y1, nw_ref[...], NORM_EPS)
    combine = _router_combine(_router_logits(t, wr_hi_ref, wr_lo_ref, br_ref))
    tb = t.astype(BF16)
    for e in range(N_EXPERTS):
        gu = jnp.dot(tb, wgu_ref[e], preferred_element_type=F32)
        gate, up = gu[:, :D_EXPERT], gu[:, D_EXPERT:]
        c = combine[:, N_GROUPS + e:N_GROUPS + e + 1]
        hid = gate * (1.0 / (1.0 + jnp.exp(-gate))) * up * c
        hid_ref[:, e * D_EXPERT:(e + 1) * D_EXPERT] = hid.astype(BF16)
    y2 = y1 + jnp.dot(hid_ref[...], wd_ref[...], preferred_element_type=F32)
    if final:
        y2 = _rms(y2, fw_ref[...], NORM_EPS)
    o_ref[...] = y2


def _wo_moe(x, attn, wo_bf, nw, wr_hi, wr_lo, br, wgu_bf, wd_bf, fw, *, final):
    n = x.shape[0]
    tm = min(512, n)
    row = lambda t: (t, 0)
    return pl.pallas_call(
        functools.partial(_wo_moe_kernel, final=final),
        grid=(n // tm,),
        in_specs=[pl.BlockSpec((tm, D_MODEL), row), pl.BlockSpec((tm, D_MODEL), row),
                  _resident(wo_bf.shape), _resident((1, D_MODEL)),
                  _resident(wr_hi.shape), _resident(wr_lo.shape), _resident(br.shape),
                  _resident(wgu_bf.shape), _resident(wd_bf.shape), _resident((1, D_MODEL))],
        out_specs=pl.BlockSpec((tm, D_MODEL), row),
        out_shape=jax.ShapeDtypeStruct((n, D_MODEL), F32),
        scratch_shapes=[pltpu.VMEM((tm, N_EXPERTS * D_EXPERT), BF16)],
        compiler_params=_params("parallel"), name="wo_moe",
    )(x, attn, wo_bf, nw.reshape(1, D_MODEL), wr_hi, wr_lo, br, wgu_bf, wd_bf,
      fw.reshape(1, D_MODEL))


def _wo_moe_f32_kernel(x_ref, a_ref, wo_ref, nw_ref, wr_ref, br_ref, wg_ref, wu_ref, wd_ref,
                       fw_ref, o_ref, y1_ref, t_ref, comb_ref, acc_ref, *, final):
    e = pl.program_id(0)

    @pl.when(e == 0)
    def _():
        y1 = x_ref[...] + _dot_f32(a_ref[...], wo_ref[...])
        t = _rms(y1, nw_ref[...], NORM_EPS)
        y1_ref[...] = y1
        t_ref[...] = t
        comb_ref[...] = _router_combine(_dot_f32(t, wr_ref[...]) + br_ref[...])
        acc_ref[...] = jnp.zeros(acc_ref.shape, F32)

    t = t_ref[...]
    gate = _dot_f32(t, wg_ref[0])
    up = _dot_f32(t, wu_ref[0])
    lane = lax.broadcasted_iota(jnp.int32, comb_ref.shape, 1)
    c = jnp.sum(jnp.where(lane == N_GROUPS + e, comb_ref[...], 0.0), axis=1, keepdims=True)
    hid = gate * (1.0 / (1.0 + jnp.exp(-gate))) * up * c
    acc_ref[...] += _dot_f32(hid, wd_ref[0])

    @pl.when(e == pl.num_programs(0) - 1)
    def _():
        y2 = y1_ref[...] + acc_ref[...]
        if final:
            y2 = _rms(y2, fw_ref[...], NORM_EPS)
        o_ref[...] = y2


def _wo_moe_f32(x, attn, wo, nw, wr, br, w_gate, w_up, w_down, fw, *, final):
    n = x.shape[0]
    whole = lambda shape: pl.BlockSpec(shape, lambda e: (0,) * len(shape))
    per_expert = lambda shape: pl.BlockSpec((1,) + shape, lambda e: (e, 0, 0))
    return pl.pallas_call(
        functools.partial(_wo_moe_f32_kernel, final=final),
        grid=(N_EXPERTS,),
        in_specs=[whole((n, D_MODEL)), whole((n, D_MODEL)), whole(wo.shape), whole((1, D_MODEL)),
                  whole(wr.shape), whole(br.shape), per_expert((D_MODEL, D_EXPERT)),
                  per_expert((D_MODEL, D_EXPERT)), per_expert((D_EXPERT, D_MODEL)),
                  whole((1, D_MODEL))],
        out_specs=whole((n, D_MODEL)),
        out_shape=jax.ShapeDtypeStruct((n, D_MODEL), F32),
        scratch_shapes=[pltpu.VMEM((n, D_MODEL), F32), pltpu.VMEM((n, D_MODEL), F32),
                        pltpu.VMEM((n, ROUTER_LANES), F32), pltpu.VMEM((n, D_MODEL), F32)],
        compiler_params=_params("arbitrary"), name="wo_moe_f32",
    )(x, attn, wo, nw.reshape(1, D_MODEL), wr, br, w_gate, w_up, w_down, fw.reshape(1, D_MODEL))


def _moe_weights(w_group, b_group, w_expert, b_expert, w_gate, w_up, w_down):
    n_r = N_GROUPS + N_EXPERTS
    wr = jnp.pad(jnp.concatenate([w_group, w_expert], axis=1), ((0, 0), (0, ROUTER_LANES - n_r)))
    wr_hi = wr.astype(BF16)
    wr_lo = (wr - wr_hi.astype(F32)).astype(BF16)
    br = jnp.pad(jnp.concatenate([b_group, b_expert]), (0, ROUTER_LANES - n_r)).reshape(1, -1)
    wgu = jnp.concatenate([w_gate, w_up], axis=2).astype(BF16)
    wd = w_down.reshape(N_EXPERTS * D_EXPERT, D_MODEL).astype(BF16)
    return (wr_hi, wr_lo, br, wgu, wd), (wr, br)


def _block_diag_rows(q, n_slots):
    db, _, m, dq, w = q.shape
    eye = jnp.eye(n_slots, dtype=q.dtype)
    out = q[:, :, :, :, None, :] * eye[None, :, None, None, :, None]
    return out.reshape(db, n_slots * m * dq, n_slots * w)


def _pad_tokens(x):
    return jnp.pad(x, ((0, 0), (0, PAGE_SIZE - x.shape[1])) + ((0, 0),) * (x.ndim - 2))


def kernel(x_prompt, x_sample, cache_k_moba, cache_v_moba, cache_k_diff, cache_v_diff, page_table,
           norm_attn, norm_ffn, norm_final, moba_w_qkv, moba_w_o, diff_w_qkv, diff_w_o,
           diff_lambda_q1, diff_lambda_k1, diff_lambda_q2, diff_lambda_k2, diff_subln,
           moe_w_group, moe_b_group, moe_w_expert, moe_b_expert, moe_w_gate, moe_w_up, moe_w_down):
    batch, seq_len, _ = x_prompt.shape
    db, dq, _ = x_sample.shape
    n_pool = cache_k_moba.shape[1]
    depth = norm_attn.shape[0]
    nb = seq_len // MOBA_BLOCK
    slopes_moba = _alibi_slopes(MOBA_HEADS)
    slopes_diff = _alibi_slopes(DIFF_HEADS)
    qi_new = np.arange(dq, dtype=np.float32)

    moba_pool_t = lambda c: jnp.transpose(c, (0, 1, 3, 4, 2)).reshape(-1, MOBA_KVW, PAGE_SIZE)
    diff_pool = lambda c: c.reshape(-1, PAGE_SIZE * DIFF_HEADS, DIFF_HW)
    k_moba_t, v_moba_t = moba_pool_t(cache_k_moba), moba_pool_t(cache_v_moba)
    k_diff, v_diff = diff_pool(cache_k_diff), diff_pool(cache_v_diff)

    yp = x_prompt.reshape(batch * seq_len, D_MODEL)
    ys = x_sample.reshape(db * dq, D_MODEL)
    outs = {name: [] for name in ("km_p", "vm_p", "kd_p", "vd_p", "km_s", "vm_s", "kd_s", "vd_s")}
    for i in range(depth):
        j = i // 2
        final = i == depth - 1
        if i % 2 == 0:
            w_bf = moba_w_qkv[j].astype(BF16)
            wkt_bf = moba_w_qkv[j][:, D_MODEL:D_MODEL + MOBA_KVW].T.astype(BF16)
            qp, kp, vp, kt, vh, kmean = _norm_qkv(yp, norm_attn[i], w_bf, mode="moba",
                                                  seq_len=seq_len, wkt_bf=wkt_bf)
            km = kmean.reshape(batch, nb, MOBA_KV_HEADS, HEAD_DIM).transpose(0, 2, 3, 1)
            km_rep = jnp.pad(jnp.tile(km, (1, 1, MOBA_GROUP, 1)),
                             ((0, 0), (0, 0), (0, 0), (0, LANES - nb)))
            attn_p = _moba_prompt(qp, km_rep, kt, vh, jnp.asarray(slopes_moba))

            qs, ks, vs = _norm_qkv(ys, norm_attn[i], moba_w_qkv[j], mode="plain")
            kmt = _kmean_pages(k_moba_t, page_table, j * n_pool)
            q5 = qs.reshape(db, dq, MOBA_KV_HEADS, MOBA_GROUP, HEAD_DIM).transpose(0, 2, 3, 1, 4)
            qf = _block_diag_rows(q5, MOBA_KV_HEADS)
            slope_rows = np.repeat(slopes_moba, dq).reshape(-1, 1)
            qi_rows = np.tile(qi_new, MOBA_HEADS).reshape(-1, 1)
            new_t = lambda x: _pad_tokens(x.reshape(db, dq, MOBA_KVW)).transpose(0, 2, 1)
            o_rows = _moba_decode(qf * SCALE, qf, kmt, jnp.asarray(slope_rows), jnp.asarray(qi_rows),
                                  new_t(ks), new_t(vs), k_moba_t, v_moba_t, page_table, j * n_pool)
            o6 = o_rows.reshape(db, MOBA_KV_HEADS, MOBA_GROUP, dq, MOBA_KV_HEADS, HEAD_DIM)
            o5 = jnp.stack([o6[:, g, :, :, g, :] for g in range(MOBA_KV_HEADS)], axis=1)
            attn_s = o5.transpose(0, 3, 1, 2, 4).reshape(db * dq, D_MODEL)
            w_o = moba_w_o[j]
            kvh, kvd = MOBA_KV_HEADS, HEAD_DIM
            names = ("km_p", "vm_p", "km_s", "vm_s")
        else:
            lambda_init = 0.8 - 0.6 * math.exp(-0.3 * i)
            w_bf = diff_w_qkv[j].astype(BF16)
            lams = [a[j].reshape(1, HEAD_DIM) for a in
                    (diff_lambda_q1, diff_lambda_k1, diff_lambda_q2, diff_lambda_k2)]
            subln_w = diff_subln[j].reshape(1, DIFF_HW)
            qp, kp, vp, kb, vb = _norm_qkv(yp, norm_attn[i], w_bf, mode="diff")
            attn_p = _diff_prompt(qp, kb, vb, lams, subln_w, jnp.asarray(slopes_diff),
                                  batch=batch, lambda_init=lambda_init)

            qs, ks, vs = _norm_qkv(ys, norm_attn[i], diff_w_qkv[j], mode="plain")
            q5 = qs.reshape(db, dq, DIFF_HEADS, 2, 1, HEAD_DIM).transpose(0, 2, 3, 4, 1, 5)
            qx = _block_diag_rows(q5.reshape(db * DIFF_HEADS, 2, 1, dq, HEAD_DIM), 2)
            qx = (qx.reshape(db, 2 * DIFF_HEADS * dq, DIFF_HW) * SCALE).astype(BF16)
            slope_rows = np.repeat(slopes_diff, 2 * dq).reshape(-1, 1)
            qi_rows = np.tile(qi_new, 2 * DIFF_HEADS).reshape(-1, 1)
            head_rows = np.repeat(np.arange(DIFF_HEADS, dtype=np.int32), 2 * dq).reshape(-1, 1)
            new_flat = lambda x: _pad_tokens(x.reshape(db, dq, DIFF_HEADS, DIFF_HW)).reshape(
                db, PAGE_SIZE * DIFF_HEADS, DIFF_HW)
            attn_s = _diff_decode(qx, jnp.asarray(slope_rows), jnp.asarray(qi_rows),
                                  jnp.asarray(head_rows), new_flat(ks), new_flat(vs), lams, subln_w,
                                  k_diff, v_diff, page_table, j * n_pool,
                                  lambda_init=lambda_init).reshape(db * dq, D_MODEL)
            w_o = diff_w_o[j]
            kvh, kvd = DIFF_HEADS, DIFF_HW
            names = ("kd_p", "vd_p", "kd_s", "vd_s")
        outs[names[0]].append(kp.reshape(batch, seq_len, kvh, kvd))
        outs[names[1]].append(vp.reshape(batch, seq_len, kvh, kvd))
        outs[names[2]].append(ks.reshape(db, dq, kvh, kvd))
        outs[names[3]].append(vs.reshape(db, dq, kvh, kvd))

        moe_w, (wr, br) = _moe_weights(moe_w_group[i], moe_b_group[i], moe_w_expert[i],
                                       moe_b_expert[i], moe_w_gate[i], moe_w_up[i], moe_w_down[i])
        yp = _wo_moe(yp, attn_p, w_o.astype(BF16), norm_ffn[i], *moe_w, norm_final, final=final)
        ys = _wo_moe_f32(ys, attn_s, w_o, norm_ffn[i], wr, br, moe_w_gate[i], moe_w_up[i],
                         moe_w_down[i], norm_final, final=final)

    stack = lambda name: jnp.stack(outs[name])
    return (yp.reshape(batch, seq_len, D_MODEL), ys.reshape(db, dq, D_MODEL),
            stack("km_p"), stack("vm_p"), stack("kd_p"), stack("vd_p"),
            stack("km_s"), stack("vm_s"), stack("kd_s"), stack("vd_s"))
```

```python
import functools
import math

import jax
import jax.numpy as jnp
import numpy as np
from jax import lax
from jax.experimental import pallas as pl
from jax.experimental.pallas import tpu as pltpu

D_MODEL = 1024
HEAD_DIM = 64
HEAD_DIM_LOG2 = 6
MOBA_HEADS = D_MODEL // HEAD_DIM
MOBA_KV_HEADS = 4
MOBA_GROUP = MOBA_HEADS // MOBA_KV_HEADS
MOBA_BLOCK = 256
MOBA_BLOCK_LOG2 = 8
MOBA_TOPK = 3
MOBA_KVW = MOBA_KV_HEADS * HEAD_DIM
DIFF_HEADS = D_MODEL // (2 * HEAD_DIM)
DIFF_HEADS_LOG2 = 3
DIFF_HW = 2 * HEAD_DIM
N_GROUPS = 4
EXPERTS_PER_GROUP = 4
N_EXPERTS = N_GROUPS * EXPERTS_PER_GROUP
D_EXPERT = 256
PAGE_SIZE = 128
NORM_EPS = 1e-6
SUBLN_EPS = 1e-5
NEG_INF = -1e30
SCALE = HEAD_DIM ** -0.5
LOG2E = math.log2(math.e)

LANES = 128
ROUTER_LANES = LANES
VMEM_LIMIT = 56 * 1024 * 1024

F32 = jnp.float32
BF16 = jnp.bfloat16
NT_DIMS = (((1,), (1,)), ((), ()))


def _alibi_slopes(n_heads):
    start = 2.0 ** (-8.0 / n_heads)
    return np.array([start ** (i + 1) for i in range(n_heads)], dtype=np.float32)


def _params(*sem):
    return pltpu.CompilerParams(dimension_semantics=sem, vmem_limit_bytes=VMEM_LIMIT)


def _resident(shape):
    nd = len(shape)
    return pl.BlockSpec(shape, lambda *_: (0,) * nd, pipeline_mode=pl.Buffered(1))


def _rms(x, w, eps):
    ms = jnp.mean(x * x, axis=-1, keepdims=True)
    return x * lax.rsqrt(ms + eps) * w


def _split_bf16(x):
    hi = x.astype(BF16)
    return hi, (x - hi.astype(F32)).astype(BF16)


def _nt_dot(a, b):
    return lax.dot_general(a, b, NT_DIMS, preferred_element_type=F32)


def _dot_f32(a, b):
    return jnp.dot(a, b, precision=lax.Precision.HIGHEST, preferred_element_type=F32)


def _norm_qkv_kernel(x_ref, nw_ref, w_ref, *refs, mode, kvw, tm):
    h = _rms(x_ref[...], nw_ref[...], NORM_EPS)
    if w_ref.dtype == F32:
        qkv = _dot_f32(h, w_ref[...])
    else:
        h = h.astype(BF16)
        qkv = jnp.dot(h, w_ref[...], preferred_element_type=F32)
    q = qkv[:, :D_MODEL]
    k = qkv[:, D_MODEL:D_MODEL + kvw]
    v = qkv[:, D_MODEL + kvw:]
    if mode == "plain":
        q_ref, k_ref, v_ref = refs
    elif mode == "moba":
        wkt_ref, q_ref, k_ref, v_ref, kt_ref, vh_ref, km_ref = refs
        kt = _nt_dot(wkt_ref[...], h)
        vb = v.astype(BF16)
        for jj in range(tm // MOBA_BLOCK):
            sl = slice(jj * MOBA_BLOCK, (jj + 1) * MOBA_BLOCK)
            kt_ref[0, jj] = kt[:, sl].astype(BF16)
            km_ref[jj] = jnp.mean(k[sl], axis=0, keepdims=True)
        for g in range(MOBA_KV_HEADS):
            vh_ref[0, g] = vb[:, g * HEAD_DIM:(g + 1) * HEAD_DIM]
    else:
        q_ref, k_ref, v_ref, kb_ref, vb_ref = refs
        kb_ref[...] = k.astype(BF16)
        vb_ref[...] = v.astype(BF16)
    q_ref[...] = q
    k_ref[...] = k
    v_ref[...] = v


def _norm_qkv(x, nw, w, *, mode, seq_len=None, wkt_bf=None):
    n = x.shape[0]
    e = w.shape[1]
    kvw = (e - D_MODEL) // 2
    tm = min(512, n)
    nt = n // tm
    row = lambda t: (t, 0)
    in_specs = [pl.BlockSpec((tm, D_MODEL), row), _resident((1, D_MODEL)), _resident((D_MODEL, e))]
    args = [x, nw.reshape(1, D_MODEL), w]
    out_shape = [jax.ShapeDtypeStruct((n, D_MODEL), F32),
                 jax.ShapeDtypeStruct((n, kvw), F32),
                 jax.ShapeDtypeStruct((n, kvw), F32)]
    out_specs = [pl.BlockSpec((tm, D_MODEL), row), pl.BlockSpec((tm, kvw), row),
                 pl.BlockSpec((tm, kvw), row)]
    if mode == "moba":
        b = n // seq_len
        tps = seq_len // tm
        bpt = tm // MOBA_BLOCK
        nb = seq_len // MOBA_BLOCK
        in_specs.append(_resident((kvw, D_MODEL)))
        args.append(wkt_bf)
        out_shape += [jax.ShapeDtypeStruct((b, nb, kvw, MOBA_BLOCK), BF16),
                      jax.ShapeDtypeStruct((b, MOBA_KV_HEADS, seq_len, HEAD_DIM), BF16),
                      jax.ShapeDtypeStruct((b * nb, 1, kvw), F32)]
        out_specs += [pl.BlockSpec((1, bpt, kvw, MOBA_BLOCK), lambda t: (t // tps, t % tps, 0, 0)),
                      pl.BlockSpec((1, MOBA_KV_HEADS, tm, HEAD_DIM), lambda t: (t // tps, 0, t % tps, 0)),
                      pl.BlockSpec((bpt, 1, kvw), lambda t: (t, 0, 0))]
    elif mode == "diff":
        out_shape += [jax.ShapeDtypeStruct((n, kvw), BF16), jax.ShapeDtypeStruct((n, kvw), BF16)]
        out_specs += [pl.BlockSpec((tm, kvw), row), pl.BlockSpec((tm, kvw), row)]
    return pl.pallas_call(
        functools.partial(_norm_qkv_kernel, mode=mode, kvw=kvw, tm=tm),
        grid=(nt,), in_specs=in_specs, out_specs=out_specs, out_shape=out_shape,
        compiler_params=_params("parallel"), name="norm_qkv_" + mode,
    )(*args)


def _softmax_update(s, m_ref, l_ref, rows, shift=0.0, base2=False):
    width = s.shape[1]
    exp = jnp.exp2 if base2 else jnp.exp
    m_prev = m_ref[rows, :]
    m_new = jnp.maximum(m_prev, jnp.max(s, axis=1, keepdims=True) + shift)
    alpha = exp(m_prev - m_new)
    p = exp(s - jnp.concatenate([m_new - shift] * (width // LANES), axis=1))
    l_ref[rows, :] = alpha * l_ref[rows, :] + jnp.sum(p, axis=1, keepdims=True)
    m_ref[rows, :] = m_new
    return p, alpha


def _init_softmax_state(m_ref, l_ref, acc_ref):
    m_ref[...] = jnp.full(m_ref.shape, -jnp.inf, F32)
    l_ref[...] = jnp.zeros(l_ref.shape, F32)
    acc_ref[...] = jnp.zeros(acc_ref.shape, F32)


def _widen(x, width):
    return jnp.concatenate([x] * (width // LANES), axis=1)


def _top3_mask(gate, lane, n_valid):
    gt = jnp.where(lane < n_valid, gate, -jnp.inf)
    sel = jnp.zeros(gate.shape, F32)
    for _ in range(MOBA_TOPK):
        mx = jnp.max(gt, axis=1, keepdims=True)
        idx = jnp.min(jnp.where(gt == mx, lane, gate.shape[1]), axis=1, keepdims=True)
        hit = (lane == idx) & (mx > -jnp.inf)
        sel = jnp.where(hit, 1.0, sel)
        gt = jnp.where(hit, -jnp.inf, gt)
    return sel


def _moba_prompt_kernel(slopes_ref, q_ref, km_ref, kt_ref, v_ref, o_ref,
                        qz_ref, sel_ref, bias_ref, m_ref, l_ref, acc_ref,
                        sa_ref, sb_ref, pa_ref, pb_ref, alphaa_ref, alphab_ref):
    tq = MOBA_BLOCK
    g = pl.program_id(1)
    i = pl.program_id(2)
    q = q_ref[...]
    gate = _dot_f32(q, km_ref[0, 0])
    lane = lax.broadcasted_iota(jnp.int32, (tq, LANES), 1)
    sel_ref[...] = _top3_mask(gate, lane, i)
    qs = q * (SCALE * LOG2E)
    qlane = lax.shift_right_logical(lax.broadcasted_iota(jnp.int32, qs.shape, 1), HEAD_DIM_LOG2)
    for r in range(MOBA_GROUP):
        qz_ref[r * tq:(r + 1) * tq, :] = jnp.where(qlane == r, qs, 0.0).astype(BF16)
    _init_softmax_state(m_ref, l_ref, acc_ref)
    tk = 2 * tq
    n_tiles = i // 2 + 1
    slope2 = [slopes_ref[g * MOBA_GROUP + r] * LOG2E for r in range(MOBA_GROUP)]

    @pl.when(i == 0)
    def _():
        rel = (lax.broadcasted_iota(jnp.int32, (tq, tk), 1)
               - lax.broadcasted_iota(jnp.int32, (tq, tk), 0)).astype(F32)
        for r in range(MOBA_GROUP):
            bias_ref[r * tq:(r + 1) * tq, :] = slope2[r] * rel

    causal = (lax.broadcasted_iota(jnp.int32, (tq, tq), 1)
              <= lax.broadcasted_iota(jnp.int32, (tq, tq), 0))

    def block_mask(b):
        picked = jnp.sum(jnp.where(lane == b, sel_ref[...], 0.0), axis=1, keepdims=True) > 0.0
        return jnp.where(picked | ((b == i) & causal), 0.0, NEG_INF)

    def scores(t, s_ref):
        tc = jnp.minimum(t, n_tiles - 1)
        kt = jnp.concatenate([kt_ref[0, 2 * tc], kt_ref[0, 2 * tc + 1]], axis=1)
        kt4 = jnp.concatenate([kt] * MOBA_GROUP, axis=0)
        s_ref[...] = jnp.dot(qz_ref[...], kt4, preferred_element_type=F32)

    def softmax(t, s_ref, p_ref, alpha_ref):
        off = (t * tk - i * tq).astype(F32)
        mask = jnp.concatenate([block_mask(2 * t), block_mask(2 * t + 1)], axis=1)
        for r in range(MOBA_GROUP):
            rows = slice(r * tq, (r + 1) * tq)
            p, alpha = _softmax_update(s_ref[rows, :] + bias_ref[rows, :] + mask, m_ref, l_ref,
                                       rows, shift=slope2[r] * off, base2=True)
            p_ref[rows, :] = p.astype(BF16)
            alpha_ref[rows, :] = alpha

    def accumulate(t, p_ref, alpha_ref):
        tc = jnp.minimum(t, n_tiles - 1)
        vt = v_ref[0, 0, pl.ds(pl.multiple_of(tc * tk, tk), tk), :]
        acc_ref[...] = acc_ref[...] * alpha_ref[:, :HEAD_DIM] + jnp.dot(
            p_ref[...], vt, preferred_element_type=F32)

    pb_ref[...] = jnp.zeros(pb_ref.shape, BF16)
    alphab_ref[...] = jnp.ones(alphab_ref.shape, F32)
    scores(0, sa_ref)

    def trip(u, carry):
        t0, t1 = 2 * u, 2 * u + 1
        scores(t1, sb_ref)
        accumulate(jnp.maximum(t0 - 1, 0), pb_ref, alphab_ref)
        softmax(t0, sa_ref, pa_ref, alphaa_ref)
        scores(t1 + 1, sa_ref)
        accumulate(t0, pa_ref, alphaa_ref)
        softmax(t1, sb_ref, pb_ref, alphab_ref)
        return carry

    n_trips = (n_tiles + 1) // 2
    lax.fori_loop(0, n_trips, trip, 0)
    accumulate(2 * n_trips - 1, pb_ref, alphab_ref)
    out = [acc_ref[r * tq:(r + 1) * tq, :] / l_ref[r * tq:(r + 1) * tq, :HEAD_DIM]
           for r in range(MOBA_GROUP)]
    o_ref[...] = jnp.concatenate(out, axis=1).astype(BF16)


def _moba_prompt(q, km_rep, kt, vh, slopes):
    b, nb = kt.shape[0], kt.shape[1]
    seq_len = nb * MOBA_BLOCK
    tq = MOBA_BLOCK
    gw = MOBA_GROUP * HEAD_DIM
    return pl.pallas_call(
        _moba_prompt_kernel,
        grid=(b, MOBA_KV_HEADS, nb),
        in_specs=[pl.BlockSpec(memory_space=pltpu.SMEM),
                  pl.BlockSpec((tq, gw), lambda bi, g, i: (bi * nb + i, g)),
                  pl.BlockSpec((1, 1, gw, LANES), lambda bi, g, i: (bi, g, 0, 0)),
                  pl.BlockSpec((1, nb, HEAD_DIM, MOBA_BLOCK), lambda bi, g, i: (bi, 0, g, 0)),
                  pl.BlockSpec((1, 1, seq_len, HEAD_DIM), lambda bi, g, i: (bi, g, 0, 0))],
        out_specs=pl.BlockSpec((tq, gw), lambda bi, g, i: (bi * nb + i, g)),
        out_shape=jax.ShapeDtypeStruct((b * seq_len, D_MODEL), BF16),
        scratch_shapes=[pltpu.VMEM((MOBA_GROUP * tq, gw), BF16),
                        pltpu.VMEM((tq, LANES), F32),
                        pltpu.VMEM((MOBA_GROUP * tq, 2 * tq), F32),
                        pltpu.VMEM((MOBA_GROUP * tq, LANES), F32),
                        pltpu.VMEM((MOBA_GROUP * tq, LANES), F32),
                        pltpu.VMEM((MOBA_GROUP * tq, HEAD_DIM), F32),
                        pltpu.VMEM((MOBA_GROUP * tq, 2 * tq), F32),
                        pltpu.VMEM((MOBA_GROUP * tq, 2 * tq), F32),
                        pltpu.VMEM((MOBA_GROUP * tq, 2 * tq), BF16),
                        pltpu.VMEM((MOBA_GROUP * tq, 2 * tq), BF16),
                        pltpu.VMEM((MOBA_GROUP * tq, LANES), F32),
                        pltpu.VMEM((MOBA_GROUP * tq, LANES), F32)],
        compiler_params=_params("parallel", "parallel", "arbitrary"), name="moba_prompt",
    )(slopes, q, km_rep, kt, vh)


def _lambda_full(lq1_ref, lk1_ref, lq2_ref, lk2_ref, lambda_init):
    a = jnp.sum(lq1_ref[...] * lk1_ref[...], axis=1, keepdims=True)
    b = jnp.sum(lq2_ref[...] * lk2_ref[...], axis=1, keepdims=True)
    return jnp.exp(a) - jnp.exp(b) + lambda_init


def _subln(o, w, lambda_init):
    return _rms(o, w, SUBLN_EPS) * (1.0 - lambda_init)


def _diff_prompt_kernel(slopes_ref, q_ref, k_ref, v_ref, lq1_ref, lk1_ref, lq2_ref, lk2_ref,
                        sw_ref, o_ref, qz_ref, bias_ref, m_ref, l_ref, acc_ref,
                        sa_ref, sb_ref, pa_ref, pb_ref, alphaa_ref, alphab_ref, *, tq, lambda_init):
    hd = pl.program_id(1)
    i = pl.program_id(2)
    qs = q_ref[...] * (SCALE * LOG2E)
    qlane = lax.shift_right_logical(lax.broadcasted_iota(jnp.int32, qs.shape, 1), HEAD_DIM_LOG2)
    for mp in range(2):
        qz_ref[mp * tq:(mp + 1) * tq, :] = jnp.where(qlane == mp, qs, 0.0).astype(BF16)
    _init_softmax_state(m_ref, l_ref, acc_ref)
    slope = slopes_ref[hd] * LOG2E
    everything = slice(None)
    tk = 2 * tq
    n_tiles = i // 2 + 1
    half = i % 2

    @pl.when(i == 0)
    def _():
        rows2 = lax.broadcasted_iota(jnp.int32, (2 * tq, tk), 0)
        rel = (lax.broadcasted_iota(jnp.int32, (2 * tq, tk), 1)
               - jnp.where(rows2 >= tq, rows2 - tq, rows2))
        alibi = slope * rel.astype(F32)
        bias_ref[0] = alibi
        bias_ref[1] = jnp.where(rel <= 0, alibi, NEG_INF)
        bias_ref[2] = jnp.where(rel <= tq, alibi, NEG_INF)
        bias_ref[3] = jnp.full((2 * tq, tk), NEG_INF, F32)

    def key_rows(t):
        return pl.ds(pl.multiple_of(jnp.minimum(t, n_tiles - 1) * tk, tk), tk)

    def scores(t, s_ref):
        which = jnp.where(t < n_tiles - 1, 0, jnp.where(t == n_tiles - 1, 1 + half, 3))
        s_ref[...] = _nt_dot(qz_ref[...], k_ref[key_rows(t), :]) + bias_ref[which]

    def softmax(t, s_ref, p_ref, alpha_ref):
        p, alpha = _softmax_update(s_ref[...], m_ref, l_ref, everything,
                                   shift=slope * (t * tk - i * tq).astype(F32), base2=True)
        p_ref[...] = p.astype(BF16)
        alpha_ref[...] = alpha

    def accumulate(t, p_ref, alpha_ref):
        acc_ref[...] = acc_ref[...] * alpha_ref[...] + jnp.dot(
            p_ref[...], v_ref[key_rows(t), :], preferred_element_type=F32)

    pb_ref[...] = jnp.zeros(pb_ref.shape, BF16)
    alphab_ref[...] = jnp.ones(alphab_ref.shape, F32)
    scores(0, sa_ref)

    def trip(u, carry):
        t0, t1 = 2 * u, 2 * u + 1
        scores(t1, sb_ref)
        accumulate(jnp.maximum(t0 - 1, 0), pb_ref, alphab_ref)
        softmax(t0, sa_ref, pa_ref, alphaa_ref)
        scores(t1 + 1, sa_ref)
        accumulate(t0, pa_ref, alphaa_ref)
        softmax(t1, sb_ref, pb_ref, alphab_ref)
        return carry

    n_trips = (n_tiles + 1) // 2
    lax.fori_loop(0, n_trips, trip, 0)
    accumulate(2 * n_trips - 1, pb_ref, alphab_ref)
    o1 = acc_ref[:tq, :] / l_ref[:tq, :]
    o2 = acc_ref[tq:, :] / l_ref[tq:, :]
    lam = _lambda_full(lq1_ref, lk1_ref, lq2_ref, lk2_ref, lambda_init)
    o_ref[...] = _subln(o1 - lam * o2, sw_ref[...], lambda_init).astype(BF16)


def _diff_prompt(q, kb, vb, lams, subln_w, slopes, *, batch, lambda_init):
    n = q.shape[0]
    seq_len = n // batch
    tq = 256
    nq = seq_len // tq
    lam_specs = [_resident((1, HEAD_DIM))] * 4
    return pl.pallas_call(
        functools.partial(_diff_prompt_kernel, tq=tq, lambda_init=lambda_init),
        grid=(batch, DIFF_HEADS, nq),
        in_specs=[pl.BlockSpec(memory_space=pltpu.SMEM),
                  pl.BlockSpec((tq, DIFF_HW), lambda bi, h, i: (bi * nq + i, h)),
                  pl.BlockSpec((seq_len, DIFF_HW), lambda bi, h, i: (bi, h)),
                  pl.BlockSpec((seq_len, DIFF_HW), lambda bi, h, i: (bi, h)),
                  *lam_specs, _resident((1, DIFF_HW))],
        out_specs=pl.BlockSpec((tq, DIFF_HW), lambda bi, h, i: (bi * nq + i, h)),
        out_shape=jax.ShapeDtypeStruct((n, D_MODEL), BF16),
        scratch_shapes=[pltpu.VMEM((2 * tq, DIFF_HW), BF16),
                        pltpu.VMEM((4, 2 * tq, 2 * tq), F32),
                        pltpu.VMEM((2 * tq, LANES), F32),
                        pltpu.VMEM((2 * tq, LANES), F32),
                        pltpu.VMEM((2 * tq, DIFF_HW), F32),
                        pltpu.VMEM((2 * tq, 2 * tq), F32), pltpu.VMEM((2 * tq, 2 * tq), F32),
                        pltpu.VMEM((2 * tq, 2 * tq), BF16), pltpu.VMEM((2 * tq, 2 * tq), BF16),
                        pltpu.VMEM((2 * tq, LANES), F32), pltpu.VMEM((2 * tq, LANES), F32)],
        compiler_params=_params("parallel", "parallel", "arbitrary"), name="diff_prompt",
    )(slopes, q, kb, vb, *lams, subln_w)


def _page_specs(block_shape, pages_per_step, page_offset):
    def spec(t):
        return pl.BlockSpec((1,) + block_shape,
                            lambda b, s, pt: (pt[b, s * pages_per_step + t] + page_offset, 0, 0))
    return [spec(t) for t in range(pages_per_step)]


def _kmean_pages_kernel(pt_ref, *refs):
    k_refs, o_ref = refs[:-1], refs[-1]
    step = pl.program_id(1)
    ppb = MOBA_BLOCK // PAGE_SIZE
    blocks_per_step = len(k_refs) // ppb
    lane = lax.broadcasted_iota(jnp.int32, o_ref.shape[1:], 1)
    upd = jnp.zeros(o_ref.shape[1:], F32)
    for t in range(blocks_per_step):
        tot = sum(k_refs[t * ppb + u][0] for u in range(ppb))
        col = jnp.sum(tot, axis=1, keepdims=True) / MOBA_BLOCK
        upd = jnp.where(lane == step * blocks_per_step + t, col, upd)

    @pl.when(step == 0)
    def _():
        o_ref[0] = upd

    @pl.when(step > 0)
    def _():
        o_ref[0] += upd


def _kmean_pages(pool_t, page_table, page_offset):
    db, n_pages = page_table.shape
    pps = 16
    width = pool_t.shape[1]
    grid_spec = pltpu.PrefetchScalarGridSpec(
        num_scalar_prefetch=1, grid=(db, n_pages // pps),
        in_specs=_page_specs((width, PAGE_SIZE), pps, page_offset),
        out_specs=pl.BlockSpec((1, width, LANES), lambda b, s, pt: (b, 0, 0)))
    return pl.pallas_call(
        _kmean_pages_kernel, grid_spec=grid_spec,
        out_shape=jax.ShapeDtypeStruct((db, width, LANES), F32),
        compiler_params=_params("parallel", "arbitrary"), name="kmean_pages",
    )(page_table, *([pool_t] * pps))


def _moba_decode_kernel(pt_ref, qs_ref, qf_ref, kmt_ref, slope_ref, qi_ref, knew_ref, vnew_ref,
                        *refs, pps, past_len):
    k_refs, v_refs = refs[:pps], refs[pps:2 * pps]
    o_ref, m_ref, l_ref, acc_ref, sel_ref = refs[2 * pps:]
    step = pl.program_id(1)
    n_rows = qs_ref.shape[1]
    chunk = pps * PAGE_SIZE
    everything = slice(None)

    @pl.when(step == 0)
    def _():
        _init_softmax_state(m_ref, l_ref, acc_ref)
        gr = _dot_f32(qf_ref[0], kmt_ref[0])
        n_blocks = past_len // MOBA_BLOCK
        rpg = n_rows // MOBA_KV_HEADS
        dq = rpg // MOBA_GROUP
        lane = lax.broadcasted_iota(jnp.int32, (dq, LANES), 1)
        for g in range(MOBA_KV_HEADS):
            gs = sum(gr[g * rpg + r * dq:g * rpg + (r + 1) * dq] for r in range(MOBA_GROUP))
            sel = _top3_mask(gs, lane, n_blocks)
            sel_ref[g * rpg:(g + 1) * rpg, :] = jnp.concatenate([sel] * MOBA_GROUP, axis=0)

    qh, ql = _split_bf16(qs_ref[0])
    q2 = jnp.concatenate([qh, ql], axis=0)

    def qk(kt32):
        kh, kl = _split_bf16(kt32)
        s2 = jnp.dot(q2, kh, preferred_element_type=F32)
        return s2[:n_rows] + s2[n_rows:] + jnp.dot(qh, kl, preferred_element_type=F32)

    def pv(p, vt32):
        ph, plo = _split_bf16(p)
        vh, vl = _split_bf16(vt32)
        o2 = _nt_dot(jnp.concatenate([ph, plo], axis=0), vh)
        return o2[:n_rows] + o2[n_rows:] + _nt_dot(ph, vl)

    def update(kt32, vt32, tok, ok, shift):
        s = qk(kt32) + slope_ref[...] * (tok - qi_ref[...])
        s = jnp.where(ok, s, NEG_INF)
        p, alpha = _softmax_update(s, m_ref, l_ref, everything, shift=shift)
        acc_ref[...] = acc_ref[...] * _widen(alpha, acc_ref.shape[1]) + pv(p, vt32)

    kt = jnp.concatenate([r[0] for r in k_refs], axis=1)
    vt = jnp.concatenate([r[0] for r in v_refs], axis=1)
    tok = lax.broadcasted_iota(jnp.int32, (1, chunk), 1).astype(F32)
    blk = lax.broadcasted_iota(jnp.int32, (LANES, chunk), 0)
    key = step * chunk + lax.broadcasted_iota(jnp.int32, (LANES, chunk), 1)
    expand = jnp.where(blk == lax.shift_right_logical(key, MOBA_BLOCK_LOG2), 1.0, 0.0).astype(BF16)
    ok = jnp.dot(sel_ref[...].astype(BF16), expand, preferred_element_type=F32) > 0.5
    update(kt, vt, tok, ok, slope_ref[...] * (step * chunk - past_len).astype(F32))

    @pl.when(step == pl.num_programs(1) - 1)
    def _():
        tok_new = lax.broadcasted_iota(jnp.int32, (1, PAGE_SIZE), 1).astype(F32)
        update(knew_ref[0], vnew_ref[0], tok_new, tok_new <= qi_ref[...], 0.0)
        o_ref[0] = acc_ref[...] / _widen(l_ref[...], acc_ref.shape[1])


def _moba_decode(q_scaled, q_plain, kmt, slope_rows, qi_rows, knew_t, vnew_t, k_pool_t, v_pool_t,
                 page_table, page_offset):
    db, n_pages = page_table.shape
    n_rows, width = q_scaled.shape[1], q_scaled.shape[2]
    pps = 16
    per_seq = lambda shape: pl.BlockSpec((1,) + shape, lambda b, s, pt: (b, 0, 0))
    const = lambda shape: pl.BlockSpec(shape, lambda b, s, pt: (0, 0))
    grid_spec = pltpu.PrefetchScalarGridSpec(
        num_scalar_prefetch=1, grid=(db, n_pages // pps),
        in_specs=[per_seq((n_rows, width)), per_seq((n_rows, width)), per_seq((width, LANES)),
                  const((n_rows, 1)), const((n_rows, 1)),
                  per_seq((width, PAGE_SIZE)), per_seq((width, PAGE_SIZE)),
                  *(_page_specs((width, PAGE_SIZE), pps, page_offset) * 2)],
        out_specs=per_seq((n_rows, width)),
        scratch_shapes=[pltpu.VMEM((n_rows, LANES), F32), pltpu.VMEM((n_rows, LANES), F32),
                        pltpu.VMEM((n_rows, width), F32), pltpu.VMEM((n_rows, LANES), F32)])
    return pl.pallas_call(
        functools.partial(_moba_decode_kernel, pps=pps, past_len=n_pages * PAGE_SIZE),
        grid_spec=grid_spec, out_shape=jax.ShapeDtypeStruct((db, n_rows, width), F32),
        compiler_params=_params("parallel", "arbitrary"), name="moba_decode",
    )(page_table, q_scaled, q_plain, kmt, slope_rows, qi_rows, knew_t, vnew_t,
      *([k_pool_t] * pps), *([v_pool_t] * pps))


def _diff_decode_kernel(pt_ref, qx_ref, slope_ref, qi_ref, head_ref, knew_ref, vnew_ref,
                        lq1_ref, lk1_ref, lq2_ref, lk2_ref, sw_ref, *refs, pps, past_len, lambda_init):
    k_refs, v_refs = refs[:pps], refs[pps:2 * pps]
    o_ref, bias_ref, m_ref, l_ref, acc_ref = refs[2 * pps:]
    step = pl.program_id(1)
    n_rows = qx_ref.shape[1]
    chunk = pps * PAGE_SIZE
    everything = slice(None)
    slope2 = slope_ref[...] * LOG2E

    def alibi_bias(n_cols):
        col = lax.broadcasted_iota(jnp.int32, (n_rows, n_cols), 1)
        tok = lax.shift_right_logical(col, DIFF_HEADS_LOG2)
        own_head = (col & (DIFF_HEADS - 1)) == head_ref[...]
        tokf = tok.astype(F32)
        return jnp.where(own_head, slope2 * (tokf - qi_ref[...]), NEG_INF), tokf

    @pl.when(step == 0)
    def _():
        _init_softmax_state(m_ref, l_ref, acc_ref)
        bias_ref[...] = alibi_bias(chunk * DIFF_HEADS)[0]

    def update(k32, v32, bias, shift):
        s = _nt_dot(qx_ref[0], k32.astype(BF16)) + bias
        p, alpha = _softmax_update(s, m_ref, l_ref, everything, shift=shift, base2=True)
        acc_ref[...] = acc_ref[...] * alpha + jnp.dot(p.astype(BF16), v32.astype(BF16),
                                                      preferred_element_type=F32)

    kc = jnp.concatenate([r[0] for r in k_refs], axis=0)
    vc = jnp.concatenate([r[0] for r in v_refs], axis=0)
    update(kc, vc, bias_ref[...], slope2 * (step * chunk - past_len).astype(F32))

    @pl.when(step == pl.num_programs(1) - 1)
    def _():
        bias_new, tokf = alibi_bias(PAGE_SIZE * DIFF_HEADS)
        update(knew_ref[0], vnew_ref[0], jnp.where(tokf <= qi_ref[...], bias_new, NEG_INF), 0.0)
        o = acc_ref[...] / l_ref[...]
        dq = n_rows // (2 * DIFF_HEADS)
        lam = _lambda_full(lq1_ref, lk1_ref, lq2_ref, lk2_ref, lambda_init)
        outs = []
        for hd in range(DIFF_HEADS):
            o1 = o[(2 * hd) * dq:(2 * hd + 1) * dq]
            o2 = o[(2 * hd + 1) * dq:(2 * hd + 2) * dq]
            outs.append(_subln(o1 - lam * o2, sw_ref[...], lambda_init))
        o_ref[0] = jnp.concatenate(outs, axis=1)


def _diff_decode(qx, slope_rows, qi_rows, head_rows, knew, vnew, lams, subln_w, k_pool, v_pool,
                 page_table, page_offset, *, lambda_init):
    db, n_pages = page_table.shape
    n_rows = qx.shape[1]
    pps = 8
    page_rows = PAGE_SIZE * DIFF_HEADS
    dq = n_rows // (2 * DIFF_HEADS)
    per_seq = lambda shape: pl.BlockSpec((1,) + shape, lambda b, s, pt: (b, 0, 0))
    const = lambda shape: pl.BlockSpec(shape, lambda b, s, pt: (0, 0))
    grid_spec = pltpu.PrefetchScalarGridSpec(
        num_scalar_prefetch=1, grid=(db, n_pages // pps),
        in_specs=[per_seq((n_rows, DIFF_HW)), const((n_rows, 1)), const((n_rows, 1)),
                  const((n_rows, 1)), per_seq((page_rows, DIFF_HW)), per_seq((page_rows, DIFF_HW)),
                  *([const((1, HEAD_DIM))] * 4), const((1, DIFF_HW)),
                  *(_page_specs((page_rows, DIFF_HW), pps, page_offset) * 2)],
        out_specs=per_seq((dq, D_MODEL)),
        scratch_shapes=[pltpu.VMEM((n_rows, pps * page_rows), F32),
                        pltpu.VMEM((n_rows, LANES), F32), pltpu.VMEM((n_rows, LANES), F32),
                        pltpu.VMEM((n_rows, DIFF_HW), F32)])
    return pl.pallas_call(
        functools.partial(_diff_decode_kernel, pps=pps, past_len=n_pages * PAGE_SIZE,
                          lambda_init=lambda_init),
        grid_spec=grid_spec, out_shape=jax.ShapeDtypeStruct((db, dq, D_MODEL), F32),
        compiler_params=_params("parallel", "arbitrary"), name="diff_decode",
    )(page_table, qx, slope_rows, qi_rows, head_rows, knew, vnew, *lams, subln_w,
      *([k_pool] * pps), *([v_pool] * pps))


def _router_logits(t, wr_hi_ref, wr_lo_ref, br_ref):
    t_hi, t_lo = _split_bf16(t)
    return (jnp.dot(t_hi, wr_hi_ref[...], preferred_element_type=F32)
            + jnp.dot(t_lo, wr_hi_ref[...], preferred_element_type=F32)
            + jnp.dot(t_hi, wr_lo_ref[...], preferred_element_type=F32)) + br_ref[...]


def _router_combine(logits):
    lane = lax.broadcasted_iota(jnp.int32, logits.shape, 1)
    width = logits.shape[1]
    first = lambda hit: jnp.min(jnp.where(hit, lane, width), axis=1, keepdims=True)
    is_g = lane < N_GROUPS
    gl = jnp.where(is_g, logits, -jnp.inf)
    g_max = jnp.max(gl, axis=1, keepdims=True)
    g_sel = first(gl == g_max)
    g_w = 1.0 / jnp.sum(jnp.where(is_g, jnp.exp(logits - g_max), 0.0), axis=1, keepdims=True)
    e_lo = N_GROUPS + g_sel * EXPERTS_PER_GROUP
    el = jnp.where((lane >= e_lo) & (lane < e_lo + EXPERTS_PER_GROUP), logits, -jnp.inf)
    v1 = jnp.max(el, axis=1, keepdims=True)
    i1 = first(el == v1)
    el2 = jnp.where(lane == i1, -jnp.inf, el)
    v2 = jnp.max(el2, axis=1, keepdims=True)
    i2 = first(el2 == v2)
    e2 = jnp.exp(v2 - v1)
    w1 = 1.0 / (1.0 + e2)
    w2 = e2 / (1.0 + e2)
    return jnp.where(lane == i1, g_w * w1, 0.0) + jnp.where(lane == i2, g_w * w2, 0.0)


def _wo_moe_kernel(x_ref, a_ref, wo_ref, nw_ref, wr_hi_ref, wr_lo_ref, br_ref, wgu_ref, wd_ref,
                   fw_ref, o_ref, hid_ref, *, final):
    y1 = x_ref[...] + jnp.dot(a_ref[...], wo_ref[...], preferred_element_type=F32)
    t = _rms(y1, nw_ref[...], NORM_EPS)
    combine = _router_combine(_router_logits(t, wr_hi_ref, wr_lo_ref, br_ref))
    tb = t.astype(BF16)
    for e in range(N_EXPERTS):
        gu = jnp.dot(tb, wgu_ref[e], preferred_element_type=F32)
        gate, up = gu[:, :D_EXPERT], gu[:, D_EXPERT:]
        c = combine[:, N_GROUPS + e:N_GROUPS + e + 1]
        hid = gate * (1.0 / (1.0 + jnp.exp(-gate))) * up * c
        hid_ref[:, e * D_EXPERT:(e + 1) * D_EXPERT] = hid.astype(BF16)
    y2 = y1 + jnp.dot(hid_ref[...], wd_ref[...], preferred_element_type=F32)
    if final:
        y2 = _rms(y2, fw_ref[...], NORM_EPS)
    o_ref[...] = y2


def _wo_moe(x, attn, wo_bf, nw, wr_hi, wr_lo, br, wgu_bf, wd_bf, fw, *, final):
    n = x.shape[0]
    tm = min(512, n)
    row = lambda t: (t, 0)
    return pl.pallas_call(
        functools.partial(_wo_moe_kernel, final=final),
        grid=(n // tm,),
        in_specs=[pl.BlockSpec((tm, D_MODEL), row), pl.BlockSpec((tm, D_MODEL), row),
                  _resident(wo_bf.shape), _resident((1, D_MODEL)),
                  _resident(wr_hi.shape), _resident(wr_lo.shape), _resident(br.shape),
                  _resident(wgu_bf.shape), _resident(wd_bf.shape), _resident((1, D_MODEL))],
        out_specs=pl.BlockSpec((tm, D_MODEL), row),
        out_shape=jax.ShapeDtypeStruct((n, D_MODEL), F32),
        scratch_shapes=[pltpu.VMEM((tm, N_EXPERTS * D_EXPERT), BF16)],
        compiler_params=_params("parallel"), name="wo_moe",
    )(x, attn, wo_bf, nw.reshape(1, D_MODEL), wr_hi, wr_lo, br, wgu_bf, wd_bf,
      fw.reshape(1, D_MODEL))


def _wo_moe_f32_kernel(x_ref, a_ref, wo_ref, nw_ref, wr_ref, br_ref, wg_ref, wu_ref, wd_ref,
                       fw_ref, o_ref, y1_ref, t_ref, comb_ref, acc_ref, *, final):
    e = pl.program_id(0)

    @pl.when(e == 0)
    def _():
        y1 = x_ref[...] + _dot_f32(a_ref[...], wo_ref[...])
        t = _rms(y1, nw_ref[...], NORM_EPS)
        y1_ref[...] = y1
        t_ref[...] = t
        comb_ref[...] = _router_combine(_dot_f32(t, wr_ref[...]) + br_ref[...])
        acc_ref[...] = jnp.zeros(acc_ref.shape, F32)

    t = t_ref[...]
    gate = _dot_f32(t, wg_ref[0])
    up = _dot_f32(t, wu_ref[0])
    lane = lax.broadcasted_iota(jnp.int32, comb_ref.shape, 1)
    c = jnp.sum(jnp.where(lane == N_GROUPS + e, comb_ref[...], 0.0), axis=1, keepdims=True)
    hid = gate * (1.0 / (1.0 + jnp.exp(-gate))) * up * c
    acc_ref[...] += _dot_f32(hid, wd_ref[0])

    @pl.when(e == pl.num_programs(0) - 1)
    def _():
        y2 = y1_ref[...] + acc_ref[...]
        if final:
            y2 = _rms(y2, fw_ref[...], NORM_EPS)
        o_ref[...] = y2


def _wo_moe_f32(x, attn, wo, nw, wr, br, w_gate, w_up, w_down, fw, *, final):
    n = x.shape[0]
    whole = lambda shape: pl.BlockSpec(shape, lambda e: (0,) * len(shape))
    per_expert = lambda shape: pl.BlockSpec((1,) + shape, lambda e: (e, 0, 0))
    return pl.pallas_call(
        functools.partial(_wo_moe_f32_kernel, final=final),
        grid=(N_EXPERTS,),
        in_specs=[whole((n, D_MODEL)), whole((n, D_MODEL)), whole(wo.shape), whole((1, D_MODEL)),
                  whole(wr.shape), whole(br.shape), per_expert((D_MODEL, D_EXPERT)),
                  per_expert((D_MODEL, D_EXPERT)), per_expert((D_EXPERT, D_MODEL)),
                  whole((1, D_MODEL))],
        out_specs=whole((n, D_MODEL)),
        out_shape=jax.ShapeDtypeStruct((n, D_MODEL), F32),
        scratch_shapes=[pltpu.VMEM((n, D_MODEL), F32), pltpu.VMEM((n, D_MODEL), F32),
                        pltpu.VMEM((n, ROUTER_LANES), F32), pltpu.VMEM((n, D_MODEL), F32)],
        compiler_params=_params("arbitrary"), name="wo_moe_f32",
    )(x, attn, wo, nw.reshape(1, D_MODEL), wr, br, w_gate, w_up, w_down, fw.reshape(1, D_MODEL))


def _moe_weights(w_group, b_group, w_expert, b_expert, w_gate, w_up, w_down):
    n_r = N_GROUPS + N_EXPERTS
    wr = jnp.pad(jnp.concatenate([w_group, w_expert], axis=1), ((0, 0), (0, ROUTER_LANES - n_r)))
    wr_hi = wr.astype(BF16)
    wr_lo = (wr - wr_hi.astype(F32)).astype(BF16)
    br = jnp.pad(jnp.concatenate([b_group, b_expert]), (0, ROUTER_LANES - n_r)).reshape(1, -1)
    wgu = jnp.concatenate([w_gate, w_up], axis=2).astype(BF16)
    wd = w_down.reshape(N_EXPERTS * D_EXPERT, D_MODEL).astype(BF16)
    return (wr_hi, wr_lo, br, wgu, wd), (wr, br)


def _block_diag_rows(q, n_slots):
    db, _, m, dq, w = q.shape
    eye = jnp.eye(n_slots, dtype=q.dtype)
    out = q[:, :, :, :, None, :] * eye[None, :, None, None, :, None]
    return out.reshape(db, n_slots * m * dq, n_slots * w)


def _pad_tokens(x):
    return jnp.pad(x, ((0, 0), (0, PAGE_SIZE - x.shape[1])) + ((0, 0),) * (x.ndim - 2))


def kernel(x_prompt, x_sample, cache_k_moba, cache_v_moba, cache_k_diff, cache_v_diff, page_table,
           norm_attn, norm_ffn, norm_final, moba_w_qkv, moba_w_o, diff_w_qkv, diff_w_o,
           diff_lambda_q1, diff_lambda_k1, diff_lambda_q2, diff_lambda_k2, diff_subln,
           moe_w_group, moe_b_group, moe_w_expert, moe_b_expert, moe_w_gate, moe_w_up, moe_w_down):
    batch, seq_len, _ = x_prompt.shape
    db, dq, _ = x_sample.shape
    n_pool = cache_k_moba.shape[1]
    depth = norm_attn.shape[0]
    nb = seq_len // MOBA_BLOCK
    slopes_moba = _alibi_slopes(MOBA_HEADS)
    slopes_diff = _alibi_slopes(DIFF_HEADS)
    qi_new = np.arange(dq, dtype=np.float32)

    moba_pool_t = lambda c: jnp.transpose(c, (0, 1, 3, 4, 2)).reshape(-1, MOBA_KVW, PAGE_SIZE)
    diff_pool = lambda c: c.reshape(-1, PAGE_SIZE * DIFF_HEADS, DIFF_HW)
    k_moba_t, v_moba_t = moba_pool_t(cache_k_moba), moba_pool_t(cache_v_moba)
    k_diff, v_diff = diff_pool(cache_k_diff), diff_pool(cache_v_diff)

    yp = x_prompt.reshape(batch * seq_len, D_MODEL)
    ys = x_sample.reshape(db * dq, D_MODEL)
    outs = {name: [] for name in ("km_p", "vm_p", "kd_p", "vd_p", "km_s", "vm_s", "kd_s", "vd_s")}
    for i in range(depth):
        j = i // 2
        final = i == depth - 1
        if i % 2 == 0:
            w_bf = moba_w_qkv[j].astype(BF16)
            wkt_bf = moba_w_qkv[j][:, D_MODEL:D_MODEL + MOBA_KVW].T.astype(BF16)
            qp, kp, vp, kt, vh, kmean = _norm_qkv(yp, norm_attn[i], w_bf, mode="moba",
                                                  seq_len=seq_len, wkt_bf=wkt_bf)
            km = kmean.reshape(batch, nb, MOBA_KV_HEADS, HEAD_DIM).transpose(0, 2, 3, 1)
            km_rep = jnp.pad(jnp.tile(km, (1, 1, MOBA_GROUP, 1)),
                             ((0, 0), (0, 0), (0, 0), (0, LANES - nb)))
            attn_p = _moba_prompt(qp, km_rep, kt, vh, jnp.asarray(slopes_moba))

            qs, ks, vs = _norm_qkv(ys, norm_attn[i], moba_w_qkv[j], mode="plain")
            kmt = _kmean_pages(k_moba_t, page_table, j * n_pool)
            q5 = qs.reshape(db, dq, MOBA_KV_HEADS, MOBA_GROUP, HEAD_DIM).transpose(0, 2, 3, 1, 4)
            qf = _block_diag_rows(q5, MOBA_KV_HEADS)
            slope_rows = np.repeat(slopes_moba, dq).reshape(-1, 1)
            qi_rows = np.tile(qi_new, MOBA_HEADS).reshape(-1, 1)
            new_t = lambda x: _pad_tokens(x.reshape(db, dq, MOBA_KVW)).transpose(0, 2, 1)
            o_rows = _moba_decode(qf * SCALE, qf, kmt, jnp.asarray(slope_rows), jnp.asarray(qi_rows),
                                  new_t(ks), new_t(vs), k_moba_t, v_moba_t, page_table, j * n_pool)
            o6 = o_rows.reshape(db, MOBA_KV_HEADS, MOBA_GROUP, dq, MOBA_KV_HEADS, HEAD_DIM)
            o5 = jnp.stack([o6[:, g, :, :, g, :] for g in range(MOBA_KV_HEADS)], axis=1)
            attn_s = o5.transpose(0, 3, 1, 2, 4).reshape(db * dq, D_MODEL)
            w_o = moba_w_o[j]
            kvh, kvd = MOBA_KV_HEADS, HEAD_DIM
            names = ("km_p", "vm_p", "km_s", "vm_s")
        else:
            lambda_init = 0.8 - 0.6 * math.exp(-0.3 * i)
            w_bf = diff_w_qkv[j].astype(BF16)
            lams = [a[j].reshape(1, HEAD_DIM) for a in
                    (diff_lambda_q1, diff_lambda_k1, diff_lambda_q2, diff_lambda_k2)]
            subln_w = diff_subln[j].reshape(1, DIFF_HW)
            qp, kp, vp, kb, vb = _norm_qkv(yp, norm_attn[i], w_bf, mode="diff")
            attn_p = _diff_prompt(qp, kb, vb, lams, subln_w, jnp.asarray(slopes_diff),
                                  batch=batch, lambda_init=lambda_init)

            qs, ks, vs = _norm_qkv(ys, norm_attn[i], diff_w_qkv[j], mode="plain")
            q5 = qs.reshape(db, dq, DIFF_HEADS, 2, 1, HEAD_DIM).transpose(0, 2, 3, 4, 1, 5)
            qx = _block_diag_rows(q5.reshape(db * DIFF_HEADS, 2, 1, dq, HEAD_DIM), 2)
            qx = (qx.reshape(db, 2 * DIFF_HEADS * dq, DIFF_HW) * (SCALE * LOG2E)).astype(BF16)
            slope_rows = np.repeat(slopes_diff, 2 * dq).reshape(-1, 1)
            qi_rows = np.tile(qi_new, 2 * DIFF_HEADS).reshape(-1, 1)
            head_rows = np.repeat(np.arange(DIFF_HEADS, dtype=np.int32), 2 * dq).reshape(-1, 1)
            new_flat = lambda x: _pad_tokens(x.reshape(db, dq, DIFF_HEADS, DIFF_HW)).reshape(
                db, PAGE_SIZE * DIFF_HEADS, DIFF_HW)
            attn_s = _diff_decode(qx, jnp.asarray(slope_rows), jnp.asarray(qi_rows),
                                  jnp.asarray(head_rows), new_flat(ks), new_flat(vs), lams, subln_w,
                                  k_diff, v_diff, page_table, j * n_pool,
                                  lambda_init=lambda_init).reshape(db * dq, D_MODEL)
            w_o = diff_w_o[j]
            kvh, kvd = DIFF_HEADS, DIFF_HW
            names = ("kd_p", "vd_p", "kd_s", "vd_s")
        outs[names[0]].append(kp.reshape(batch, seq_len, kvh, kvd))
        outs[names[1]].append(vp.reshape(batch, seq_len, kvh, kvd))
        outs[names[2]].append(ks.reshape(db, dq, kvh, kvd))
        outs[names[3]].append(vs.reshape(db, dq, kvh, kvd))

        moe_w, (wr, br) = _moe_weights(moe_w_group[i], moe_b_group[i], moe_w_expert[i],
                                       moe_b_expert[i], moe_w_gate[i], moe_w_up[i], moe_w_down[i])
        yp = _wo_moe(yp, attn_p, w_o.astype(BF16), norm_ffn[i], *moe_w, norm_final, final=final)
        ys = _wo_moe_f32(ys, attn_s, w_o, norm_ffn[i], wr, br, moe_w_gate[i], moe_w_up[i],
                         moe_w_down[i], norm_final, final=final)

    stack = lambda name: jnp.stack(outs[name])
    return (yp.reshape(batch, seq_len, D_MODEL), ys.reshape(db, dq, D_MODEL),
            stack("km_p"), stack("vm_p"), stack("kd_p"), stack("vd_p"),
            stack("km_s"), stack("vm_s"), stack("kd_s"), stack("vd_s"))
```

```python
import functools
import math

import jax
import jax.numpy as jnp
import numpy as np
from jax import lax
from jax.experimental import pallas as pl
from jax.experimental.pallas import tpu as pltpu

D_MODEL = 1024
HEAD_DIM = 64
HEAD_DIM_LOG2 = 6
MOBA_HEADS = D_MODEL // HEAD_DIM
MOBA_KV_HEADS = 4
MOBA_GROUP = MOBA_HEADS // MOBA_KV_HEADS
MOBA_BLOCK = 256
MOBA_BLOCK_LOG2 = 8
MOBA_TOPK = 3
MOBA_KVW = MOBA_KV_HEADS * HEAD_DIM
DIFF_HEADS = D_MODEL // (2 * HEAD_DIM)
DIFF_HEADS_LOG2 = 3
DIFF_HW = 2 * HEAD_DIM
N_GROUPS = 4
EXPERTS_PER_GROUP = 4
N_EXPERTS = N_GROUPS * EXPERTS_PER_GROUP
D_EXPERT = 256
PAGE_SIZE = 128
NORM_EPS = 1e-6
SUBLN_EPS = 1e-5
NEG_INF = -1e30
SCALE = HEAD_DIM ** -0.5
LOG2E = math.log2(math.e)

PROMPT_TILE_BLOCKS = 4
LANES = 128
ROUTER_LANES = LANES
VMEM_LIMIT = 56 * 1024 * 1024

F32 = jnp.float32
BF16 = jnp.bfloat16
NT_DIMS = (((1,), (1,)), ((), ()))


def _alibi_slopes(n_heads):
    start = 2.0 ** (-8.0 / n_heads)
    return np.array([start ** (i + 1) for i in range(n_heads)], dtype=np.float32)


def _params(*sem):
    return pltpu.CompilerParams(dimension_semantics=sem, vmem_limit_bytes=VMEM_LIMIT)


def _resident(shape):
    nd = len(shape)
    return pl.BlockSpec(shape, lambda *_: (0,) * nd, pipeline_mode=pl.Buffered(1))


def _rms(x, w, eps):
    ms = jnp.mean(x * x, axis=-1, keepdims=True)
    return x * lax.rsqrt(ms + eps) * w


def _split_bf16(x):
    hi = x.astype(BF16)
    return hi, (x - hi.astype(F32)).astype(BF16)


def _nt_dot(a, b):
    return lax.dot_general(a, b, NT_DIMS, preferred_element_type=F32)


def _dot_f32(a, b):
    return jnp.dot(a, b, precision=lax.Precision.HIGHEST, preferred_element_type=F32)


def _norm_qkv_kernel(x_ref, nw_ref, w_ref, *refs, mode, kvw, tm):
    h = _rms(x_ref[...], nw_ref[...], NORM_EPS)
    if w_ref.dtype == F32:
        qkv = _dot_f32(h, w_ref[...])
    else:
        h = h.astype(BF16)
        qkv = jnp.dot(h, w_ref[...], preferred_element_type=F32)
    q = qkv[:, :D_MODEL]
    k = qkv[:, D_MODEL:D_MODEL + kvw]
    v = qkv[:, D_MODEL + kvw:]
    if mode == "plain":
        q_ref, k_ref, v_ref = refs
    elif mode == "moba":
        wkt_ref, q_ref, k_ref, v_ref, kt_ref, vh_ref, km_ref = refs
        kt = _nt_dot(wkt_ref[...], h)
        vb = v.astype(BF16)
        for jj in range(tm // MOBA_BLOCK):
            sl = slice(jj * MOBA_BLOCK, (jj + 1) * MOBA_BLOCK)
            kt_ref[0, jj] = kt[:, sl].astype(BF16)
            km_ref[jj] = jnp.mean(k[sl], axis=0, keepdims=True)
        for g in range(MOBA_KV_HEADS):
            vh_ref[0, g] = vb[:, g * HEAD_DIM:(g + 1) * HEAD_DIM]
    else:
        q_ref, k_ref, v_ref, kb_ref, vb_ref = refs
        kb_ref[...] = k.astype(BF16)
        vb_ref[...] = v.astype(BF16)
    q_ref[...] = q
    k_ref[...] = k
    v_ref[...] = v


def _norm_qkv(x, nw, w, *, mode, seq_len=None, wkt_bf=None):
    n = x.shape[0]
    e = w.shape[1]
    kvw = (e - D_MODEL) // 2
    tm = min(512, n)
    nt = n // tm
    row = lambda t: (t, 0)
    in_specs = [pl.BlockSpec((tm, D_MODEL), row), _resident((1, D_MODEL)), _resident((D_MODEL, e))]
    args = [x, nw.reshape(1, D_MODEL), w]
    out_shape = [jax.ShapeDtypeStruct((n, D_MODEL), F32),
                 jax.ShapeDtypeStruct((n, kvw), F32),
                 jax.ShapeDtypeStruct((n, kvw), F32)]
    out_specs = [pl.BlockSpec((tm, D_MODEL), row), pl.BlockSpec((tm, kvw), row),
                 pl.BlockSpec((tm, kvw), row)]
    if mode == "moba":
        b = n // seq_len
        tps = seq_len // tm
        bpt = tm // MOBA_BLOCK
        nb = seq_len // MOBA_BLOCK
        in_specs.append(_resident((kvw, D_MODEL)))
        args.append(wkt_bf)
        out_shape += [jax.ShapeDtypeStruct((b, nb, kvw, MOBA_BLOCK), BF16),
                      jax.ShapeDtypeStruct((b, MOBA_KV_HEADS, seq_len, HEAD_DIM), BF16),
                      jax.ShapeDtypeStruct((b * nb, 1, kvw), F32)]
        out_specs += [pl.BlockSpec((1, bpt, kvw, MOBA_BLOCK), lambda t: (t // tps, t % tps, 0, 0)),
                      pl.BlockSpec((1, MOBA_KV_HEADS, tm, HEAD_DIM), lambda t: (t // tps, 0, t % tps, 0)),
                      pl.BlockSpec((bpt, 1, kvw), lambda t: (t, 0, 0))]
    elif mode == "diff":
        out_shape += [jax.ShapeDtypeStruct((n, kvw), BF16), jax.ShapeDtypeStruct((n, kvw), BF16)]
        out_specs += [pl.BlockSpec((tm, kvw), row), pl.BlockSpec((tm, kvw), row)]
    return pl.pallas_call(
        functools.partial(_norm_qkv_kernel, mode=mode, kvw=kvw, tm=tm),
        grid=(nt,), in_specs=in_specs, out_specs=out_specs, out_shape=out_shape,
        compiler_params=_params("parallel"), name="norm_qkv_" + mode,
    )(*args)


def _softmax_update(s, m_ref, l_ref, rows, shift=0.0, base2=False):
    width = s.shape[1]
    exp = jnp.exp2 if base2 else jnp.exp
    m_prev = m_ref[rows, :]
    m_new = jnp.maximum(m_prev, jnp.max(s, axis=1, keepdims=True) + shift)
    alpha = exp(m_prev - m_new)
    p = exp(s - jnp.concatenate([m_new - shift] * (width // LANES), axis=1))
    l_ref[rows, :] = alpha * l_ref[rows, :] + jnp.sum(p, axis=1, keepdims=True)
    m_ref[rows, :] = m_new
    return p, alpha


def _init_softmax_state(m_ref, l_ref, acc_ref):
    m_ref[...] = jnp.full(m_ref.shape, -jnp.inf, F32)
    l_ref[...] = jnp.zeros(l_ref.shape, F32)
    acc_ref[...] = jnp.zeros(acc_ref.shape, F32)


def _widen(x, width):
    return jnp.concatenate([x] * (width // LANES), axis=1)


def _for_each_key_tile(i, attend):
    assert PROMPT_TILE_BLOCKS == 4

    def full_tile(jj, carry):
        attend(jj * PROMPT_TILE_BLOCKS, PROMPT_TILE_BLOCKS, False)
        return carry

    lax.fori_loop(0, i // PROMPT_TILE_BLOCKS, full_tile, 0)
    left = i % PROMPT_TILE_BLOCKS

    @pl.when(left >= 2)
    def _():
        attend(i - left, 2, False)

    @pl.when(left % 2 == 0)
    def _():
        attend(i, 1, True)

    @pl.when(left % 2 == 1)
    def _():
        attend(i - 1, 2, True)


def _top3_mask(gate, lane, n_valid):
    gt = jnp.where(lane < n_valid, gate, -jnp.inf)
    sel = jnp.zeros(gate.shape, F32)
    for _ in range(MOBA_TOPK):
        mx = jnp.max(gt, axis=1, keepdims=True)
        idx = jnp.min(jnp.where(gt == mx, lane, gate.shape[1]), axis=1, keepdims=True)
        hit = (lane == idx) & (mx > -jnp.inf)
        sel = jnp.where(hit, 1.0, sel)
        gt = jnp.where(hit, -jnp.inf, gt)
    return sel


def _moba_prompt_kernel(slopes_ref, q_ref, km_ref, kt_ref, v_ref, o_ref,
                        qz_ref, sel_ref, bias_ref, m_ref, l_ref, acc_ref, p_ref):
    tq = MOBA_BLOCK
    g = pl.program_id(1)
    i = pl.program_id(2)
    q = q_ref[...]
    gate = _dot_f32(q, km_ref[0, 0])
    lane = lax.broadcasted_iota(jnp.int32, (tq, LANES), 1)
    sel_ref[...] = _top3_mask(gate, lane, i)
    qs = q * (SCALE * LOG2E)
    qlane = lax.shift_right_logical(lax.broadcasted_iota(jnp.int32, qs.shape, 1), HEAD_DIM_LOG2)
    for r in range(MOBA_GROUP):
        qz_ref[r * tq:(r + 1) * tq, :] = jnp.where(qlane == r, qs, 0.0).astype(BF16)
    _init_softmax_state(m_ref, l_ref, acc_ref)
    slope2 = [slopes_ref[g * MOBA_GROUP + r] * LOG2E for r in range(MOBA_GROUP)]

    @pl.when(i == 0)
    def _():
        width = PROMPT_TILE_BLOCKS * tq
        rel = (lax.broadcasted_iota(jnp.int32, (tq, width), 1)
               - lax.broadcasted_iota(jnp.int32, (tq, width), 0)).astype(F32)
        for r in range(MOBA_GROUP):
            bias_ref[r * tq:(r + 1) * tq, :] = slope2[r] * rel

    def attend(first, n, own_last):
        width = n * tq
        kt = jnp.concatenate([kt_ref[0, first + u] for u in range(n)], axis=1)
        kt4 = jnp.concatenate([kt] * MOBA_GROUP, axis=0)
        s = jnp.dot(qz_ref[...], kt4, preferred_element_type=F32)
        oks = []
        for u in range(n):
            if own_last and u == n - 1:
                oks.append(lax.broadcasted_iota(jnp.int32, (tq, tq), 1)
                           <= lax.broadcasted_iota(jnp.int32, (tq, tq), 0))
            else:
                picked = jnp.sum(jnp.where(lane == first + u, sel_ref[...], 0.0), axis=1,
                                 keepdims=True)
                oks.append(jnp.broadcast_to(picked, (tq, tq)) > 0.0)
        ok = jnp.concatenate(oks, axis=1)
        off = ((first - i) * tq).astype(F32)
        for r in range(MOBA_GROUP):
            rows = slice(r * tq, (r + 1) * tq)
            sr = jnp.where(ok, s[rows] + bias_ref[rows, :width], NEG_INF)
            p, alpha = _softmax_update(sr, m_ref, l_ref, rows, shift=slope2[r] * off, base2=True)
            acc_ref[rows, :] = acc_ref[rows, :] * alpha[:, :HEAD_DIM]
            p_ref[rows, :width] = p.astype(BF16)
        vj = v_ref[0, 0, pl.ds(pl.multiple_of(first * tq, tq), width), :]
        acc_ref[...] += jnp.dot(p_ref[:, :width], vj, preferred_element_type=F32)

    _for_each_key_tile(i, attend)
    out = [acc_ref[r * tq:(r + 1) * tq, :] / l_ref[r * tq:(r + 1) * tq, :HEAD_DIM]
           for r in range(MOBA_GROUP)]
    o_ref[...] = jnp.concatenate(out, axis=1).astype(BF16)


def _moba_prompt(q, km_rep, kt, vh, slopes):
    b, nb = kt.shape[0], kt.shape[1]
    seq_len = nb * MOBA_BLOCK
    tq = MOBA_BLOCK
    gw = MOBA_GROUP * HEAD_DIM
    return pl.pallas_call(
        _moba_prompt_kernel,
        grid=(b, MOBA_KV_HEADS, nb),
        in_specs=[pl.BlockSpec(memory_space=pltpu.SMEM),
                  pl.BlockSpec((tq, gw), lambda bi, g, i: (bi * nb + i, g)),
                  pl.BlockSpec((1, 1, gw, LANES), lambda bi, g, i: (bi, g, 0, 0)),
                  pl.BlockSpec((1, nb, HEAD_DIM, MOBA_BLOCK), lambda bi, g, i: (bi, 0, g, 0)),
                  pl.BlockSpec((1, 1, seq_len, HEAD_DIM), lambda bi, g, i: (bi, g, 0, 0))],
        out_specs=pl.BlockSpec((tq, gw), lambda bi, g, i: (bi * nb + i, g)),
        out_shape=jax.ShapeDtypeStruct((b * seq_len, D_MODEL), BF16),
        scratch_shapes=[pltpu.VMEM((MOBA_GROUP * tq, gw), BF16),
                        pltpu.VMEM((tq, LANES), F32),
                        pltpu.VMEM((MOBA_GROUP * tq, PROMPT_TILE_BLOCKS * tq), F32),
                        pltpu.VMEM((MOBA_GROUP * tq, LANES), F32),
                        pltpu.VMEM((MOBA_GROUP * tq, LANES), F32),
                        pltpu.VMEM((MOBA_GROUP * tq, HEAD_DIM), F32),
                        pltpu.VMEM((MOBA_GROUP * tq, PROMPT_TILE_BLOCKS * tq), BF16)],
        compiler_params=_params("parallel", "parallel", "arbitrary"), name="moba_prompt",
    )(slopes, q, km_rep, kt, vh)


def _lambda_full(lq1_ref, lk1_ref, lq2_ref, lk2_ref, lambda_init):
    a = jnp.sum(lq1_ref[...] * lk1_ref[...], axis=1, keepdims=True)
    b = jnp.sum(lq2_ref[...] * lk2_ref[...], axis=1, keepdims=True)
    return jnp.exp(a) - jnp.exp(b) + lambda_init


def _subln(o, w, lambda_init):
    return _rms(o, w, SUBLN_EPS) * (1.0 - lambda_init)


def _diff_prompt_kernel(slopes_ref, q_ref, k_ref, v_ref, lq1_ref, lk1_ref, lq2_ref, lk2_ref,
                        sw_ref, o_ref, qz_ref, bias_ref, m_ref, l_ref, acc_ref, *, tq, lambda_init):
    hd = pl.program_id(1)
    i = pl.program_id(2)
    qs = q_ref[...] * (SCALE * LOG2E)
    qlane = lax.shift_right_logical(lax.broadcasted_iota(jnp.int32, qs.shape, 1), HEAD_DIM_LOG2)
    for mp in range(2):
        qz_ref[mp * tq:(mp + 1) * tq, :] = jnp.where(qlane == mp, qs, 0.0).astype(BF16)
    _init_softmax_state(m_ref, l_ref, acc_ref)
    slope = slopes_ref[hd] * LOG2E
    everything = slice(None)

    def in_tile_offsets(width):
        rows2 = lax.broadcasted_iota(jnp.int32, (2 * tq, width), 0)
        return (lax.broadcasted_iota(jnp.int32, (2 * tq, width), 1)
                - jnp.where(rows2 >= tq, rows2 - tq, rows2))

    @pl.when(i == 0)
    def _():
        bias_ref[...] = slope * in_tile_offsets(PROMPT_TILE_BLOCKS * tq).astype(F32)

    def attend(first, n, own_last):
        width = n * tq
        ks = pl.ds(pl.multiple_of(first * tq, tq), width)
        s = _nt_dot(qz_ref[...], k_ref[ks, :]) + bias_ref[:, :width]
        if own_last:
            s = jnp.where(in_tile_offsets(width) <= (n - 1) * tq, s, NEG_INF)
        p, alpha = _softmax_update(s, m_ref, l_ref, everything,
                                   shift=slope * ((first - i) * tq).astype(F32), base2=True)
        acc_ref[...] = acc_ref[...] * alpha + jnp.dot(p.astype(BF16), v_ref[ks, :],
                                                      preferred_element_type=F32)

    _for_each_key_tile(i, attend)
    o1 = acc_ref[:tq, :] / l_ref[:tq, :]
    o2 = acc_ref[tq:, :] / l_ref[tq:, :]
    lam = _lambda_full(lq1_ref, lk1_ref, lq2_ref, lk2_ref, lambda_init)
    o_ref[...] = _subln(o1 - lam * o2, sw_ref[...], lambda_init).astype(BF16)


def _diff_prompt(q, kb, vb, lams, subln_w, slopes, *, batch, lambda_init):
    n = q.shape[0]
    seq_len = n // batch
    tq = 256
    nq = seq_len // tq
    lam_specs = [_resident((1, HEAD_DIM))] * 4
    return pl.pallas_call(
        functools.partial(_diff_prompt_kernel, tq=tq, lambda_init=lambda_init),
        grid=(batch, DIFF_HEADS, nq),
        in_specs=[pl.BlockSpec(memory_space=pltpu.SMEM),
                  pl.BlockSpec((tq, DIFF_HW), lambda bi, h, i: (bi * nq + i, h)),
                  pl.BlockSpec((seq_len, DIFF_HW), lambda bi, h, i: (bi, h)),
                  pl.BlockSpec((seq_len, DIFF_HW), lambda bi, h, i: (bi, h)),
                  *lam_specs, _resident((1, DIFF_HW))],
        out_specs=pl.BlockSpec((tq, DIFF_HW), lambda bi, h, i: (bi * nq + i, h)),
        out_shape=jax.ShapeDtypeStruct((n, D_MODEL), BF16),
        scratch_shapes=[pltpu.VMEM((2 * tq, DIFF_HW), BF16),
                        pltpu.VMEM((2 * tq, PROMPT_TILE_BLOCKS * tq), F32),
                        pltpu.VMEM((2 * tq, LANES), F32),
                        pltpu.VMEM((2 * tq, LANES), F32),
                        pltpu.VMEM((2 * tq, DIFF_HW), F32)],
        compiler_params=_params("parallel", "parallel", "arbitrary"), name="diff_prompt",
    )(slopes, q, kb, vb, *lams, subln_w)


def _page_specs(block_shape, pages_per_step, page_offset):
    def spec(t):
        return pl.BlockSpec((1,) + block_shape,
                            lambda b, s, pt: (pt[b, s * pages_per_step + t] + page_offset, 0, 0))
    return [spec(t) for t in range(pages_per_step)]


def _kmean_pages_kernel(pt_ref, *refs):
    k_refs, o_ref = refs[:-1], refs[-1]
    step = pl.program_id(1)
    ppb = MOBA_BLOCK // PAGE_SIZE
    blocks_per_step = len(k_refs) // ppb
    lane = lax.broadcasted_iota(jnp.int32, o_ref.shape[1:], 1)
    upd = jnp.zeros(o_ref.shape[1:], F32)
    for t in range(blocks_per_step):
        tot = sum(k_refs[t * ppb + u][0] for u in range(ppb))
        col = jnp.sum(tot, axis=1, keepdims=True) / MOBA_BLOCK
        upd = jnp.where(lane == step * blocks_per_step + t, col, upd)

    @pl.when(step == 0)
    def _():
        o_ref[0] = upd

    @pl.when(step > 0)
    def _():
        o_ref[0] += upd


def _kmean_pages(pool_t, page_table, page_offset):
    db, n_pages = page_table.shape
    pps = 16
    width = pool_t.shape[1]
    grid_spec = pltpu.PrefetchScalarGridSpec(
        num_scalar_prefetch=1, grid=(db, n_pages // pps),
        in_specs=_page_specs((width, PAGE_SIZE), pps, page_offset),
        out_specs=pl.BlockSpec((1, width, LANES), lambda b, s, pt: (b, 0, 0)))
    return pl.pallas_call(
        _kmean_pages_kernel, grid_spec=grid_spec,
        out_shape=jax.ShapeDtypeStruct((db, width, LANES), F32),
        compiler_params=_params("parallel", "arbitrary"), name="kmean_pages",
    )(page_table, *([pool_t] * pps))


def _moba_decode_kernel(pt_ref, qs_ref, qf_ref, kmt_ref, slope_ref, qi_ref, knew_ref, vnew_ref,
                        *refs, pps, past_len):
    k_refs, v_refs = refs[:pps], refs[pps:2 * pps]
    o_ref, m_ref, l_ref, acc_ref, sel_ref = refs[2 * pps:]
    step = pl.program_id(1)
    n_rows = qs_ref.shape[1]
    chunk = pps * PAGE_SIZE
    everything = slice(None)

    @pl.when(step == 0)
    def _():
        _init_softmax_state(m_ref, l_ref, acc_ref)
        gr = _dot_f32(qf_ref[0], kmt_ref[0])
        n_blocks = past_len // MOBA_BLOCK
        rpg = n_rows // MOBA_KV_HEADS
        dq = rpg // MOBA_GROUP
        lane = lax.broadcasted_iota(jnp.int32, (dq, LANES), 1)
        for g in range(MOBA_KV_HEADS):
            gs = sum(gr[g * rpg + r * dq:g * rpg + (r + 1) * dq] for r in range(MOBA_GROUP))
            sel = _top3_mask(gs, lane, n_blocks)
            sel_ref[g * rpg:(g + 1) * rpg, :] = jnp.concatenate([sel] * MOBA_GROUP, axis=0)

    qh, ql = _split_bf16(qs_ref[0])
    q2 = jnp.concatenate([qh, ql], axis=0)

    def qk(kt32):
        kh, kl = _split_bf16(kt32)
        s2 = jnp.dot(q2, kh, preferred_element_type=F32)
        return s2[:n_rows] + s2[n_rows:] + jnp.dot(qh, kl, preferred_element_type=F32)

    def pv(p, vt32):
        ph, plo = _split_bf16(p)
        vh, vl = _split_bf16(vt32)
        o2 = _nt_dot(jnp.concatenate([ph, plo], axis=0), vh)
        return o2[:n_rows] + o2[n_rows:] + _nt_dot(ph, vl)

    def update(kt32, vt32, tok, ok, shift):
        s = qk(kt32) + slope_ref[...] * (tok - qi_ref[...])
        s = jnp.where(ok, s, NEG_INF)
        p, alpha = _softmax_update(s, m_ref, l_ref, everything, shift=shift)
        acc_ref[...] = acc_ref[...] * _widen(alpha, acc_ref.shape[1]) + pv(p, vt32)

    kt = jnp.concatenate([r[0] for r in k_refs], axis=1)
    vt = jnp.concatenate([r[0] for r in v_refs], axis=1)
    tok = lax.broadcasted_iota(jnp.int32, (1, chunk), 1).astype(F32)
    blk = lax.broadcasted_iota(jnp.int32, (LANES, chunk), 0)
    key = step * chunk + lax.broadcasted_iota(jnp.int32, (LANES, chunk), 1)
    expand = jnp.where(blk == lax.shift_right_logical(key, MOBA_BLOCK_LOG2), 1.0, 0.0).astype(BF16)
    ok = jnp.dot(sel_ref[...].astype(BF16), expand, preferred_element_type=F32) > 0.5
    update(kt, vt, tok, ok, slope_ref[...] * (step * chunk - past_len).astype(F32))

    @pl.when(step == pl.num_programs(1) - 1)
    def _():
        tok_new = lax.broadcasted_iota(jnp.int32, (1, PAGE_SIZE), 1).astype(F32)
        update(knew_ref[0], vnew_ref[0], tok_new, tok_new <= qi_ref[...], 0.0)
        o_ref[0] = acc_ref[...] / _widen(l_ref[...], acc_ref.shape[1])


def _moba_decode(q_scaled, q_plain, kmt, slope_rows, qi_rows, knew_t, vnew_t, k_pool_t, v_pool_t,
                 page_table, page_offset):
    db, n_pages = page_table.shape
    n_rows, width = q_scaled.shape[1], q_scaled.shape[2]
    pps = 16
    per_seq = lambda shape: pl.BlockSpec((1,) + shape, lambda b, s, pt: (b, 0, 0))
    const = lambda shape: pl.BlockSpec(shape, lambda b, s, pt: (0, 0))
    grid_spec = pltpu.PrefetchScalarGridSpec(
        num_scalar_prefetch=1, grid=(db, n_pages // pps),
        in_specs=[per_seq((n_rows, width)), per_seq((n_rows, width)), per_seq((width, LANES)),
                  const((n_rows, 1)), const((n_rows, 1)),
                  per_seq((width, PAGE_SIZE)), per_seq((width, PAGE_SIZE)),
                  *(_page_specs((width, PAGE_SIZE), pps, page_offset) * 2)],
        out_specs=per_seq((n_rows, width)),
        scratch_shapes=[pltpu.VMEM((n_rows, LANES), F32), pltpu.VMEM((n_rows, LANES), F32),
                        pltpu.VMEM((n_rows, width), F32), pltpu.VMEM((n_rows, LANES), F32)])
    return pl.pallas_call(
        functools.partial(_moba_decode_kernel, pps=pps, past_len=n_pages * PAGE_SIZE),
        grid_spec=grid_spec, out_shape=jax.ShapeDtypeStruct((db, n_rows, width), F32),
        compiler_params=_params("parallel", "arbitrary"), name="moba_decode",
    )(page_table, q_scaled, q_plain, kmt, slope_rows, qi_rows, knew_t, vnew_t,
      *([k_pool_t] * pps), *([v_pool_t] * pps))


def _diff_decode_kernel(pt_ref, qx_ref, slope_ref, qi_ref, head_ref, knew_ref, vnew_ref,
                        lq1_ref, lk1_ref, lq2_ref, lk2_ref, sw_ref, *refs, pps, past_len, lambda_init):
    k_refs, v_refs = refs[:pps], refs[pps:2 * pps]
    o_ref, bias_ref, m_ref, l_ref, acc_ref = refs[2 * pps:]
    step = pl.program_id(1)
    n_rows = qx_ref.shape[1]
    chunk = pps * PAGE_SIZE
    everything = slice(None)
    slope2 = slope_ref[...] * LOG2E

    def alibi_bias(n_cols):
        col = lax.broadcasted_iota(jnp.int32, (n_rows, n_cols), 1)
        tok = lax.shift_right_logical(col, DIFF_HEADS_LOG2)
        own_head = (col & (DIFF_HEADS - 1)) == head_ref[...]
        tokf = tok.astype(F32)
        return jnp.where(own_head, slope2 * (tokf - qi_ref[...]), NEG_INF), tokf

    @pl.when(step == 0)
    def _():
        _init_softmax_state(m_ref, l_ref, acc_ref)
        bias_ref[...] = alibi_bias(chunk * DIFF_HEADS)[0]

    def update(k32, v32, bias, shift):
        s = _nt_dot(qx_ref[0], k32.astype(BF16)) + bias
        p, alpha = _softmax_update(s, m_ref, l_ref, everything, shift=shift, base2=True)
        acc_ref[...] = acc_ref[...] * alpha + jnp.dot(p.astype(BF16), v32.astype(BF16),
                                                      preferred_element_type=F32)

    kc = jnp.concatenate([r[0] for r in k_refs], axis=0)
    vc = jnp.concatenate([r[0] for r in v_refs], axis=0)
    update(kc, vc, bias_ref[...], slope2 * (step * chunk - past_len).astype(F32))

    @pl.when(step == pl.num_programs(1) - 1)
    def _():
        bias_new, tokf = alibi_bias(PAGE_SIZE * DIFF_HEADS)
        update(knew_ref[0], vnew_ref[0], jnp.where(tokf <= qi_ref[...], bias_new, NEG_INF), 0.0)
        o = acc_ref[...] / l_ref[...]
        dq = n_rows // (2 * DIFF_HEADS)
        lam = _lambda_full(lq1_ref, lk1_ref, lq2_ref, lk2_ref, lambda_init)
        outs = []
        for hd in range(DIFF_HEADS):
            o1 = o[(2 * hd) * dq:(2 * hd + 1) * dq]
            o2 = o[(2 * hd + 1) * dq:(2 * hd + 2) * dq]
            outs.append(_subln(o1 - lam * o2, sw_ref[...], lambda_init))
        o_ref[0] = jnp.concatenate(outs, axis=1)


def _diff_decode(qx, slope_rows, qi_rows, head_rows, knew, vnew, lams, subln_w, k_pool, v_pool,
                 page_table, page_offset, *, lambda_init):
    db, n_pages = page_table.shape
    n_rows = qx.shape[1]
    pps = 8
    page_rows = PAGE_SIZE * DIFF_HEADS
    dq = n_rows // (2 * DIFF_HEADS)
    per_seq = lambda shape: pl.BlockSpec((1,) + shape, lambda b, s, pt: (b, 0, 0))
    const = lambda shape: pl.BlockSpec(shape, lambda b, s, pt: (0, 0))
    grid_spec = pltpu.PrefetchScalarGridSpec(
        num_scalar_prefetch=1, grid=(db, n_pages // pps),
        in_specs=[per_seq((n_rows, DIFF_HW)), const((n_rows, 1)), const((n_rows, 1)),
                  const((n_rows, 1)), per_seq((page_rows, DIFF_HW)), per_seq((page_rows, DIFF_HW)),
                  *([const((1, HEAD_DIM))] * 4), const((1, DIFF_HW)),
                  *(_page_specs((page_rows, DIFF_HW), pps, page_offset) * 2)],
        out_specs=per_seq((dq, D_MODEL)),
        scratch_shapes=[pltpu.VMEM((n_rows, pps * page_rows), F32),
                        pltpu.VMEM((n_rows, LANES), F32), pltpu.VMEM((n_rows, LANES), F32),
                        pltpu.VMEM((n_rows, DIFF_HW), F32)])
    return pl.pallas_call(
        functools.partial(_diff_decode_kernel, pps=pps, past_len=n_pages * PAGE_SIZE,
                          lambda_init=lambda_init),
        grid_spec=grid_spec, out_shape=jax.ShapeDtypeStruct((db, dq, D_MODEL), F32),
        compiler_params=_params("parallel", "arbitrary"), name="diff_decode",
    )(page_table, qx, slope_rows, qi_rows, head_rows, knew, vnew, *lams, subln_w,
      *([k_pool] * pps), *([v_pool] * pps))


def _router_logits(t, wr_hi_ref, wr_lo_ref, br_ref):
    t_hi, t_lo = _split_bf16(t)
    return (jnp.dot(t_hi, wr_hi_ref[...], preferred_element_type=F32)
            + jnp.dot(t_lo, wr_hi_ref[...], preferred_element_type=F32)
            + jnp.dot(t_hi, wr_lo_ref[...], preferred_element_type=F32)) + br_ref[...]


def _router_combine(logits):
    lane = lax.broadcasted_iota(jnp.int32, logits.shape, 1)
    width = logits.shape[1]
    first = lambda hit: jnp.min(jnp.where(hit, lane, width), axis=1, keepdims=True)
    is_g = lane < N_GROUPS
    gl = jnp.where(is_g, logits, -jnp.inf)
    g_max = jnp.max(gl, axis=1, keepdims=True)
    g_sel = first(gl == g_max)
    g_w = 1.0 / jnp.sum(jnp.where(is_g, jnp.exp(logits - g_max), 0.0), axis=1, keepdims=True)
    e_lo = N_GROUPS + g_sel * EXPERTS_PER_GROUP
    el = jnp.where((lane >= e_lo) & (lane < e_lo + EXPERTS_PER_GROUP), logits, -jnp.inf)
    v1 = jnp.max(el, axis=1, keepdims=True)
    i1 = first(el == v1)
    el2 = jnp.where(lane == i1, -jnp.inf, el)
    v2 = jnp.max(el2, axis=1, keepdims=True)
    i2 = first(el2 == v2)
    e2 = jnp.exp(v2 - v1)
    w1 = 1.0 / (1.0 + e2)
    w2 = e2 / (1.0 + e2)
    return jnp.where(lane == i1, g_w * w1, 0.0) + jnp.where(lane == i2, g_w * w2, 0.0)


def _wo_moe_kernel(x_ref, a_ref, wo_ref, nw_ref, wr_hi_ref, wr_lo_ref, br_ref, wgu_ref, wd_ref,
                   fw_ref, o_ref, hid_ref, *, final):
    y1 = x_ref[...] + jnp.dot(a_ref[...], wo_ref[...], preferred_element_type=F32)
    t = _rms(y1, nw_ref[...], NORM_EPS)
    combine = _router_combine(_router_logits(t, wr_hi_ref, wr_lo_ref, br_ref))
    tb = t.astype(BF16)
    for e in range(N_EXPERTS):
        gu = jnp.dot(tb, wgu_ref[e], preferred_element_type=F32)
        gate, up = gu[:, :D_EXPERT], gu[:, D_EXPERT:]
        c = combine[:, N_GROUPS + e:N_GROUPS + e + 1]
        hid = gate * (1.0 / (1.0 + jnp.exp(-gate))) * up * c
        hid_ref[:, e * D_EXPERT:(e + 1) * D_EXPERT] = hid.astype(BF16)
    y2 = y1 + jnp.dot(hid_ref[...], wd_ref[...], preferred_element_type=F32)
    if final:
        y2 = _rms(y2, fw_ref[...], NORM_EPS)
    o_ref[...] = y2


def _wo_moe(x, attn, wo_bf, nw, wr_hi, wr_lo, br, wgu_bf, wd_bf, fw, *, final):
    n = x.shape[0]
    tm = min(512, n)
    row = lambda t: (t, 0)
    return pl.pallas_call(
        functools.partial(_wo_moe_kernel, final=final),
        grid=(n // tm,),
        in_specs=[pl.BlockSpec((tm, D_MODEL), row), pl.BlockSpec((tm, D_MODEL), row),
                  _resident(wo_bf.shape), _resident((1, D_MODEL)),
                  _resident(wr_hi.shape), _resident(wr_lo.shape), _resident(br.shape),
                  _resident(wgu_bf.shape), _resident(wd_bf.shape), _resident((1, D_MODEL))],
        out_specs=pl.BlockSpec((tm, D_MODEL), row),
        out_shape=jax.ShapeDtypeStruct((n, D_MODEL), F32),
        scratch_shapes=[pltpu.VMEM((tm, N_EXPERTS * D_EXPERT), BF16)],
        compiler_params=_params("parallel"), name="wo_moe",
    )(x, attn, wo_bf, nw.reshape(1, D_MODEL), wr_hi, wr_lo, br, wgu_bf, wd_bf,
      fw.reshape(1, D_MODEL))


def _wo_moe_f32_kernel(x_ref, a_ref, wo_ref, nw_ref, wr_ref, br_ref, wg_ref, wu_ref, wd_ref,
                       fw_ref, o_ref, y1_ref, t_ref, comb_ref, acc_ref, *, final):
    e = pl.program_id(0)

    @pl.when(e == 0)
    def _():
        y1 = x_ref[...] + _dot_f32(a_ref[...], wo_ref[...])
        t = _rms(y1, nw_ref[...], NORM_EPS)
        y1_ref[...] = y1
        t_ref[...] = t
        comb_ref[...] = _router_combine(_dot_f32(t, wr_ref[...]) + br_ref[...])
        acc_ref[...] = jnp.zeros(acc_ref.shape, F32)

    t = t_ref[...]
    gate = _dot_f32(t, wg_ref[0])
    up = _dot_f32(t, wu_ref[0])
    lane = lax.broadcasted_iota(jnp.int32, comb_ref.shape, 1)
    c = jnp.sum(jnp.where(lane == N_GROUPS + e, comb_ref[...], 0.0), axis=1, keepdims=True)
    hid = gate * (1.0 / (1.0 + jnp.exp(-gate))) * up * c
    acc_ref[...] += _dot_f32(hid, wd_ref[0])

    @pl.when(e == pl.num_programs(0) - 1)
    def _():
        y2 = y1_ref[...] + acc_ref[...]
        if final:
            y2 = _rms(y2, fw_ref[...], NORM_EPS)
        o_ref[...] = y2


def _wo_moe_f32(x, attn, wo, nw, wr, br, w_gate, w_up, w_down, fw, *, final):
    n = x.shape[0]
    whole = lambda shape: pl.BlockSpec(shape, lambda e: (0,) * len(shape))
    per_expert = lambda shape: pl.BlockSpec((1,) + shape, lambda e: (e, 0, 0))
    return pl.pallas_call(
        functools.partial(_wo_moe_f32_kernel, final=final),
        grid=(N_EXPERTS,),
        in_specs=[whole((n, D_MODEL)), whole((n, D_MODEL)), whole(wo.shape), whole((1, D_MODEL)),
                  whole(wr.shape), whole(br.shape), per_expert((D_MODEL, D_EXPERT)),
                  per_expert((D_MODEL, D_EXPERT)), per_expert((D_EXPERT, D_MODEL)),
                  whole((1, D_MODEL))],
        out_specs=whole((n, D_MODEL)),
        out_shape=jax.ShapeDtypeStruct((n, D_MODEL), F32),
        scratch_shapes=[pltpu.VMEM((n, D_MODEL), F32), pltpu.VMEM((n, D_MODEL), F32),
                        pltpu.VMEM((n, ROUTER_LANES), F32), pltpu.VMEM((n, D_MODEL), F32)],
        compiler_params=_params("arbitrary"), name="wo_moe_f32",
    )(x, attn, wo, nw.reshape(1, D_MODEL), wr, br, w_gate, w_up, w_down, fw.reshape(1, D_MODEL))


def _moe_weights(w_group, b_group, w_expert, b_expert, w_gate, w_up, w_down):
    n_r = N_GROUPS + N_EXPERTS
    wr = jnp.pad(jnp.concatenate([w_group, w_expert], axis=1), ((0, 0), (0, ROUTER_LANES - n_r)))
    wr_hi = wr.astype(BF16)
    wr_lo = (wr - wr_hi.astype(F32)).astype(BF16)
    br = jnp.pad(jnp.concatenate([b_group, b_expert]), (0, ROUTER_LANES - n_r)).reshape(1, -1)
    wgu = jnp.concatenate([w_gate, w_up], axis=2).astype(BF16)
    wd = w_down.reshape(N_EXPERTS * D_EXPERT, D_MODEL).astype(BF16)
    return (wr_hi, wr_lo, br, wgu, wd), (wr, br)


def _block_diag_rows(q, n_slots):
    db, _, m, dq, w = q.shape
    eye = jnp.eye(n_slots, dtype=q.dtype)
    out = q[:, :, :, :, None, :] * eye[None, :, None, None, :, None]
    return out.reshape(db, n_slots * m * dq, n_slots * w)


def _pad_tokens(x):
    return jnp.pad(x, ((0, 0), (0, PAGE_SIZE - x.shape[1])) + ((0, 0),) * (x.ndim - 2))


def kernel(x_prompt, x_sample, cache_k_moba, cache_v_moba, cache_k_diff, cache_v_diff, page_table,
           norm_attn, norm_ffn, norm_final, moba_w_qkv, moba_w_o, diff_w_qkv, diff_w_o,
           diff_lambda_q1, diff_lambda_k1, diff_lambda_q2, diff_lambda_k2, diff_subln,
           moe_w_group, moe_b_group, moe_w_expert, moe_b_expert, moe_w_gate, moe_w_up, moe_w_down):
    batch, seq_len, _ = x_prompt.shape
    db, dq, _ = x_sample.shape
    n_pool = cache_k_moba.shape[1]
    depth = norm_attn.shape[0]
    nb = seq_len // MOBA_BLOCK
    slopes_moba = _alibi_slopes(MOBA_HEADS)
    slopes_diff = _alibi_slopes(DIFF_HEADS)
    qi_new = np.arange(dq, dtype=np.float32)

    moba_pool_t = lambda c: jnp.transpose(c, (0, 1, 3, 4, 2)).reshape(-1, MOBA_KVW, PAGE_SIZE)
    diff_pool = lambda c: c.reshape(-1, PAGE_SIZE * DIFF_HEADS, DIFF_HW)
    k_moba_t, v_moba_t = moba_pool_t(cache_k_moba), moba_pool_t(cache_v_moba)
    k_diff, v_diff = diff_pool(cache_k_diff), diff_pool(cache_v_diff)

    yp = x_prompt.reshape(batch * seq_len, D_MODEL)
    ys = x_sample.reshape(db * dq, D_MODEL)
    outs = {name: [] for name in ("km_p", "vm_p", "kd_p", "vd_p", "km_s", "vm_s", "kd_s", "vd_s")}
    for i in range(depth):
        j = i // 2
        final = i == depth - 1
        if i % 2 == 0:
            w_bf = moba_w_qkv[j].astype(BF16)
            wkt_bf = moba_w_qkv[j][:, D_MODEL:D_MODEL + MOBA_KVW].T.astype(BF16)
            qp, kp, vp, kt, vh, kmean = _norm_qkv(yp, norm_attn[i], w_bf, mode="moba",
                                                  seq_len=seq_len, wkt_bf=wkt_bf)
            km = kmean.reshape(batch, nb, MOBA_KV_HEADS, HEAD_DIM).transpose(0, 2, 3, 1)
            km_rep = jnp.pad(jnp.tile(km, (1, 1, MOBA_GROUP, 1)),
                             ((0, 0), (0, 0), (0, 0), (0, LANES - nb)))
            attn_p = _moba_prompt(qp, km_rep, kt, vh, jnp.asarray(slopes_moba))

            qs, ks, vs = _norm_qkv(ys, norm_attn[i], moba_w_qkv[j], mode="plain")
            kmt = _kmean_pages(k_moba_t, page_table, j * n_pool)
            q5 = qs.reshape(db, dq, MOBA_KV_HEADS, MOBA_GROUP, HEAD_DIM).transpose(0, 2, 3, 1, 4)
            qf = _block_diag_rows(q5, MOBA_KV_HEADS)
            slope_rows = np.repeat(slopes_moba, dq).reshape(-1, 1)
            qi_rows = np.tile(qi_new, MOBA_HEADS).reshape(-1, 1)
            new_t = lambda x: _pad_tokens(x.reshape(db, dq, MOBA_KVW)).transpose(0, 2, 1)
            o_rows = _moba_decode(qf * SCALE, qf, kmt, jnp.asarray(slope_rows), jnp.asarray(qi_rows),
                                  new_t(ks), new_t(vs), k_moba_t, v_moba_t, page_table, j * n_pool)
            o6 = o_rows.reshape(db, MOBA_KV_HEADS, MOBA_GROUP, dq, MOBA_KV_HEADS, HEAD_DIM)
            o5 = jnp.stack([o6[:, g, :, :, g, :] for g in range(MOBA_KV_HEADS)], axis=1)
            attn_s = o5.transpose(0, 3, 1, 2, 4).reshape(db * dq, D_MODEL)
            w_o = moba_w_o[j]
            kvh, kvd = MOBA_KV_HEADS, HEAD_DIM
            names = ("km_p", "vm_p", "km_s", "vm_s")
        else:
            lambda_init = 0.8 - 0.6 * math.exp(-0.3 * i)
            w_bf = diff_w_qkv[j].astype(BF16)
            lams = [a[j].reshape(1, HEAD_DIM) for a in
                    (diff_lambda_q1, diff_lambda_k1, diff_lambda_q2, diff_lambda_k2)]
            subln_w = diff_subln[j].reshape(1, DIFF_HW)
            qp, kp, vp, kb, vb = _norm_qkv(yp, norm_attn[i], w_bf, mode="diff")
            attn_p = _diff_prompt(qp, kb, vb, lams, subln_w, jnp.asarray(slopes_diff),
                                  batch=batch, lambda_init=lambda_init)

            qs, ks, vs = _norm_qkv(ys, norm_attn[i], diff_w_qkv[j], mode="plain")
            q5 = qs.reshape(db, dq, DIFF_HEADS, 2, 1, HEAD_DIM).transpose(0, 2, 3, 4, 1, 5)
            qx = _block_diag_rows(q5.reshape(db * DIFF_HEADS, 2, 1, dq, HEAD_DIM), 2)
            qx = (qx.reshape(db, 2 * DIFF_HEADS * dq, DIFF_HW) * (SCALE * LOG2E)).astype(BF16)
            slope_rows = np.repeat(slopes_diff, 2 * dq).reshape(-1, 1)
            qi_rows = np.tile(qi_new, 2 * DIFF_HEADS).reshape(-1, 1)
            head_rows = np.repeat(np.arange(DIFF_HEADS, dtype=np.int32), 2 * dq).reshape(-1, 1)
            new_flat = lambda x: _pad_tokens(x.reshape(db, dq, DIFF_HEADS, DIFF_HW)).reshape(
                db, PAGE_SIZE * DIFF_HEADS, DIFF_HW)
            attn_s = _diff_decode(qx, jnp.asarray(slope_rows), jnp.asarray(qi_rows),
                                  jnp.asarray(head_rows), new_flat(ks), new_flat(vs), lams, subln_w,
                                  k_diff, v_diff, page_table, j * n_pool,
                                  lambda_init=lambda_init).reshape(db * dq, D_MODEL)
            w_o = diff_w_o[j]
            kvh, kvd = DIFF_HEADS, DIFF_HW
            names = ("kd_p", "vd_p", "kd_s", "vd_s")
        outs[names[0]].append(kp.reshape(batch, seq_len, kvh, kvd))
        outs[names[1]].append(vp.reshape(batch, seq_len, kvh, kvd))
        outs[names[2]].append(ks.reshape(db, dq, kvh, kvd))
        outs[names[3]].append(vs.reshape(db, dq, kvh, kvd))

        moe_w, (wr, br) = _moe_weights(moe_w_group[i], moe_b_group[i], moe_w_expert[i],
                                       moe_b_expert[i], moe_w_gate[i], moe_w_up[i], moe_w_down[i])
        yp = _wo_moe(yp, attn_p, w_o.astype(BF16), norm_ffn[i], *moe_w, norm_final, final=final)
        ys = _wo_moe_f32(ys, attn_s, w_o, norm_ffn[i], wr, br, moe_w_gate[i], moe_w_up[i],
                         moe_w_down[i], norm_final, final=final)

    stack = lambda name: jnp.stack(outs[name])
    return (yp.reshape(batch, seq_len, D_MODEL), ys.reshape(db, dq, D_MODEL),
            stack("km_p"), stack("vm_p"), stack("kd_p"), stack("vd_p"),
            stack("km_s"), stack("vm_s"), stack("kd_s"), stack("vd_s"))
```

```python
import functools
import math

import jax
import jax.numpy as jnp
import numpy as np
from jax import lax
from jax.experimental import pallas as pl
from jax.experimental.pallas import tpu as pltpu

D_MODEL = 1024
HEAD_DIM = 64
HEAD_DIM_LOG2 = 6
MOBA_HEADS = D_MODEL // HEAD_DIM
MOBA_KV_HEADS = 4
MOBA_GROUP = MOBA_HEADS // MOBA_KV_HEADS
MOBA_BLOCK = 256
MOBA_BLOCK_LOG2 = 8
MOBA_TOPK = 3
MOBA_KVW = MOBA_KV_HEADS * HEAD_DIM
DIFF_HEADS = D_MODEL // (2 * HEAD_DIM)
DIFF_HEADS_LOG2 = 3
DIFF_HW = 2 * HEAD_DIM
N_GROUPS = 4
EXPERTS_PER_GROUP = 4
N_EXPERTS = N_GROUPS * EXPERTS_PER_GROUP
D_EXPERT = 256
PAGE_SIZE = 128
NORM_EPS = 1e-6
SUBLN_EPS = 1e-5
NEG_INF = -1e30
SCALE = HEAD_DIM ** -0.5
LOG2E = math.log2(math.e)

PROMPT_TILE_BLOCKS = 4
MOBA_KV_PER_STEP = 2
DIFF_HEADS_PER_STEP = 4
LANES = 128
ROUTER_LANES = LANES
VMEM_LIMIT = 56 * 1024 * 1024

F32 = jnp.float32
BF16 = jnp.bfloat16
NT_DIMS = (((1,), (1,)), ((), ()))


def _alibi_slopes(n_heads):
    start = 2.0 ** (-8.0 / n_heads)
    return np.array([start ** (i + 1) for i in range(n_heads)], dtype=np.float32)


def _params(*sem):
    return pltpu.CompilerParams(dimension_semantics=sem, vmem_limit_bytes=VMEM_LIMIT)


def _resident(shape):
    nd = len(shape)
    return pl.BlockSpec(shape, lambda *_: (0,) * nd, pipeline_mode=pl.Buffered(1))


def _rms(x, w, eps):
    ms = jnp.mean(x * x, axis=-1, keepdims=True)
    return x * lax.rsqrt(ms + eps) * w


def _split_bf16(x):
    hi = x.astype(BF16)
    return hi, (x - hi.astype(F32)).astype(BF16)


def _nt_dot(a, b):
    return lax.dot_general(a, b, NT_DIMS, preferred_element_type=F32)


def _dot_f32(a, b):
    return jnp.dot(a, b, precision=lax.Precision.HIGHEST, preferred_element_type=F32)


def _norm_qkv_kernel(x_ref, nw_ref, w_ref, *refs, mode, kvw, tm):
    h = _rms(x_ref[...], nw_ref[...], NORM_EPS)
    if w_ref.dtype == F32:
        qkv = _dot_f32(h, w_ref[...])
    else:
        h = h.astype(BF16)
        qkv = jnp.dot(h, w_ref[...], preferred_element_type=F32)
    q = qkv[:, :D_MODEL]
    k = qkv[:, D_MODEL:D_MODEL + kvw]
    v = qkv[:, D_MODEL + kvw:]
    if mode == "plain":
        q_ref, k_ref, v_ref = refs
    elif mode == "moba":
        wkt_ref, q_ref, k_ref, v_ref, kt_ref, vh_ref, km_ref = refs
        kt = _nt_dot(wkt_ref[...], h)
        vb = v.astype(BF16)
        for jj in range(tm // MOBA_BLOCK):
            sl = slice(jj * MOBA_BLOCK, (jj + 1) * MOBA_BLOCK)
            kt_ref[0, jj] = kt[:, sl].astype(BF16)
            km_ref[jj] = jnp.mean(k[sl], axis=0, keepdims=True)
        for g in range(MOBA_KV_HEADS):
            vh_ref[0, g] = vb[:, g * HEAD_DIM:(g + 1) * HEAD_DIM]
    else:
        q_ref, k_ref, v_ref, kb_ref, vb_ref = refs
        kb_ref[...] = k.astype(BF16)
        vb_ref[...] = v.astype(BF16)
    q_ref[...] = q
    k_ref[...] = k
    v_ref[...] = v


def _norm_qkv(x, nw, w, *, mode, seq_len=None, wkt_bf=None):
    n = x.shape[0]
    e = w.shape[1]
    kvw = (e - D_MODEL) // 2
    tm = min(512, n)
    nt = n // tm
    row = lambda t: (t, 0)
    in_specs = [pl.BlockSpec((tm, D_MODEL), row), _resident((1, D_MODEL)), _resident((D_MODEL, e))]
    args = [x, nw.reshape(1, D_MODEL), w]
    out_shape = [jax.ShapeDtypeStruct((n, D_MODEL), F32),
                 jax.ShapeDtypeStruct((n, kvw), F32),
                 jax.ShapeDtypeStruct((n, kvw), F32)]
    out_specs = [pl.BlockSpec((tm, D_MODEL), row), pl.BlockSpec((tm, kvw), row),
                 pl.BlockSpec((tm, kvw), row)]
    if mode == "moba":
        b = n // seq_len
        tps = seq_len // tm
        bpt = tm // MOBA_BLOCK
        nb = seq_len // MOBA_BLOCK
        in_specs.append(_resident((kvw, D_MODEL)))
        args.append(wkt_bf)
        out_shape += [jax.ShapeDtypeStruct((b, nb, kvw, MOBA_BLOCK), BF16),
                      jax.ShapeDtypeStruct((b, MOBA_KV_HEADS, seq_len, HEAD_DIM), BF16),
                      jax.ShapeDtypeStruct((b * nb, 1, kvw), F32)]
        out_specs += [pl.BlockSpec((1, bpt, kvw, MOBA_BLOCK), lambda t: (t // tps, t % tps, 0, 0)),
                      pl.BlockSpec((1, MOBA_KV_HEADS, tm, HEAD_DIM), lambda t: (t // tps, 0, t % tps, 0)),
                      pl.BlockSpec((bpt, 1, kvw), lambda t: (t, 0, 0))]
    elif mode == "diff":
        out_shape += [jax.ShapeDtypeStruct((n, kvw), BF16), jax.ShapeDtypeStruct((n, kvw), BF16)]
        out_specs += [pl.BlockSpec((tm, kvw), row), pl.BlockSpec((tm, kvw), row)]
    return pl.pallas_call(
        functools.partial(_norm_qkv_kernel, mode=mode, kvw=kvw, tm=tm),
        grid=(nt,), in_specs=in_specs, out_specs=out_specs, out_shape=out_shape,
        compiler_params=_params("parallel"), name="norm_qkv_" + mode,
    )(*args)


def _softmax_update(s, m_ref, l_ref, rows, shift=0.0, base2=False):
    width = s.shape[1]
    exp = jnp.exp2 if base2 else jnp.exp
    m_prev = m_ref[rows, :]
    m_new = jnp.maximum(m_prev, jnp.max(s, axis=1, keepdims=True) + shift)
    alpha = exp(m_prev - m_new)
    p = exp(s - jnp.concatenate([m_new - shift] * (width // LANES), axis=1))
    l_ref[rows, :] = alpha * l_ref[rows, :] + jnp.sum(p, axis=1, keepdims=True)
    m_ref[rows, :] = m_new
    return p, alpha


def _init_softmax_state(m_ref, l_ref, acc_ref):
    m_ref[...] = jnp.full(m_ref.shape, -jnp.inf, F32)
    l_ref[...] = jnp.zeros(l_ref.shape, F32)
    acc_ref[...] = jnp.zeros(acc_ref.shape, F32)


def _widen(x, width):
    return jnp.concatenate([x] * (width // LANES), axis=1)


def _for_each_key_tile(i, attend):
    assert PROMPT_TILE_BLOCKS == 4

    def full_tile(jj, carry):
        attend(jj * PROMPT_TILE_BLOCKS, PROMPT_TILE_BLOCKS, False)
        return carry

    lax.fori_loop(0, i // PROMPT_TILE_BLOCKS, full_tile, 0)
    left = i % PROMPT_TILE_BLOCKS

    @pl.when(left >= 2)
    def _():
        attend(i - left, 2, False)

    @pl.when(left % 2 == 0)
    def _():
        attend(i, 1, True)

    @pl.when(left % 2 == 1)
    def _():
        attend(i - 1, 2, True)


def _top3_mask(gate, lane, n_valid):
    gt = jnp.where(lane < n_valid, gate, -jnp.inf)
    sel = jnp.zeros(gate.shape, F32)
    for _ in range(MOBA_TOPK):
        mx = jnp.max(gt, axis=1, keepdims=True)
        idx = jnp.min(jnp.where(gt == mx, lane, gate.shape[1]), axis=1, keepdims=True)
        hit = (lane == idx) & (mx > -jnp.inf)
        sel = jnp.where(hit, 1.0, sel)
        gt = jnp.where(hit, -jnp.inf, gt)
    return sel


def _moba_prompt_kernel(slopes_ref, q_ref, km_ref, kt_ref, v_ref, o_ref,
                        qz_ref, sel_ref, bias_ref, m_ref, l_ref, acc_ref, p_ref):
    tq = MOBA_BLOCK
    gw = MOBA_GROUP * HEAD_DIM
    gp = pl.program_id(1)
    i = pl.program_id(2)
    chains = range(MOBA_KV_PER_STEP)
    lane = lax.broadcasted_iota(jnp.int32, (tq, LANES), 1)
    slope2 = [[slopes_ref[(gp * MOBA_KV_PER_STEP + c) * MOBA_GROUP + r] * LOG2E
               for r in range(MOBA_GROUP)] for c in chains]
    for c in chains:
        q = q_ref[:, c * gw:(c + 1) * gw]
        gate = _dot_f32(q, km_ref[0, c])
        sel_ref[c] = _top3_mask(gate, lane, i)
        qs = q * (SCALE * LOG2E)
        qlane = lax.shift_right_logical(lax.broadcasted_iota(jnp.int32, qs.shape, 1), HEAD_DIM_LOG2)
        for r in range(MOBA_GROUP):
            qz_ref[c, r * tq:(r + 1) * tq, :] = jnp.where(qlane == r, qs, 0.0).astype(BF16)
    _init_softmax_state(m_ref, l_ref, acc_ref)

    @pl.when(i == 0)
    def _():
        width = PROMPT_TILE_BLOCKS * tq
        rel = (lax.broadcasted_iota(jnp.int32, (tq, width), 1)
               - lax.broadcasted_iota(jnp.int32, (tq, width), 0)).astype(F32)
        for c in chains:
            for r in range(MOBA_GROUP):
                bias_ref[c, r * tq:(r + 1) * tq, :] = slope2[c][r] * rel

    def attend(first, n, own_last):
        width = n * tq
        off = ((first - i) * tq).astype(F32)
        vs = pl.ds(pl.multiple_of(first * tq, tq), width)
        for c in chains:
            hd = slice(c * HEAD_DIM, (c + 1) * HEAD_DIM)
            kt = jnp.concatenate([kt_ref[0, first + u, hd, :] for u in range(n)], axis=1)
            kt4 = jnp.concatenate([kt] * MOBA_GROUP, axis=0)
            s = jnp.dot(qz_ref[c], kt4, preferred_element_type=F32)
            oks = []
            for u in range(n):
                if own_last and u == n - 1:
                    oks.append(lax.broadcasted_iota(jnp.int32, (tq, tq), 1)
                               <= lax.broadcasted_iota(jnp.int32, (tq, tq), 0))
                else:
                    picked = jnp.sum(jnp.where(lane == first + u, sel_ref[c], 0.0), axis=1,
                                     keepdims=True)
                    oks.append(jnp.broadcast_to(picked, (tq, tq)) > 0.0)
            ok = jnp.concatenate(oks, axis=1)
            for r in range(MOBA_GROUP):
                rows = slice(r * tq, (r + 1) * tq)
                sr = jnp.where(ok, s[rows] + bias_ref[c, rows, :width], NEG_INF)
                p, alpha = _softmax_update(sr, m_ref.at[c], l_ref.at[c], rows,
                                           shift=slope2[c][r] * off, base2=True)
                acc_ref[c, rows, :] = acc_ref[c, rows, :] * alpha[:, :HEAD_DIM]
                p_ref[c, rows, :width] = p.astype(BF16)
            acc_ref[c] += jnp.dot(p_ref[c, :, :width], v_ref[0, c, vs, :],
                                  preferred_element_type=F32)

    _for_each_key_tile(i, attend)
    out = [acc_ref[c, r * tq:(r + 1) * tq, :] / l_ref[c, r * tq:(r + 1) * tq, :HEAD_DIM]
           for c in chains for r in range(MOBA_GROUP)]
    o_ref[...] = jnp.concatenate(out, axis=1).astype(BF16)


def _moba_prompt(q, km_rep, kt, vh, slopes):
    b, nb = kt.shape[0], kt.shape[1]
    seq_len = nb * MOBA_BLOCK
    tq = MOBA_BLOCK
    gw = MOBA_GROUP * HEAD_DIM
    kps = MOBA_KV_PER_STEP
    rows = MOBA_GROUP * tq
    return pl.pallas_call(
        _moba_prompt_kernel,
        grid=(b, MOBA_KV_HEADS // kps, nb),
        in_specs=[pl.BlockSpec(memory_space=pltpu.SMEM),
                  pl.BlockSpec((tq, kps * gw), lambda bi, g, i: (bi * nb + i, g)),
                  pl.BlockSpec((1, kps, gw, LANES), lambda bi, g, i: (bi, g, 0, 0)),
                  pl.BlockSpec((1, nb, kps * HEAD_DIM, MOBA_BLOCK), lambda bi, g, i: (bi, 0, g, 0)),
                  pl.BlockSpec((1, kps, seq_len, HEAD_DIM), lambda bi, g, i: (bi, g, 0, 0))],
        out_specs=pl.BlockSpec((tq, kps * gw), lambda bi, g, i: (bi * nb + i, g)),
        out_shape=jax.ShapeDtypeStruct((b * seq_len, D_MODEL), BF16),
        scratch_shapes=[pltpu.VMEM((kps, rows, gw), BF16),
                        pltpu.VMEM((kps, tq, LANES), F32),
                        pltpu.VMEM((kps, rows, PROMPT_TILE_BLOCKS * tq), F32),
                        pltpu.VMEM((kps, rows, LANES), F32),
                        pltpu.VMEM((kps, rows, LANES), F32),
                        pltpu.VMEM((kps, rows, HEAD_DIM), F32),
                        pltpu.VMEM((kps, rows, PROMPT_TILE_BLOCKS * tq), BF16)],
        compiler_params=_params("parallel", "parallel", "arbitrary"), name="moba_prompt",
    )(slopes, q, km_rep, kt, vh)


def _lambda_full(lq1_ref, lk1_ref, lq2_ref, lk2_ref, lambda_init):
    a = jnp.sum(lq1_ref[...] * lk1_ref[...], axis=1, keepdims=True)
    b = jnp.sum(lq2_ref[...] * lk2_ref[...], axis=1, keepdims=True)
    return jnp.exp(a) - jnp.exp(b) + lambda_init


def _subln(o, w, lambda_init):
    return _rms(o, w, SUBLN_EPS) * (1.0 - lambda_init)


def _diff_prompt_kernel(slopes_ref, q_ref, k_ref, v_ref, lq1_ref, lk1_ref, lq2_ref, lk2_ref,
                        sw_ref, o_ref, qz_ref, bias_ref, m_ref, l_ref, acc_ref, *, tq, lambda_init):
    hp = pl.program_id(1)
    i = pl.program_id(2)
    heads = range(DIFF_HEADS_PER_STEP)
    lanes = [slice(c * DIFF_HW, (c + 1) * DIFF_HW) for c in heads]
    slope = [slopes_ref[hp * DIFF_HEADS_PER_STEP + c] * LOG2E for c in heads]
    everything = slice(None)
    for c in heads:
        qs = q_ref[:, lanes[c]] * (SCALE * LOG2E)
        qlane = lax.shift_right_logical(lax.broadcasted_iota(jnp.int32, qs.shape, 1), HEAD_DIM_LOG2)
        for mp in range(2):
            qz_ref[c, mp * tq:(mp + 1) * tq, :] = jnp.where(qlane == mp, qs, 0.0).astype(BF16)
    _init_softmax_state(m_ref, l_ref, acc_ref)

    def in_tile_offsets(width):
        rows2 = lax.broadcasted_iota(jnp.int32, (2 * tq, width), 0)
        return (lax.broadcasted_iota(jnp.int32, (2 * tq, width), 1)
                - jnp.where(rows2 >= tq, rows2 - tq, rows2))

    @pl.when(i == 0)
    def _():
        rel = in_tile_offsets(PROMPT_TILE_BLOCKS * tq).astype(F32)
        for c in heads:
            bias_ref[c] = slope[c] * rel

    def attend(first, n, own_last):
        width = n * tq
        ks = pl.ds(pl.multiple_of(first * tq, tq), width)
        for c in heads:
            s = _nt_dot(qz_ref[c], k_ref[ks, lanes[c]]) + bias_ref[c, :, :width]
            if own_last:
                s = jnp.where(in_tile_offsets(width) <= (n - 1) * tq, s, NEG_INF)
            p, alpha = _softmax_update(s, m_ref.at[c], l_ref.at[c], everything,
                                       shift=slope[c] * ((first - i) * tq).astype(F32), base2=True)
            acc_ref[c] = acc_ref[c] * alpha + jnp.dot(p.astype(BF16), v_ref[ks, lanes[c]],
                                                      preferred_element_type=F32)

    _for_each_key_tile(i, attend)
    lam = _lambda_full(lq1_ref, lk1_ref, lq2_ref, lk2_ref, lambda_init)
    for c in heads:
        o1 = acc_ref[c, :tq, :] / l_ref[c, :tq, :]
        o2 = acc_ref[c, tq:, :] / l_ref[c, tq:, :]
        o_ref[:, lanes[c]] = _subln(o1 - lam * o2, sw_ref[...], lambda_init).astype(BF16)


def _diff_prompt(q, kb, vb, lams, subln_w, slopes, *, batch, lambda_init):
    n = q.shape[0]
    seq_len = n // batch
    tq = 256
    nq = seq_len // tq
    lam_specs = [_resident((1, HEAD_DIM))] * 4
    hps = DIFF_HEADS_PER_STEP
    gw = hps * DIFF_HW
    return pl.pallas_call(
        functools.partial(_diff_prompt_kernel, tq=tq, lambda_init=lambda_init),
        grid=(batch, DIFF_HEADS // hps, nq),
        in_specs=[pl.BlockSpec(memory_space=pltpu.SMEM),
                  pl.BlockSpec((tq, gw), lambda bi, h, i: (bi * nq + i, h)),
                  pl.BlockSpec((seq_len, gw), lambda bi, h, i: (bi, h)),
                  pl.BlockSpec((seq_len, gw), lambda bi, h, i: (bi, h)),
                  *lam_specs, _resident((1, DIFF_HW))],
        out_specs=pl.BlockSpec((tq, gw), lambda bi, h, i: (bi * nq + i, h)),
        out_shape=jax.ShapeDtypeStruct((n, D_MODEL), BF16),
        scratch_shapes=[pltpu.VMEM((hps, 2 * tq, DIFF_HW), BF16),
                        pltpu.VMEM((hps, 2 * tq, PROMPT_TILE_BLOCKS * tq), F32),
                        pltpu.VMEM((hps, 2 * tq, LANES), F32),
                        pltpu.VMEM((hps, 2 * tq, LANES), F32),
                        pltpu.VMEM((hps, 2 * tq, DIFF_HW), F32)],
        compiler_params=_params("parallel", "parallel", "arbitrary"), name="diff_prompt",
    )(slopes, q, kb, vb, *lams, subln_w)


def _page_specs(block_shape, pages_per_step, page_offset):
    def spec(t):
        return pl.BlockSpec((1,) + block_shape,
                            lambda b, s, pt: (pt[b, s * pages_per_step + t] + page_offset, 0, 0))
    return [spec(t) for t in range(pages_per_step)]


def _kmean_pages_kernel(pt_ref, *refs):
    k_refs, o_ref = refs[:-1], refs[-1]
    step = pl.program_id(1)
    ppb = MOBA_BLOCK // PAGE_SIZE
    blocks_per_step = len(k_refs) // ppb
    lane = lax.broadcasted_iota(jnp.int32, o_ref.shape[1:], 1)
    upd = jnp.zeros(o_ref.shape[1:], F32)
    for t in range(blocks_per_step):
        tot = sum(k_refs[t * ppb + u][0] for u in range(ppb))
        col = jnp.sum(tot, axis=1, keepdims=True) / MOBA_BLOCK
        upd = jnp.where(lane == step * blocks_per_step + t, col, upd)

    @pl.when(step == 0)
    def _():
        o_ref[0] = upd

    @pl.when(step > 0)
    def _():
        o_ref[0] += upd


def _kmean_pages(pool_t, page_table, page_offset):
    db, n_pages = page_table.shape
    pps = 16
    width = pool_t.shape[1]
    grid_spec = pltpu.PrefetchScalarGridSpec(
        num_scalar_prefetch=1, grid=(db, n_pages // pps),
        in_specs=_page_specs((width, PAGE_SIZE), pps, page_offset),
        out_specs=pl.BlockSpec((1, width, LANES), lambda b, s, pt: (b, 0, 0)))
    return pl.pallas_call(
        _kmean_pages_kernel, grid_spec=grid_spec,
        out_shape=jax.ShapeDtypeStruct((db, width, LANES), F32),
        compiler_params=_params("parallel", "arbitrary"), name="kmean_pages",
    )(page_table, *([pool_t] * pps))


def _moba_decode_kernel(pt_ref, qs_ref, qf_ref, kmt_ref, slope_ref, qi_ref, knew_ref, vnew_ref,
                        *refs, pps, past_len):
    k_refs, v_refs = refs[:pps], refs[pps:2 * pps]
    o_ref, m_ref, l_ref, acc_ref, sel_ref = refs[2 * pps:]
    step = pl.program_id(1)
    n_rows = qs_ref.shape[1]
    chunk = pps * PAGE_SIZE
    everything = slice(None)

    @pl.when(step == 0)
    def _():
        _init_softmax_state(m_ref, l_ref, acc_ref)
        gr = _dot_f32(qf_ref[0], kmt_ref[0])
        n_blocks = past_len // MOBA_BLOCK
        rpg = n_rows // MOBA_KV_HEADS
        dq = rpg // MOBA_GROUP
        lane = lax.broadcasted_iota(jnp.int32, (dq, LANES), 1)
        for g in range(MOBA_KV_HEADS):
            gs = sum(gr[g * rpg + r * dq:g * rpg + (r + 1) * dq] for r in range(MOBA_GROUP))
            sel = _top3_mask(gs, lane, n_blocks)
            sel_ref[g * rpg:(g + 1) * rpg, :] = jnp.concatenate([sel] * MOBA_GROUP, axis=0)

    qh, ql = _split_bf16(qs_ref[0])
    q2 = jnp.concatenate([qh, ql], axis=0)

    def qk(kt32):
        kh, kl = _split_bf16(kt32)
        s2 = jnp.dot(q2, kh, preferred_element_type=F32)
        return s2[:n_rows] + s2[n_rows:] + jnp.dot(qh, kl, preferred_element_type=F32)

    def pv(p, vt32):
        ph, plo = _split_bf16(p)
        vh, vl = _split_bf16(vt32)
        o2 = _nt_dot(jnp.concatenate([ph, plo], axis=0), vh)
        return o2[:n_rows] + o2[n_rows:] + _nt_dot(ph, vl)

    def update(kt32, vt32, tok, ok, shift):
        s = qk(kt32) + slope_ref[...] * (tok - qi_ref[...])
        s = jnp.where(ok, s, NEG_INF)
        p, alpha = _softmax_update(s, m_ref, l_ref, everything, shift=shift)
        acc_ref[...] = acc_ref[...] * _widen(alpha, acc_ref.shape[1]) + pv(p, vt32)

    kt = jnp.concatenate([r[0] for r in k_refs], axis=1)
    vt = jnp.concatenate([r[0] for r in v_refs], axis=1)
    tok = lax.broadcasted_iota(jnp.int32, (1, chunk), 1).astype(F32)
    blk = lax.broadcasted_iota(jnp.int32, (LANES, chunk), 0)
    key = step * chunk + lax.broadcasted_iota(jnp.int32, (LANES, chunk), 1)
    expand = jnp.where(blk == lax.shift_right_logical(key, MOBA_BLOCK_LOG2), 1.0, 0.0).astype(BF16)
    ok = jnp.dot(sel_ref[...].astype(BF16), expand, preferred_element_type=F32) > 0.5
    update(kt, vt, tok, ok, slope_ref[...] * (step * chunk - past_len).astype(F32))

    @pl.when(step == pl.num_programs(1) - 1)
    def _():
        tok_new = lax.broadcasted_iota(jnp.int32, (1, PAGE_SIZE), 1).astype(F32)
        update(knew_ref[0], vnew_ref[0], tok_new, tok_new <= qi_ref[...], 0.0)
        o_ref[0] = acc_ref[...] / _widen(l_ref[...], acc_ref.shape[1])


def _moba_decode(q_scaled, q_plain, kmt, slope_rows, qi_rows, knew_t, vnew_t, k_pool_t, v_pool_t,
                 page_table, page_offset):
    db, n_pages = page_table.shape
    n_rows, width = q_scaled.shape[1], q_scaled.shape[2]
    pps = 16
    per_seq = lambda shape: pl.BlockSpec((1,) + shape, lambda b, s, pt: (b, 0, 0))
    const = lambda shape: pl.BlockSpec(shape, lambda b, s, pt: (0, 0))
    grid_spec = pltpu.PrefetchScalarGridSpec(
        num_scalar_prefetch=1, grid=(db, n_pages // pps),
        in_specs=[per_seq((n_rows, width)), per_seq((n_rows, width)), per_seq((width, LANES)),
                  const((n_rows, 1)), const((n_rows, 1)),
                  per_seq((width, PAGE_SIZE)), per_seq((width, PAGE_SIZE)),
                  *(_page_specs((width, PAGE_SIZE), pps, page_offset) * 2)],
        out_specs=per_seq((n_rows, width)),
        scratch_shapes=[pltpu.VMEM((n_rows, LANES), F32), pltpu.VMEM((n_rows, LANES), F32),
                        pltpu.VMEM((n_rows, width), F32), pltpu.VMEM((n_rows, LANES), F32)])
    return pl.pallas_call(
        functools.partial(_moba_decode_kernel, pps=pps, past_len=n_pages * PAGE_SIZE),
        grid_spec=grid_spec, out_shape=jax.ShapeDtypeStruct((db, n_rows, width), F32),
        compiler_params=_params("parallel", "arbitrary"), name="moba_decode",
    )(page_table, q_scaled, q_plain, kmt, slope_rows, qi_rows, knew_t, vnew_t,
      *([k_pool_t] * pps), *([v_pool_t] * pps))


def _diff_decode_kernel(pt_ref, qx_ref, slope_ref, qi_ref, head_ref, knew_ref, vnew_ref,
                        lq1_ref, lk1_ref, lq2_ref, lk2_ref, sw_ref, *refs, pps, past_len, lambda_init):
    k_refs, v_refs = refs[:pps], refs[pps:2 * pps]
    o_ref, bias_ref, m_ref, l_ref, acc_ref = refs[2 * pps:]
    step = pl.program_id(1)
    n_rows = qx_ref.shape[1]
    chunk = pps * PAGE_SIZE
    everything = slice(None)
    slope2 = slope_ref[...] * LOG2E

    def alibi_bias(n_cols):
        col = lax.broadcasted_iota(jnp.int32, (n_rows, n_cols), 1)
        tok = lax.shift_right_logical(col, DIFF_HEADS_LOG2)
        own_head = (col & (DIFF_HEADS - 1)) == head_ref[...]
        tokf = tok.astype(F32)
        return jnp.where(own_head, slope2 * (tokf - qi_ref[...]), NEG_INF), tokf

    @pl.when(step == 0)
    def _():
        _init_softmax_state(m_ref, l_ref, acc_ref)
        bias_ref[...] = alibi_bias(chunk * DIFF_HEADS)[0]

    def update(k32, v32, bias, shift):
        s = _nt_dot(qx_ref[0], k32.astype(BF16)) + bias
        p, alpha = _softmax_update(s, m_ref, l_ref, everything, shift=shift, base2=True)
        acc_ref[...] = acc_ref[...] * alpha + jnp.dot(p.astype(BF16), v32.astype(BF16),
                                                      preferred_element_type=F32)

    kc = jnp.concatenate([r[0] for r in k_refs], axis=0)
    vc = jnp.concatenate([r[0] for r in v_refs], axis=0)
    update(kc, vc, bias_ref[...], slope2 * (step * chunk - past_len).astype(F32))

    @pl.when(step == pl.num_programs(1) - 1)
    def _():
        bias_new, tokf = alibi_bias(PAGE_SIZE * DIFF_HEADS)
        update(knew_ref[0], vnew_ref[0], jnp.where(tokf <= qi_ref[...], bias_new, NEG_INF), 0.0)
        o = acc_ref[...] / l_ref[...]
        dq = n_rows // (2 * DIFF_HEADS)
        lam = _lambda_full(lq1_ref, lk1_ref, lq2_ref, lk2_ref, lambda_init)
        outs = []
        for hd in range(DIFF_HEADS):
            o1 = o[(2 * hd) * dq:(2 * hd + 1) * dq]
            o2 = o[(2 * hd + 1) * dq:(2 * hd + 2) * dq]
            outs.append(_subln(o1 - lam * o2, sw_ref[...], lambda_init))
        o_ref[0] = jnp.concatenate(outs, axis=1)


def _diff_decode(qx, slope_rows, qi_rows, head_rows, knew, vnew, lams, subln_w, k_pool, v_pool,
                 page_table, page_offset, *, lambda_init):
    db, n_pages = page_table.shape
    n_rows = qx.shape[1]
    pps = 8
    page_rows = PAGE_SIZE * DIFF_HEADS
    dq = n_rows // (2 * DIFF_HEADS)
    per_seq = lambda shape: pl.BlockSpec((1,) + shape, lambda b, s, pt: (b, 0, 0))
    const = lambda shape: pl.BlockSpec(shape, lambda b, s, pt: (0, 0))
    grid_spec = pltpu.PrefetchScalarGridSpec(
        num_scalar_prefetch=1, grid=(db, n_pages // pps),
        in_specs=[per_seq((n_rows, DIFF_HW)), const((n_rows, 1)), const((n_rows, 1)),
                  const((n_rows, 1)), per_seq((page_rows, DIFF_HW)), per_seq((page_rows, DIFF_HW)),
                  *([const((1, HEAD_DIM))] * 4), const((1, DIFF_HW)),
                  *(_page_specs((page_rows, DIFF_HW), pps, page_offset) * 2)],
        out_specs=per_seq((dq, D_MODEL)),
        scratch_shapes=[pltpu.VMEM((n_rows, pps * page_rows), F32),
                        pltpu.VMEM((n_rows, LANES), F32), pltpu.VMEM((n_rows, LANES), F32),
                        pltpu.VMEM((n_rows, DIFF_HW), F32)])
    return pl.pallas_call(
        functools.partial(_diff_decode_kernel, pps=pps, past_len=n_pages * PAGE_SIZE,
                          lambda_init=lambda_init),
        grid_spec=grid_spec, out_shape=jax.ShapeDtypeStruct((db, dq, D_MODEL), F32),
        compiler_params=_params("parallel", "arbitrary"), name="diff_decode",
    )(page_table, qx, slope_rows, qi_rows, head_rows, knew, vnew, *lams, subln_w,
      *([k_pool] * pps), *([v_pool] * pps))


def _router_logits(t, wr_hi_ref, wr_lo_ref, br_ref):
    t_hi, t_lo = _split_bf16(t)
    return (jnp.dot(t_hi, wr_hi_ref[...], preferred_element_type=F32)
            + jnp.dot(t_lo, wr_hi_ref[...], preferred_element_type=F32)
            + jnp.dot(t_hi, wr_lo_ref[...], preferred_element_type=F32)) + br_ref[...]


def _router_combine(logits):
    lane = lax.broadcasted_iota(jnp.int32, logits.shape, 1)
    width = logits.shape[1]
    first = lambda hit: jnp.min(jnp.where(hit, lane, width), axis=1, keepdims=True)
    is_g = lane < N_GROUPS
    gl = jnp.where(is_g, logits, -jnp.inf)
    g_max = jnp.max(gl, axis=1, keepdims=True)
    g_sel = first(gl == g_max)
    g_w = 1.0 / jnp.sum(jnp.where(is_g, jnp.exp(logits - g_max), 0.0), axis=1, keepdims=True)
    e_lo = N_GROUPS + g_sel * EXPERTS_PER_GROUP
    el = jnp.where((lane >= e_lo) & (lane < e_lo + EXPERTS_PER_GROUP), logits, -jnp.inf)
    v1 = jnp.max(el, axis=1, keepdims=True)
    i1 = first(el == v1)
    el2 = jnp.where(lane == i1, -jnp.inf, el)
    v2 = jnp.max(el2, axis=1, keepdims=True)
    i2 = first(el2 == v2)
    e2 = jnp.exp(v2 - v1)
    w1 = 1.0 / (1.0 + e2)
    w2 = e2 / (1.0 + e2)
    return jnp.where(lane == i1, g_w * w1, 0.0) + jnp.where(lane == i2, g_w * w2, 0.0)


def _wo_moe_kernel(x_ref, a_ref, wo_ref, nw_ref, wr_hi_ref, wr_lo_ref, br_ref, wgu_ref, wd_ref,
                   fw_ref, o_ref, hid_ref, *, final):
    y1 = x_ref[...] + jnp.dot(a_ref[...], wo_ref[...], preferred_element_type=F32)
    t = _rms(y1, nw_ref[...], NORM_EPS)
    combine = _router_combine(_router_logits(t, wr_hi_ref, wr_lo_ref, br_ref))
    tb = t.astype(BF16)
    for e in range(N_EXPERTS):
        gu = jnp.dot(tb, wgu_ref[e], preferred_element_type=F32)
        gate, up = gu[:, :D_EXPERT], gu[:, D_EXPERT:]
        c = combine[:, N_GROUPS + e:N_GROUPS + e + 1]
        hid = gate * (1.0 / (1.0 + jnp.exp(-gate))) * up * c
        hid_ref[:, e * D_EXPERT:(e + 1) * D_EXPERT] = hid.astype(BF16)
    y2 = y1 + jnp.dot(hid_ref[...], wd_ref[...], preferred_element_type=F32)
    if final:
        y2 = _rms(y2, fw_ref[...], NORM_EPS)
    o_ref[...] = y2


def _wo_moe(x, attn, wo_bf, nw, wr_hi, wr_lo, br, wgu_bf, wd_bf, fw, *, final):
    n = x.shape[0]
    tm = min(512, n)
    row = lambda t: (t, 0)
    return pl.pallas_call(
        functools.partial(_wo_moe_kernel, final=final),
        grid=(n // tm,),
        in_specs=[pl.BlockSpec((tm, D_MODEL), row), pl.BlockSpec((tm, D_MODEL), row),
                  _resident(wo_bf.shape), _resident((1, D_MODEL)),
                  _resident(wr_hi.shape), _resident(wr_lo.shape), _resident(br.shape),
                  _resident(wgu_bf.shape), _resident(wd_bf.shape), _resident((1, D_MODEL))],
        out_specs=pl.BlockSpec((tm, D_MODEL), row),
        out_shape=jax.ShapeDtypeStruct((n, D_MODEL), F32),
        scratch_shapes=[pltpu.VMEM((tm, N_EXPERTS * D_EXPERT), BF16)],
        compiler_params=_params("parallel"), name="wo_moe",
    )(x, attn, wo_bf, nw.reshape(1, D_MODEL), wr_hi, wr_lo, br, wgu_bf, wd_bf,
      fw.reshape(1, D_MODEL))


def _wo_moe_f32_kernel(x_ref, a_ref, wo_ref, nw_ref, wr_ref, br_ref, wg_ref, wu_ref, wd_ref,
                       fw_ref, o_ref, y1_ref, t_ref, comb_ref, acc_ref, *, final):
    e = pl.program_id(0)

    @pl.when(e == 0)
    def _():
        y1 = x_ref[...] + _dot_f32(a_ref[...], wo_ref[...])
        t = _rms(y1, nw_ref[...], NORM_EPS)
        y1_ref[...] = y1
        t_ref[...] = t
        comb_ref[...] = _router_combine(_dot_f32(t, wr_ref[...]) + br_ref[...])
        acc_ref[...] = jnp.zeros(acc_ref.shape, F32)

    t = t_ref[...]
    gate = _dot_f32(t, wg_ref[0])
    up = _dot_f32(t, wu_ref[0])
    lane = lax.broadcasted_iota(jnp.int32, comb_ref.shape, 1)
    c = jnp.sum(jnp.where(lane == N_GROUPS + e, comb_ref[...], 0.0), axis=1, keepdims=True)
    hid = gate * (1.0 / (1.0 + jnp.exp(-gate))) * up * c
    acc_ref[...] += _dot_f32(hid, wd_ref[0])

    @pl.when(e == pl.num_programs(0) - 1)
    def _():
        y2 = y1_ref[...] + acc_ref[...]
        if final:
            y2 = _rms(y2, fw_ref[...], NORM_EPS)
        o_ref[...] = y2


def _wo_moe_f32(x, attn, wo, nw, wr, br, w_gate, w_up, w_down, fw, *, final):
    n = x.shape[0]
    whole = lambda shape: pl.BlockSpec(shape, lambda e: (0,) * len(shape))
    per_expert = lambda shape: pl.BlockSpec((1,) + shape, lambda e: (e, 0, 0))
    return pl.pallas_call(
        functools.partial(_wo_moe_f32_kernel, final=final),
        grid=(N_EXPERTS,),
        in_specs=[whole((n, D_MODEL)), whole((n, D_MODEL)), whole(wo.shape), whole((1, D_MODEL)),
                  whole(wr.shape), whole(br.shape), per_expert((D_MODEL, D_EXPERT)),
                  per_expert((D_MODEL, D_EXPERT)), per_expert((D_EXPERT, D_MODEL)),
                  whole((1, D_MODEL))],
        out_specs=whole((n, D_MODEL)),
        out_shape=jax.ShapeDtypeStruct((n, D_MODEL), F32),
        scratch_shapes=[pltpu.VMEM((n, D_MODEL), F32), pltpu.VMEM((n, D_MODEL), F32),
                        pltpu.VMEM((n, ROUTER_LANES), F32), pltpu.VMEM((n, D_MODEL), F32)],
        compiler_params=_params("arbitrary"), name="wo_moe_f32",
    )(x, attn, wo, nw.reshape(1, D_MODEL), wr, br, w_gate, w_up, w_down, fw.reshape(1, D_MODEL))


def _moe_weights(w_group, b_group, w_expert, b_expert, w_gate, w_up, w_down):
    n_r = N_GROUPS + N_EXPERTS
    wr = jnp.pad(jnp.concatenate([w_group, w_expert], axis=1), ((0, 0), (0, ROUTER_LANES - n_r)))
    wr_hi = wr.astype(BF16)
    wr_lo = (wr - wr_hi.astype(F32)).astype(BF16)
    br = jnp.pad(jnp.concatenate([b_group, b_expert]), (0, ROUTER_LANES - n_r)).reshape(1, -1)
    wgu = jnp.concatenate([w_gate, w_up], axis=2).astype(BF16)
    wd = w_down.reshape(N_EXPERTS * D_EXPERT, D_MODEL).astype(BF16)
    return (wr_hi, wr_lo, br, wgu, wd), (wr, br)


def _block_diag_rows(q, n_slots):
    db, _, m, dq, w = q.shape
    eye = jnp.eye(n_slots, dtype=q.dtype)
    out = q[:, :, :, :, None, :] * eye[None, :, None, None, :, None]
    return out.reshape(db, n_slots * m * dq, n_slots * w)


def _pad_tokens(x):
    return jnp.pad(x, ((0, 0), (0, PAGE_SIZE - x.shape[1])) + ((0, 0),) * (x.ndim - 2))


def kernel(x_prompt, x_sample, cache_k_moba, cache_v_moba, cache_k_diff, cache_v_diff, page_table,
           norm_attn, norm_ffn, norm_final, moba_w_qkv, moba_w_o, diff_w_qkv, diff_w_o,
           diff_lambda_q1, diff_lambda_k1, diff_lambda_q2, diff_lambda_k2, diff_subln,
           moe_w_group, moe_b_group, moe_w_expert, moe_b_expert, moe_w_gate, moe_w_up, moe_w_down):
    batch, seq_len, _ = x_prompt.shape
    db, dq, _ = x_sample.shape
    n_pool = cache_k_moba.shape[1]
    depth = norm_attn.shape[0]
    nb = seq_len // MOBA_BLOCK
    slopes_moba = _alibi_slopes(MOBA_HEADS)
    slopes_diff = _alibi_slopes(DIFF_HEADS)
    qi_new = np.arange(dq, dtype=np.float32)

    moba_pool_t = lambda c: jnp.transpose(c, (0, 1, 3, 4, 2)).reshape(-1, MOBA_KVW, PAGE_SIZE)
    diff_pool = lambda c: c.reshape(-1, PAGE_SIZE * DIFF_HEADS, DIFF_HW)
    k_moba_t, v_moba_t = moba_pool_t(cache_k_moba), moba_pool_t(cache_v_moba)
    k_diff, v_diff = diff_pool(cache_k_diff), diff_pool(cache_v_diff)

    yp = x_prompt.reshape(batch * seq_len, D_MODEL)
    ys = x_sample.reshape(db * dq, D_MODEL)
    outs = {name: [] for name in ("km_p", "vm_p", "kd_p", "vd_p", "km_s", "vm_s", "kd_s", "vd_s")}
    for i in range(depth):
        j = i // 2
        final = i == depth - 1
        if i % 2 == 0:
            w_bf = moba_w_qkv[j].astype(BF16)
            wkt_bf = moba_w_qkv[j][:, D_MODEL:D_MODEL + MOBA_KVW].T.astype(BF16)
            qp, kp, vp, kt, vh, kmean = _norm_qkv(yp, norm_attn[i], w_bf, mode="moba",
                                                  seq_len=seq_len, wkt_bf=wkt_bf)
            km = kmean.reshape(batch, nb, MOBA_KV_HEADS, HEAD_DIM).transpose(0, 2, 3, 1)
            km_rep = jnp.pad(jnp.tile(km, (1, 1, MOBA_GROUP, 1)),
                             ((0, 0), (0, 0), (0, 0), (0, LANES - nb)))
            attn_p = _moba_prompt(qp, km_rep, kt, vh, jnp.asarray(slopes_moba))

            qs, ks, vs = _norm_qkv(ys, norm_attn[i], moba_w_qkv[j], mode="plain")
            kmt = _kmean_pages(k_moba_t, page_table, j * n_pool)
            q5 = qs.reshape(db, dq, MOBA_KV_HEADS, MOBA_GROUP, HEAD_DIM).transpose(0, 2, 3, 1, 4)
            qf = _block_diag_rows(q5, MOBA_KV_HEADS)
            slope_rows = np.repeat(slopes_moba, dq).reshape(-1, 1)
            qi_rows = np.tile(qi_new, MOBA_HEADS).reshape(-1, 1)
            new_t = lambda x: _pad_tokens(x.reshape(db, dq, MOBA_KVW)).transpose(0, 2, 1)
            o_rows = _moba_decode(qf * SCALE, qf, kmt, jnp.asarray(slope_rows), jnp.asarray(qi_rows),
                                  new_t(ks), new_t(vs), k_moba_t, v_moba_t, page_table, j * n_pool)
            o6 = o_rows.reshape(db, MOBA_KV_HEADS, MOBA_GROUP, dq, MOBA_KV_HEADS, HEAD_DIM)
            o5 = jnp.stack([o6[:, g, :, :, g, :] for g in range(MOBA_KV_HEADS)], axis=1)
            attn_s = o5.transpose(0, 3, 1, 2, 4).reshape(db * dq, D_MODEL)
            w_o = moba_w_o[j]
            kvh, kvd = MOBA_KV_HEADS, HEAD_DIM
            names = ("km_p", "vm_p", "km_s", "vm_s")
        else:
            lambda_init = 0.8 - 0.6 * math.exp(-0.3 * i)
            w_bf = diff_w_qkv[j].astype(BF16)
            lams = [a[j].reshape(1, HEAD_DIM) for a in
                    (diff_lambda_q1, diff_lambda_k1, diff_lambda_q2, diff_lambda_k2)]
            subln_w = diff_subln[j].reshape(1, DIFF_HW)
            qp, kp, vp, kb, vb = _norm_qkv(yp, norm_attn[i], w_bf, mode="diff")
            attn_p = _diff_prompt(qp, kb, vb, lams, subln_w, jnp.asarray(slopes_diff),
                                  batch=batch, lambda_init=lambda_init)

            qs, ks, vs = _norm_qkv(ys, norm_attn[i], diff_w_qkv[j], mode="plain")
            q5 = qs.reshape(db, dq, DIFF_HEADS, 2, 1, HEAD_DIM).transpose(0, 2, 3, 4, 1, 5)
            qx = _block_diag_rows(q5.reshape(db * DIFF_HEADS, 2, 1, dq, HEAD_DIM), 2)
            qx = (qx.reshape(db, 2 * DIFF_HEADS * dq, DIFF_HW) * (SCALE * LOG2E)).astype(BF16)
            slope_rows = np.repeat(slopes_diff, 2 * dq).reshape(-1, 1)
            qi_rows = np.tile(qi_new, 2 * DIFF_HEADS).reshape(-1, 1)
            head_rows = np.repeat(np.arange(DIFF_HEADS, dtype=np.int32), 2 * dq).reshape(-1, 1)
            new_flat = lambda x: _pad_tokens(x.reshape(db, dq, DIFF_HEADS, DIFF_HW)).reshape(
                db, PAGE_SIZE * DIFF_HEADS, DIFF_HW)
            attn_s = _diff_decode(qx, jnp.asarray(slope_rows), jnp.asarray(qi_rows),
                                  jnp.asarray(head_rows), new_flat(ks), new_flat(vs), lams, subln_w,
                                  k_diff, v_diff, page_table, j * n_pool,
                                  lambda_init=lambda_init).reshape(db * dq, D_MODEL)
            w_o = diff_w_o[j]
            kvh, kvd = DIFF_HEADS, DIFF_HW
            names = ("kd_p", "vd_p", "kd_s", "vd_s")
        outs[names[0]].append(kp.reshape(batch, seq_len, kvh, kvd))
        outs[names[1]].append(vp.reshape(batch, seq_len, kvh, kvd))
        outs[names[2]].append(ks.reshape(db, dq, kvh, kvd))
        outs[names[3]].append(vs.reshape(db, dq, kvh, kvd))

        moe_w, (wr, br) = _moe_weights(moe_w_group[i], moe_b_group[i], moe_w_expert[i],
                                       moe_b_expert[i], moe_w_gate[i], moe_w_up[i], moe_w_down[i])
        yp = _wo_moe(yp, attn_p, w_o.astype(BF16), norm_ffn[i], *moe_w, norm_final, final=final)
        ys = _wo_moe_f32(ys, attn_s, w_o, norm_ffn[i], wr, br, moe_w_gate[i], moe_w_up[i],
                         moe_w_down[i], norm_final, final=final)

    stack = lambda name: jnp.stack(outs[name])
    return (yp.reshape(batch, seq_len, D_MODEL), ys.reshape(db, dq, D_MODEL),
            stack("km_p"), stack("vm_p"), stack("kd_p"), stack("vd_p"),
            stack("km_s"), stack("vm_s"), stack("kd_s"), stack("vd_s"))
```

```python
import functools
import math

import jax
import jax.numpy as jnp
import numpy as np
from jax import lax
from jax.experimental import pallas as pl
from jax.experimental.pallas import tpu as pltpu

D_MODEL = 1024
HEAD_DIM = 64
HEAD_DIM_LOG2 = 6
MOBA_HEADS = D_MODEL // HEAD_DIM
MOBA_KV_HEADS = 4
MOBA_GROUP = MOBA_HEADS // MOBA_KV_HEADS
MOBA_BLOCK = 256
MOBA_BLOCK_LOG2 = 8
MOBA_TOPK = 3
MOBA_KVW = MOBA_KV_HEADS * HEAD_DIM
DIFF_HEADS = D_MODEL // (2 * HEAD_DIM)
DIFF_HEADS_LOG2 = 3
DIFF_HW = 2 * HEAD_DIM
N_GROUPS = 4
EXPERTS_PER_GROUP = 4
N_EXPERTS = N_GROUPS * EXPERTS_PER_GROUP
D_EXPERT = 256
PAGE_SIZE = 128
NORM_EPS = 1e-6
SUBLN_EPS = 1e-5
NEG_INF = -1e30
SCALE = HEAD_DIM ** -0.5
LOG2E = math.log2(math.e)

PROMPT_TILE_BLOCKS = 4
MOBA_KV_PER_STEP = 2
DIFF_HEADS_PER_STEP = 4
LANES = 128
ROUTER_LANES = LANES
VMEM_LIMIT = 56 * 1024 * 1024

F32 = jnp.float32
BF16 = jnp.bfloat16
NT_DIMS = (((1,), (1,)), ((), ()))


def _alibi_slopes(n_heads):
    start = 2.0 ** (-8.0 / n_heads)
    return np.array([start ** (i + 1) for i in range(n_heads)], dtype=np.float32)


def _params(*sem):
    return pltpu.CompilerParams(dimension_semantics=sem, vmem_limit_bytes=VMEM_LIMIT)


def _resident(shape):
    nd = len(shape)
    return pl.BlockSpec(shape, lambda *_: (0,) * nd, pipeline_mode=pl.Buffered(1))


def _rms(x, w, eps):
    ms = jnp.mean(x * x, axis=-1, keepdims=True)
    return x * lax.rsqrt(ms + eps) * w


def _split_bf16(x):
    hi = x.astype(BF16)
    return hi, (x - hi.astype(F32)).astype(BF16)


def _nt_dot(a, b):
    return lax.dot_general(a, b, NT_DIMS, preferred_element_type=F32)


def _dot_f32(a, b):
    return jnp.dot(a, b, precision=lax.Precision.HIGHEST, preferred_element_type=F32)


def _norm_qkv_kernel(x_ref, nw_ref, w_ref, *refs, mode, kvw, tm):
    h = _rms(x_ref[...], nw_ref[...], NORM_EPS)
    if w_ref.dtype == F32:
        qkv = _dot_f32(h, w_ref[...])
    else:
        h = h.astype(BF16)
        qkv = jnp.dot(h, w_ref[...], preferred_element_type=F32)
    q = qkv[:, :D_MODEL]
    k = qkv[:, D_MODEL:D_MODEL + kvw]
    v = qkv[:, D_MODEL + kvw:]
    if mode == "plain":
        q_ref, k_ref, v_ref = refs
    elif mode == "moba":
        wkt_ref, q_ref, k_ref, v_ref, kt_ref, vh_ref, km_ref = refs
        kt = _nt_dot(wkt_ref[...], h)
        vb = v.astype(BF16)
        for jj in range(tm // MOBA_BLOCK):
            sl = slice(jj * MOBA_BLOCK, (jj + 1) * MOBA_BLOCK)
            kt_ref[0, jj] = kt[:, sl].astype(BF16)
            km_ref[jj] = jnp.mean(k[sl], axis=0, keepdims=True)
        for g in range(MOBA_KV_HEADS):
            vh_ref[0, g] = vb[:, g * HEAD_DIM:(g + 1) * HEAD_DIM]
    else:
        q_ref, k_ref, v_ref, kb_ref, vb_ref = refs
        kb_ref[...] = k.astype(BF16)
        vb_ref[...] = v.astype(BF16)
    q_ref[...] = q
    k_ref[...] = k
    v_ref[...] = v


def _norm_qkv(x, nw, w, *, mode, seq_len=None, wkt_bf=None):
    n = x.shape[0]
    e = w.shape[1]
    kvw = (e - D_MODEL) // 2
    tm = min(512, n)
    nt = n // tm
    row = lambda t: (t, 0)
    in_specs = [pl.BlockSpec((tm, D_MODEL), row), _resident((1, D_MODEL)), _resident((D_MODEL, e))]
    args = [x, nw.reshape(1, D_MODEL), w]
    out_shape = [jax.ShapeDtypeStruct((n, D_MODEL), F32),
                 jax.ShapeDtypeStruct((n, kvw), F32),
                 jax.ShapeDtypeStruct((n, kvw), F32)]
    out_specs = [pl.BlockSpec((tm, D_MODEL), row), pl.BlockSpec((tm, kvw), row),
                 pl.BlockSpec((tm, kvw), row)]
    if mode == "moba":
        b = n // seq_len
        tps = seq_len // tm
        bpt = tm // MOBA_BLOCK
        nb = seq_len // MOBA_BLOCK
        in_specs.append(_resident((kvw, D_MODEL)))
        args.append(wkt_bf)
        out_shape += [jax.ShapeDtypeStruct((b, nb, kvw, MOBA_BLOCK), BF16),
                      jax.ShapeDtypeStruct((b, MOBA_KV_HEADS, seq_len, HEAD_DIM), BF16),
                      jax.ShapeDtypeStruct((b * nb, 1, kvw), F32)]
        out_specs += [pl.BlockSpec((1, bpt, kvw, MOBA_BLOCK), lambda t: (t // tps, t % tps, 0, 0)),
                      pl.BlockSpec((1, MOBA_KV_HEADS, tm, HEAD_DIM), lambda t: (t // tps, 0, t % tps, 0)),
                      pl.BlockSpec((bpt, 1, kvw), lambda t: (t, 0, 0))]
    elif mode == "diff":
        out_shape += [jax.ShapeDtypeStruct((n, kvw), BF16), jax.ShapeDtypeStruct((n, kvw), BF16)]
        out_specs += [pl.BlockSpec((tm, kvw), row), pl.BlockSpec((tm, kvw), row)]
    return pl.pallas_call(
        functools.partial(_norm_qkv_kernel, mode=mode, kvw=kvw, tm=tm),
        grid=(nt,), in_specs=in_specs, out_specs=out_specs, out_shape=out_shape,
        compiler_params=_params("parallel"), name="norm_qkv_" + mode,
    )(*args)


def _softmax_update(s, m_ref, l_ref, rows, shift=0.0, base2=False):
    width = s.shape[1]
    exp = jnp.exp2 if base2 else jnp.exp
    m_prev = m_ref[rows, :]
    m_new = jnp.maximum(m_prev, jnp.max(s, axis=1, keepdims=True) + shift)
    alpha = exp(m_prev - m_new)
    p = exp(s - jnp.concatenate([m_new - shift] * (width // LANES), axis=1))
    l_ref[rows, :] = alpha * l_ref[rows, :] + jnp.sum(p, axis=1, keepdims=True)
    m_ref[rows, :] = m_new
    return p, alpha


def _init_softmax_state(m_ref, l_ref, acc_ref):
    m_ref[...] = jnp.full(m_ref.shape, -jnp.inf, F32)
    l_ref[...] = jnp.zeros(l_ref.shape, F32)
    acc_ref[...] = jnp.zeros(acc_ref.shape, F32)


def _widen(x, width):
    return jnp.concatenate([x] * (width // LANES), axis=1)


def _for_each_key_tile(i, attend):
    assert PROMPT_TILE_BLOCKS == 4

    def full_tile(jj, carry):
        attend(jj * PROMPT_TILE_BLOCKS, PROMPT_TILE_BLOCKS, False)
        return carry

    lax.fori_loop(0, i // PROMPT_TILE_BLOCKS, full_tile, 0)
    left = i % PROMPT_TILE_BLOCKS

    @pl.when(left >= 2)
    def _():
        attend(i - left, 2, False)

    @pl.when(left % 2 == 0)
    def _():
        attend(i, 1, True)

    @pl.when(left % 2 == 1)
    def _():
        attend(i - 1, 2, True)


def _top3_mask(gate, lane, n_valid):
    gt = jnp.where(lane < n_valid, gate, -jnp.inf)
    sel = jnp.zeros(gate.shape, F32)
    for _ in range(MOBA_TOPK):
        mx = jnp.max(gt, axis=1, keepdims=True)
        idx = jnp.min(jnp.where(gt == mx, lane, gate.shape[1]), axis=1, keepdims=True)
        hit = (lane == idx) & (mx > -jnp.inf)
        sel = jnp.where(hit, 1.0, sel)
        gt = jnp.where(hit, -jnp.inf, gt)
    return sel


def _moba_prompt_kernel(slopes_ref, q_ref, km_ref, kt_ref, v_ref, o_ref,
                        qz_ref, sel_ref, bias_ref, m_ref, l_ref, acc_ref, p_ref):
    tq = MOBA_BLOCK
    gw = MOBA_GROUP * HEAD_DIM
    gp = pl.program_id(1)
    i = pl.program_id(2)
    chains = range(MOBA_KV_PER_STEP)
    lane = lax.broadcasted_iota(jnp.int32, (tq, LANES), 1)
    slope2 = [[slopes_ref[(gp * MOBA_KV_PER_STEP + c) * MOBA_GROUP + r] * LOG2E
               for r in range(MOBA_GROUP)] for c in chains]
    for c in chains:
        q = q_ref[:, c * gw:(c + 1) * gw]
        gate = _dot_f32(q, km_ref[0, c])
        sel_ref[c] = _top3_mask(gate, lane, i)
        qs = q * (SCALE * LOG2E)
        qlane = lax.shift_right_logical(lax.broadcasted_iota(jnp.int32, qs.shape, 1), HEAD_DIM_LOG2)
        for r in range(MOBA_GROUP):
            qz_ref[c, r * tq:(r + 1) * tq, :] = jnp.where(qlane == r, qs, 0.0).astype(BF16)
    _init_softmax_state(m_ref, l_ref, acc_ref)

    @pl.when(i == 0)
    def _():
        width = PROMPT_TILE_BLOCKS * tq
        rel = (lax.broadcasted_iota(jnp.int32, (tq, width), 1)
               - lax.broadcasted_iota(jnp.int32, (tq, width), 0)).astype(F32)
        for c in chains:
            for r in range(MOBA_GROUP):
                bias_ref[c, r * tq:(r + 1) * tq, :] = slope2[c][r] * rel

    def attend(first, n, own_last):
        width = n * tq
        off = ((first - i) * tq).astype(F32)
        vs = pl.ds(pl.multiple_of(first * tq, tq), width)
        for c in chains:
            hd = slice(c * HEAD_DIM, (c + 1) * HEAD_DIM)
            kt = jnp.concatenate([kt_ref[0, first + u, hd, :] for u in range(n)], axis=1)
            kt4 = jnp.concatenate([kt] * MOBA_GROUP, axis=0)
            s = jnp.dot(qz_ref[c], kt4, preferred_element_type=F32)
            oks = []
            for u in range(n):
                if own_last and u == n - 1:
                    oks.append(lax.broadcasted_iota(jnp.int32, (tq, tq), 1)
                               <= lax.broadcasted_iota(jnp.int32, (tq, tq), 0))
                else:
                    oks.append(jnp.sum(jnp.where(lane == first + u, sel_ref[c], 0.0), axis=1,
                                       keepdims=True) > 0.0)
            for r in range(MOBA_GROUP):
                rows = slice(r * tq, (r + 1) * tq)
                sr = jnp.concatenate(
                    [jnp.where(oks[u], s[rows, u * tq:(u + 1) * tq]
                               + bias_ref[c, rows, u * tq:(u + 1) * tq], NEG_INF)
                     for u in range(n)], axis=1)
                p, alpha = _softmax_update(sr, m_ref.at[c], l_ref.at[c], rows,
                                           shift=slope2[c][r] * off, base2=True)
                acc_ref[c, rows, :] = (acc_ref[c, rows, :] * alpha[:, :HEAD_DIM]
                                       + jnp.dot(p.astype(BF16), v_ref[0, c, vs, :],
                                                 preferred_element_type=F32))

    _for_each_key_tile(i, attend)
    out = [acc_ref[c, r * tq:(r + 1) * tq, :] / l_ref[c, r * tq:(r + 1) * tq, :HEAD_DIM]
           for c in chains for r in range(MOBA_GROUP)]
    o_ref[...] = jnp.concatenate(out, axis=1).astype(BF16)


def _moba_prompt(q, km_rep, kt, vh, slopes):
    b, nb = kt.shape[0], kt.shape[1]
    seq_len = nb * MOBA_BLOCK
    tq = MOBA_BLOCK
    gw = MOBA_GROUP * HEAD_DIM
    kps = MOBA_KV_PER_STEP
    rows = MOBA_GROUP * tq
    return pl.pallas_call(
        _moba_prompt_kernel,
        grid=(b, MOBA_KV_HEADS // kps, nb),
        in_specs=[pl.BlockSpec(memory_space=pltpu.SMEM),
                  pl.BlockSpec((tq, kps * gw), lambda bi, g, i: (bi * nb + i, g)),
                  pl.BlockSpec((1, kps, gw, LANES), lambda bi, g, i: (bi, g, 0, 0)),
                  pl.BlockSpec((1, nb, kps * HEAD_DIM, MOBA_BLOCK), lambda bi, g, i: (bi, 0, g, 0)),
                  pl.BlockSpec((1, kps, seq_len, HEAD_DIM), lambda bi, g, i: (bi, g, 0, 0))],
        out_specs=pl.BlockSpec((tq, kps * gw), lambda bi, g, i: (bi * nb + i, g)),
        out_shape=jax.ShapeDtypeStruct((b * seq_len, D_MODEL), BF16),
        scratch_shapes=[pltpu.VMEM((kps, rows, gw), BF16),
                        pltpu.VMEM((kps, tq, LANES), F32),
                        pltpu.VMEM((kps, rows, PROMPT_TILE_BLOCKS * tq), F32),
                        pltpu.VMEM((kps, rows, LANES), F32),
                        pltpu.VMEM((kps, rows, LANES), F32),
                        pltpu.VMEM((kps, rows, HEAD_DIM), F32),
                        pltpu.VMEM((kps, rows, PROMPT_TILE_BLOCKS * tq), BF16)],
        compiler_params=_params("parallel", "parallel", "arbitrary"), name="moba_prompt",
    )(slopes, q, km_rep, kt, vh)


def _lambda_full(lq1_ref, lk1_ref, lq2_ref, lk2_ref, lambda_init):
    a = jnp.sum(lq1_ref[...] * lk1_ref[...], axis=1, keepdims=True)
    b = jnp.sum(lq2_ref[...] * lk2_ref[...], axis=1, keepdims=True)
    return jnp.exp(a) - jnp.exp(b) + lambda_init


def _subln(o, w, lambda_init):
    return _rms(o, w, SUBLN_EPS) * (1.0 - lambda_init)


def _diff_prompt_kernel(slopes_ref, q_ref, k_ref, v_ref, lq1_ref, lk1_ref, lq2_ref, lk2_ref,
                        sw_ref, o_ref, qz_ref, bias_ref, m_ref, l_ref, acc_ref, *, tq, lambda_init):
    hp = pl.program_id(1)
    i = pl.program_id(2)
    heads = range(DIFF_HEADS_PER_STEP)
    lanes = [slice(c * DIFF_HW, (c + 1) * DIFF_HW) for c in heads]
    slope = [slopes_ref[hp * DIFF_HEADS_PER_STEP + c] * LOG2E for c in heads]
    everything = slice(None)
    for c in heads:
        qs = q_ref[:, lanes[c]] * (SCALE * LOG2E)
        qlane = lax.shift_right_logical(lax.broadcasted_iota(jnp.int32, qs.shape, 1), HEAD_DIM_LOG2)
        for mp in range(2):
            qz_ref[c, mp * tq:(mp + 1) * tq, :] = jnp.where(qlane == mp, qs, 0.0).astype(BF16)
    _init_softmax_state(m_ref, l_ref, acc_ref)

    def in_tile_offsets(width):
        rows2 = lax.broadcasted_iota(jnp.int32, (2 * tq, width), 0)
        return (lax.broadcasted_iota(jnp.int32, (2 * tq, width), 1)
                - jnp.where(rows2 >= tq, rows2 - tq, rows2))

    @pl.when(i == 0)
    def _():
        rel = in_tile_offsets(PROMPT_TILE_BLOCKS * tq).astype(F32)
        for c in heads:
            bias_ref[c] = slope[c] * rel

    def attend(first, n, own_last):
        width = n * tq
        ks = pl.ds(pl.multiple_of(first * tq, tq), width)
        for c in heads:
            s = _nt_dot(qz_ref[c], k_ref[ks, lanes[c]]) + bias_ref[c, :, :width]
            if own_last:
                s = jnp.where(in_tile_offsets(width) <= (n - 1) * tq, s, NEG_INF)
            p, alpha = _softmax_update(s, m_ref.at[c], l_ref.at[c], everything,
                                       shift=slope[c] * ((first - i) * tq).astype(F32), base2=True)
            acc_ref[c] = acc_ref[c] * alpha + jnp.dot(p.astype(BF16), v_ref[ks, lanes[c]],
                                                      preferred_element_type=F32)

    _for_each_key_tile(i, attend)
    lam = _lambda_full(lq1_ref, lk1_ref, lq2_ref, lk2_ref, lambda_init)
    for c in heads:
        o1 = acc_ref[c, :tq, :] / l_ref[c, :tq, :]
        o2 = acc_ref[c, tq:, :] / l_ref[c, tq:, :]
        o_ref[:, lanes[c]] = _subln(o1 - lam * o2, sw_ref[...], lambda_init).astype(BF16)


def _diff_prompt(q, kb, vb, lams, subln_w, slopes, *, batch, lambda_init):
    n = q.shape[0]
    seq_len = n // batch
    tq = 256
    nq = seq_len // tq
    lam_specs = [_resident((1, HEAD_DIM))] * 4
    hps = DIFF_HEADS_PER_STEP
    gw = hps * DIFF_HW
    return pl.pallas_call(
        functools.partial(_diff_prompt_kernel, tq=tq, lambda_init=lambda_init),
        grid=(batch, DIFF_HEADS // hps, nq),
        in_specs=[pl.BlockSpec(memory_space=pltpu.SMEM),
                  pl.BlockSpec((tq, gw), lambda bi, h, i: (bi * nq + i, h)),
                  pl.BlockSpec((seq_len, gw), lambda bi, h, i: (bi, h)),
                  pl.BlockSpec((seq_len, gw), lambda bi, h, i: (bi, h)),
                  *lam_specs, _resident((1, DIFF_HW))],
        out_specs=pl.BlockSpec((tq, gw), lambda bi, h, i: (bi * nq + i, h)),
        out_shape=jax.ShapeDtypeStruct((n, D_MODEL), BF16),
        scratch_shapes=[pltpu.VMEM((hps, 2 * tq, DIFF_HW), BF16),
                        pltpu.VMEM((hps, 2 * tq, PROMPT_TILE_BLOCKS * tq), F32),
                        pltpu.VMEM((hps, 2 * tq, LANES), F32),
                        pltpu.VMEM((hps, 2 * tq, LANES), F32),
                        pltpu.VMEM((hps, 2 * tq, DIFF_HW), F32)],
        compiler_params=_params("parallel", "parallel", "arbitrary"), name="diff_prompt",
    )(slopes, q, kb, vb, *lams, subln_w)


def _page_specs(block_shape, pages_per_step, page_offset):
    def spec(t):
        return pl.BlockSpec((1,) + block_shape,
                            lambda b, s, pt: (pt[b, s * pages_per_step + t] + page_offset, 0, 0))
    return [spec(t) for t in range(pages_per_step)]


def _moba_decode_kernel(pt_ref, qs_ref, qf_ref, slope_ref, qi_ref, knew_ref, vnew_ref,
                        *refs, pps, past_len):
    k_refs, v_refs = refs[:pps], refs[pps:2 * pps]
    o_ref, s_ref, km_ref, bmax_ref, sel_ref, m_ref, l_ref, acc_ref = refs[2 * pps:]
    phase = pl.program_id(1)
    step = pl.program_id(2)
    n_rows = qs_ref.shape[1]
    chunk = pps * PAGE_SIZE
    ppb = MOBA_BLOCK // PAGE_SIZE
    blocks_per_step = pps // ppb
    lane = lax.broadcasted_iota(jnp.int32, (n_rows, LANES), 1)
    shift = slope_ref[...] * (step * chunk - past_len).astype(F32)

    qh, ql = _split_bf16(qs_ref[0])
    q2 = jnp.concatenate([qh, ql], axis=0)

    def qk(kt32):
        kh, kl = _split_bf16(kt32)
        s2 = jnp.dot(q2, kh, preferred_element_type=F32)
        return s2[:n_rows] + s2[n_rows:] + jnp.dot(qh, kl, preferred_element_type=F32)

    def pv(p, vt32):
        ph, plo = _split_bf16(p)
        vh, vl = _split_bf16(vt32)
        o2 = _nt_dot(jnp.concatenate([ph, plo], axis=0), vh)
        return o2[:n_rows] + o2[n_rows:] + _nt_dot(ph, vl)

    def logits(kt32, n):
        tok = lax.broadcasted_iota(jnp.int32, (1, n), 1).astype(F32)
        return qk(kt32) + slope_ref[...] * (tok - qi_ref[...]), tok

    @pl.when(phase == 0)
    def _():
        @pl.when(step == 0)
        def _():
            km_ref[...] = jnp.zeros(km_ref.shape, F32)
            bmax_ref[...] = jnp.full(bmax_ref.shape, -jnp.inf, F32)

        kt = jnp.concatenate([r[0] for r in k_refs], axis=1)
        s, _ = logits(kt, chunk)
        s_ref[step] = s
        km_lane = lax.broadcasted_iota(jnp.int32, km_ref.shape, 1)
        km, bmax = km_ref[...], bmax_ref[...]
        for t in range(blocks_per_step):
            blk = step * blocks_per_step + t
            tot = sum(k_refs[t * ppb + u][0] for u in range(ppb))
            km = jnp.where(km_lane == blk, jnp.sum(tot, axis=1, keepdims=True) / MOBA_BLOCK, km)
            top = jnp.max(s[:, t * MOBA_BLOCK:(t + 1) * MOBA_BLOCK], axis=1, keepdims=True)
            bmax = jnp.where(lane == blk, top + shift, bmax)
        km_ref[...] = km
        bmax_ref[...] = bmax

    @pl.when(phase == 1)
    def _():
        @pl.when(step == 0)
        def _():
            gr = _dot_f32(qf_ref[0], km_ref[...])
            n_blocks = past_len // MOBA_BLOCK
            rpg = n_rows // MOBA_KV_HEADS
            dq = rpg // MOBA_GROUP
            for g in range(MOBA_KV_HEADS):
                gs = sum(gr[g * rpg + r * dq:g * rpg + (r + 1) * dq] for r in range(MOBA_GROUP))
                sel = _top3_mask(gs, lane[:dq], n_blocks)
                sel_ref[g * rpg:(g + 1) * rpg, :] = jnp.concatenate([sel] * MOBA_GROUP, axis=0)
            m_past = jnp.max(jnp.where(sel_ref[...] > 0.0, bmax_ref[...], -jnp.inf), axis=1,
                             keepdims=True)
            s_new, tok_new = logits(knew_ref[0], PAGE_SIZE)
            s_new = jnp.where(tok_new <= qi_ref[...], s_new, NEG_INF)
            m_fin = jnp.maximum(m_past, jnp.max(s_new, axis=1, keepdims=True))
            p_new = jnp.exp(s_new - m_fin)
            m_ref[...] = jnp.broadcast_to(m_fin, m_ref.shape)
            l_ref[...] = jnp.broadcast_to(jnp.sum(p_new, axis=1, keepdims=True), l_ref.shape)
            acc_ref[...] = pv(p_new, vnew_ref[0])

        vt = jnp.concatenate([r[0] for r in v_refs], axis=1)
        blk = lax.broadcasted_iota(jnp.int32, (LANES, chunk), 0)
        key = step * chunk + lax.broadcasted_iota(jnp.int32, (LANES, chunk), 1)
        expand = jnp.where(blk == lax.shift_right_logical(key, MOBA_BLOCK_LOG2), 1.0,
                           0.0).astype(BF16)
        ok = jnp.dot(sel_ref[...].astype(BF16), expand, preferred_element_type=F32) > 0.5
        p = jnp.where(ok, jnp.exp(s_ref[step] - _widen(m_ref[...] - shift, chunk)), 0.0)
        l_ref[...] += jnp.sum(p, axis=1, keepdims=True)
        acc_ref[...] += pv(p, vt)

        @pl.when(step == pl.num_programs(2) - 1)
        def _():
            o_ref[0] = acc_ref[...] / _widen(l_ref[...], acc_ref.shape[1])


def _moba_decode(q_scaled, q_plain, slope_rows, qi_rows, knew_t, vnew_t, k_pool_t, v_pool_t,
                 page_table, page_offset):
    db, n_pages = page_table.shape
    n_rows, width = q_scaled.shape[1], q_scaled.shape[2]
    pps = 16
    n_steps = n_pages // pps
    per_seq = lambda shape: pl.BlockSpec((1,) + shape, lambda b, ph, s, pt: (b, 0, 0))
    const = lambda shape: pl.BlockSpec(shape, lambda b, ph, s, pt: (0, 0))

    def page_spec(t, for_keys):
        def index(b, ph, s, pt):
            at = s * (1 - ph) + (n_steps - 1) * ph if for_keys else s * ph
            return (pt[b, at * pps + t] + page_offset, 0, 0)
        return pl.BlockSpec((1, width, PAGE_SIZE), index)

    grid_spec = pltpu.PrefetchScalarGridSpec(
        num_scalar_prefetch=1, grid=(db, 2, n_steps),
        in_specs=[per_seq((n_rows, width)), per_seq((n_rows, width)),
                  const((n_rows, 1)), const((n_rows, 1)),
                  per_seq((width, PAGE_SIZE)), per_seq((width, PAGE_SIZE)),
                  *[page_spec(t, True) for t in range(pps)],
                  *[page_spec(t, False) for t in range(pps)]],
        out_specs=per_seq((n_rows, width)),
        scratch_shapes=[pltpu.VMEM((n_steps, n_rows, pps * PAGE_SIZE), F32),
                        pltpu.VMEM((width, LANES), F32), pltpu.VMEM((n_rows, LANES), F32),
                        pltpu.VMEM((n_rows, LANES), F32), pltpu.VMEM((n_rows, LANES), F32),
                        pltpu.VMEM((n_rows, LANES), F32), pltpu.VMEM((n_rows, width), F32)])
    return pl.pallas_call(
        functools.partial(_moba_decode_kernel, pps=pps, past_len=n_pages * PAGE_SIZE),
        grid_spec=grid_spec, out_shape=jax.ShapeDtypeStruct((db, n_rows, width), F32),
        compiler_params=_params("parallel", "arbitrary", "arbitrary"), name="moba_decode",
    )(page_table, q_scaled, q_plain, slope_rows, qi_rows, knew_t, vnew_t,
      *([k_pool_t] * pps), *([v_pool_t] * pps))


def _diff_decode_kernel(pt_ref, qx_ref, slope_ref, qi_ref, head_ref, knew_ref, vnew_ref,
                        lq1_ref, lk1_ref, lq2_ref, lk2_ref, sw_ref, *refs, pps, past_len, lambda_init):
    k_refs, v_refs = refs[:pps], refs[pps:2 * pps]
    o_ref, bias_ref, m_ref, l_ref, acc_ref = refs[2 * pps:]
    step = pl.program_id(1)
    n_rows = qx_ref.shape[1]
    chunk = pps * PAGE_SIZE
    everything = slice(None)
    slope2 = slope_ref[...] * LOG2E

    def alibi_bias(n_cols):
        col = lax.broadcasted_iota(jnp.int32, (n_rows, n_cols), 1)
        tok = lax.shift_right_logical(col, DIFF_HEADS_LOG2)
        own_head = (col & (DIFF_HEADS - 1)) == head_ref[...]
        tokf = tok.astype(F32)
        return jnp.where(own_head, slope2 * (tokf - qi_ref[...]), NEG_INF), tokf

    @pl.when(step == 0)
    def _():
        _init_softmax_state(m_ref, l_ref, acc_ref)
        bias_ref[...] = alibi_bias(chunk * DIFF_HEADS)[0]

    def update(k32, v32, bias, shift):
        s = _nt_dot(qx_ref[0], k32.astype(BF16)) + bias
        p, alpha = _softmax_update(s, m_ref, l_ref, everything, shift=shift, base2=True)
        acc_ref[...] = acc_ref[...] * alpha + jnp.dot(p.astype(BF16), v32.astype(BF16),
                                                      preferred_element_type=F32)

    kc = jnp.concatenate([r[0] for r in k_refs], axis=0)
    vc = jnp.concatenate([r[0] for r in v_refs], axis=0)
    update(kc, vc, bias_ref[...], slope2 * (step * chunk - past_len).astype(F32))

    @pl.when(step == pl.num_programs(1) - 1)
    def _():
        bias_new, tokf = alibi_bias(PAGE_SIZE * DIFF_HEADS)
        update(knew_ref[0], vnew_ref[0], jnp.where(tokf <= qi_ref[...], bias_new, NEG_INF), 0.0)
        o = acc_ref[...] / l_ref[...]
        dq = n_rows // (2 * DIFF_HEADS)
        lam = _lambda_full(lq1_ref, lk1_ref, lq2_ref, lk2_ref, lambda_init)
        outs = []
        for hd in range(DIFF_HEADS):
            o1 = o[(2 * hd) * dq:(2 * hd + 1) * dq]
            o2 = o[(2 * hd + 1) * dq:(2 * hd + 2) * dq]
            outs.append(_subln(o1 - lam * o2, sw_ref[...], lambda_init))
        o_ref[0] = jnp.concatenate(outs, axis=1)


def _diff_decode(qx, slope_rows, qi_rows, head_rows, knew, vnew, lams, subln_w, k_pool, v_pool,
                 page_table, page_offset, *, lambda_init):
    db, n_pages = page_table.shape
    n_rows = qx.shape[1]
    pps = 8
    page_rows = PAGE_SIZE * DIFF_HEADS
    dq = n_rows // (2 * DIFF_HEADS)
    per_seq = lambda shape: pl.BlockSpec((1,) + shape, lambda b, s, pt: (b, 0, 0))
    const = lambda shape: pl.BlockSpec(shape, lambda b, s, pt: (0, 0))
    grid_spec = pltpu.PrefetchScalarGridSpec(
        num_scalar_prefetch=1, grid=(db, n_pages // pps),
        in_specs=[per_seq((n_rows, DIFF_HW)), const((n_rows, 1)), const((n_rows, 1)),
                  const((n_rows, 1)), per_seq((page_rows, DIFF_HW)), per_seq((page_rows, DIFF_HW)),
                  *([const((1, HEAD_DIM))] * 4), const((1, DIFF_HW)),
                  *(_page_specs((page_rows, DIFF_HW), pps, page_offset) * 2)],
        out_specs=per_seq((dq, D_MODEL)),
        scratch_shapes=[pltpu.VMEM((n_rows, pps * page_rows), F32),
                        pltpu.VMEM((n_rows, LANES), F32), pltpu.VMEM((n_rows, LANES), F32),
                        pltpu.VMEM((n_rows, DIFF_HW), F32)])
    return pl.pallas_call(
        functools.partial(_diff_decode_kernel, pps=pps, past_len=n_pages * PAGE_SIZE,
                          lambda_init=lambda_init),
        grid_spec=grid_spec, out_shape=jax.ShapeDtypeStruct((db, dq, D_MODEL), F32),
        compiler_params=_params("parallel", "arbitrary"), name="diff_decode",
    )(page_table, qx, slope_rows, qi_rows, head_rows, knew, vnew, *lams, subln_w,
      *([k_pool] * pps), *([v_pool] * pps))


def _router_logits(t, wr_hi_ref, wr_lo_ref, br_ref):
    t_hi, t_lo = _split_bf16(t)
    return (jnp.dot(t_hi, wr_hi_ref[...], preferred_element_type=F32)
            + jnp.dot(t_lo, wr_hi_ref[...], preferred_element_type=F32)
            + jnp.dot(t_hi, wr_lo_ref[...], preferred_element_type=F32)) + br_ref[...]


def _router_combine(logits):
    lane = lax.broadcasted_iota(jnp.int32, logits.shape, 1)
    width = logits.shape[1]
    first = lambda hit: jnp.min(jnp.where(hit, lane, width), axis=1, keepdims=True)
    is_g = lane < N_GROUPS
    gl = jnp.where(is_g, logits, -jnp.inf)
    g_max = jnp.max(gl, axis=1, keepdims=True)
    g_sel = first(gl == g_max)
    g_w = 1.0 / jnp.sum(jnp.where(is_g, jnp.exp(logits - g_max), 0.0), axis=1, keepdims=True)
    e_lo = N_GROUPS + g_sel * EXPERTS_PER_GROUP
    el = jnp.where((lane >= e_lo) & (lane < e_lo + EXPERTS_PER_GROUP), logits, -jnp.inf)
    v1 = jnp.max(el, axis=1, keepdims=True)
    i1 = first(el == v1)
    el2 = jnp.where(lane == i1, -jnp.inf, el)
    v2 = jnp.max(el2, axis=1, keepdims=True)
    i2 = first(el2 == v2)
    e2 = jnp.exp(v2 - v1)
    w1 = 1.0 / (1.0 + e2)
    w2 = e2 / (1.0 + e2)
    return jnp.where(lane == i1, g_w * w1, 0.0) + jnp.where(lane == i2, g_w * w2, 0.0)


def _wo_moe_kernel(x_ref, a_ref, wo_ref, nw_ref, wr_hi_ref, wr_lo_ref, br_ref, wgu_ref, wd_ref,
                   fw_ref, o_ref, hid_ref, *, final):
    y1 = x_ref[...] + jnp.dot(a_ref[...], wo_ref[...], preferred_element_type=F32)
    t = _rms(y1, nw_ref[...], NORM_EPS)
    combine = _router_combine(_router_logits(t, wr_hi_ref, wr_lo_ref, br_ref))
    tb = t.astype(BF16)
    for e in range(N_EXPERTS):
        gu = jnp.dot(tb, wgu_ref[e], preferred_element_type=F32)
        gate, up = gu[:, :D_EXPERT], gu[:, D_EXPERT:]
        c = combine[:, N_GROUPS + e:N_GROUPS + e + 1]
        hid = gate * (1.0 / (1.0 + jnp.exp(-gate))) * up * c
        hid_ref[:, e * D_EXPERT:(e + 1) * D_EXPERT] = hid.astype(BF16)
    y2 = y1 + jnp.dot(hid_ref[...], wd_ref[...], preferred_element_type=F32)
    if final:
        y2 = _rms(y2, fw_ref[...], NORM_EPS)
    o_ref[...] = y2


def _wo_moe(x, attn, wo_bf, nw, wr_hi, wr_lo, br, wgu_bf, wd_bf, fw, *, final):
    n = x.shape[0]
    tm = min(512, n)
    row = lambda t: (t, 0)
    return pl.pallas_call(
        functools.partial(_wo_moe_kernel, final=final),
        grid=(n // tm,),
        in_specs=[pl.BlockSpec((tm, D_MODEL), row), pl.BlockSpec((tm, D_MODEL), row),
                  _resident(wo_bf.shape), _resident((1, D_MODEL)),
                  _resident(wr_hi.shape), _resident(wr_lo.shape), _resident(br.shape),
                  _resident(wgu_bf.shape), _resident(wd_bf.shape), _resident((1, D_MODEL))],
        out_specs=pl.BlockSpec((tm, D_MODEL), row),
        out_shape=jax.ShapeDtypeStruct((n, D_MODEL), F32),
        scratch_shapes=[pltpu.VMEM((tm, N_EXPERTS * D_EXPERT), BF16)],
        compiler_params=_params("parallel"), name="wo_moe",
    )(x, attn, wo_bf, nw.reshape(1, D_MODEL), wr_hi, wr_lo, br, wgu_bf, wd_bf,
      fw.reshape(1, D_MODEL))


def _wo_moe_f32_kernel(x_ref, a_ref, wo_ref, nw_ref, wr_ref, br_ref, wg_ref, wu_ref, wd_ref,
                       fw_ref, o_ref, y1_ref, t_ref, comb_ref, acc_ref, *, final):
    e = pl.program_id(0)

    @pl.when(e == 0)
    def _():
        y1 = x_ref[...] + _dot_f32(a_ref[...], wo_ref[...])
        t = _rms(y1, nw_ref[...], NORM_EPS)
        y1_ref[...] = y1
        t_ref[...] = t
        comb_ref[...] = _router_combine(_dot_f32(t, wr_ref[...]) + br_ref[...])
        acc_ref[...] = jnp.zeros(acc_ref.shape, F32)

    t = t_ref[...]
    gate = _dot_f32(t, wg_ref[0])
    up = _dot_f32(t, wu_ref[0])
    lane = lax.broadcasted_iota(jnp.int32, comb_ref.shape, 1)
    c = jnp.sum(jnp.where(lane == N_GROUPS + e, comb_ref[...], 0.0), axis=1, keepdims=True)
    hid = gate * (1.0 / (1.0 + jnp.exp(-gate))) * up * c
    acc_ref[...] += _dot_f32(hid, wd_ref[0])

    @pl.when(e == pl.num_programs(0) - 1)
    def _():
        y2 = y1_ref[...] + acc_ref[...]
        if final:
            y2 = _rms(y2, fw_ref[...], NORM_EPS)
        o_ref[...] = y2


def _wo_moe_f32(x, attn, wo, nw, wr, br, w_gate, w_up, w_down, fw, *, final):
    n = x.shape[0]
    whole = lambda shape: pl.BlockSpec(shape, lambda e: (0,) * len(shape))
    per_expert = lambda shape: pl.BlockSpec((1,) + shape, lambda e: (e, 0, 0))
    return pl.pallas_call(
        functools.partial(_wo_moe_f32_kernel, final=final),
        grid=(N_EXPERTS,),
        in_specs=[whole((n, D_MODEL)), whole((n, D_MODEL)), whole(wo.shape), whole((1, D_MODEL)),
                  whole(wr.shape), whole(br.shape), per_expert((D_MODEL, D_EXPERT)),
                  per_expert((D_MODEL, D_EXPERT)), per_expert((D_EXPERT, D_MODEL)),
                  whole((1, D_MODEL))],
        out_specs=whole((n, D_MODEL)),
        out_shape=jax.ShapeDtypeStruct((n, D_MODEL), F32),
        scratch_shapes=[pltpu.VMEM((n, D_MODEL), F32), pltpu.VMEM((n, D_MODEL), F32),
                        pltpu.VMEM((n, ROUTER_LANES), F32), pltpu.VMEM((n, D_MODEL), F32)],
        compiler_params=_params("arbitrary"), name="wo_moe_f32",
    )(x, attn, wo, nw.reshape(1, D_MODEL), wr, br, w_gate, w_up, w_down, fw.reshape(1, D_MODEL))


def _moe_weights(w_group, b_group, w_expert, b_expert, w_gate, w_up, w_down):
    n_r = N_GROUPS + N_EXPERTS
    wr = jnp.pad(jnp.concatenate([w_group, w_expert], axis=1), ((0, 0), (0, ROUTER_LANES - n_r)))
    wr_hi = wr.astype(BF16)
    wr_lo = (wr - wr_hi.astype(F32)).astype(BF16)
    br = jnp.pad(jnp.concatenate([b_group, b_expert]), (0, ROUTER_LANES - n_r)).reshape(1, -1)
    wgu = jnp.concatenate([w_gate, w_up], axis=2).astype(BF16)
    wd = w_down.reshape(N_EXPERTS * D_EXPERT, D_MODEL).astype(BF16)
    return (wr_hi, wr_lo, br, wgu, wd), (wr, br)


def _block_diag_rows(q, n_slots):
    db, _, m, dq, w = q.shape
    eye = jnp.eye(n_slots, dtype=q.dtype)
    out = q[:, :, :, :, None, :] * eye[None, :, None, None, :, None]
    return out.reshape(db, n_slots * m * dq, n_slots * w)


def _pad_tokens(x):
    return jnp.pad(x, ((0, 0), (0, PAGE_SIZE - x.shape[1])) + ((0, 0),) * (x.ndim - 2))


def kernel(x_prompt, x_sample, cache_k_moba, cache_v_moba, cache_k_diff, cache_v_diff, page_table,
           norm_attn, norm_ffn, norm_final, moba_w_qkv, moba_w_o, diff_w_qkv, diff_w_o,
           diff_lambda_q1, diff_lambda_k1, diff_lambda_q2, diff_lambda_k2, diff_subln,
           moe_w_group, moe_b_group, moe_w_expert, moe_b_expert, moe_w_gate, moe_w_up, moe_w_down):
    batch, seq_len, _ = x_prompt.shape
    db, dq, _ = x_sample.shape
    n_pool = cache_k_moba.shape[1]
    depth = norm_attn.shape[0]
    nb = seq_len // MOBA_BLOCK
    slopes_moba = _alibi_slopes(MOBA_HEADS)
    slopes_diff = _alibi_slopes(DIFF_HEADS)
    qi_new = np.arange(dq, dtype=np.float32)

    moba_pool_t = lambda c: jnp.transpose(c, (0, 1, 3, 4, 2)).reshape(-1, MOBA_KVW, PAGE_SIZE)
    diff_pool = lambda c: c.reshape(-1, PAGE_SIZE * DIFF_HEADS, DIFF_HW)
    k_moba_t, v_moba_t = moba_pool_t(cache_k_moba), moba_pool_t(cache_v_moba)
    k_diff, v_diff = diff_pool(cache_k_diff), diff_pool(cache_v_diff)

    yp = x_prompt.reshape(batch * seq_len, D_MODEL)
    ys = x_sample.reshape(db * dq, D_MODEL)
    outs = {name: [] for name in ("km_p", "vm_p", "kd_p", "vd_p", "km_s", "vm_s", "kd_s", "vd_s")}
    for i in range(depth):
        j = i // 2
        final = i == depth - 1
        if i % 2 == 0:
            w_bf = moba_w_qkv[j].astype(BF16)
            wkt_bf = moba_w_qkv[j][:, D_MODEL:D_MODEL + MOBA_KVW].T.astype(BF16)
            qp, kp, vp, kt, vh, kmean = _norm_qkv(yp, norm_attn[i], w_bf, mode="moba",
                                                  seq_len=seq_len, wkt_bf=wkt_bf)
            km = kmean.reshape(batch, nb, MOBA_KV_HEADS, HEAD_DIM).transpose(0, 2, 3, 1)
            km_rep = jnp.pad(jnp.tile(km, (1, 1, MOBA_GROUP, 1)),
                             ((0, 0), (0, 0), (0, 0), (0, LANES - nb)))
            attn_p = _moba_prompt(qp, km_rep, kt, vh, jnp.asarray(slopes_moba))

            qs, ks, vs = _norm_qkv(ys, norm_attn[i], moba_w_qkv[j], mode="plain")
            q5 = qs.reshape(db, dq, MOBA_KV_HEADS, MOBA_GROUP, HEAD_DIM).transpose(0, 2, 3, 1, 4)
            qf = _block_diag_rows(q5, MOBA_KV_HEADS)
            slope_rows = np.repeat(slopes_moba, dq).reshape(-1, 1)
            qi_rows = np.tile(qi_new, MOBA_HEADS).reshape(-1, 1)
            new_t = lambda x: _pad_tokens(x.reshape(db, dq, MOBA_KVW)).transpose(0, 2, 1)
            o_rows = _moba_decode(qf * SCALE, qf, jnp.asarray(slope_rows), jnp.asarray(qi_rows),
                                  new_t(ks), new_t(vs), k_moba_t, v_moba_t, page_table, j * n_pool)
            o6 = o_rows.reshape(db, MOBA_KV_HEADS, MOBA_GROUP, dq, MOBA_KV_HEADS, HEAD_DIM)
            o5 = jnp.stack([o6[:, g, :, :, g, :] for g in range(MOBA_KV_HEADS)], axis=1)
            attn_s = o5.transpose(0, 3, 1, 2, 4).reshape(db * dq, D_MODEL)
            w_o = moba_w_o[j]
            kvh, kvd = MOBA_KV_HEADS, HEAD_DIM
            names = ("km_p", "vm_p", "km_s", "vm_s")
        else:
            lambda_init = 0.8 - 0.6 * math.exp(-0.3 * i)
            w_bf = diff_w_qkv[j].astype(BF16)
            lams = [a[j].reshape(1, HEAD_DIM) for a in
                    (diff_lambda_q1, diff_lambda_k1, diff_lambda_q2, diff_lambda_k2)]
            subln_w = diff_subln[j].reshape(1, DIFF_HW)
            qp, kp, vp, kb, vb = _norm_qkv(yp, norm_attn[i], w_bf, mode="diff")
            attn_p = _diff_prompt(qp, kb, vb, lams, subln_w, jnp.asarray(slopes_diff),
                                  batch=batch, lambda_init=lambda_init)

            qs, ks, vs = _norm_qkv(ys, norm_attn[i], diff_w_qkv[j], mode="plain")
            q5 = qs.reshape(db, dq, DIFF_HEADS, 2, 1, HEAD_DIM).transpose(0, 2, 3, 4, 1, 5)
            qx = _block_diag_rows(q5.reshape(db * DIFF_HEADS, 2, 1, dq, HEAD_DIM), 2)
            qx = (qx.reshape(db, 2 * DIFF_HEADS * dq, DIFF_HW) * (SCALE * LOG2E)).astype(BF16)
            slope_rows = np.repeat(slopes_diff, 2 * dq).reshape(-1, 1)
            qi_rows = np.tile(qi_new, 2 * DIFF_HEADS).reshape(-1, 1)
            head_rows = np.repeat(np.arange(DIFF_HEADS, dtype=np.int32), 2 * dq).reshape(-1, 1)
            new_flat = lambda x: _pad_tokens(x.reshape(db, dq, DIFF_HEADS, DIFF_HW)).reshape(
                db, PAGE_SIZE * DIFF_HEADS, DIFF_HW)
            attn_s = _diff_decode(qx, jnp.asarray(slope_rows), jnp.asarray(qi_rows),
                                  jnp.asarray(head_rows), new_flat(ks), new_flat(vs), lams, subln_w,
                                  k_diff, v_diff, page_table, j * n_pool,
                                  lambda_init=lambda_init).reshape(db * dq, D_MODEL)
            w_o = diff_w_o[j]
            kvh, kvd = DIFF_HEADS, DIFF_HW
            names = ("kd_p", "vd_p", "kd_s", "vd_s")
        outs[names[0]].append(kp.reshape(batch, seq_len, kvh, kvd))
        outs[names[1]].append(vp.reshape(batch, seq_len, kvh, kvd))
        outs[names[2]].append(ks.reshape(db, dq, kvh, kvd))
        outs[names[3]].append(vs.reshape(db, dq, kvh, kvd))

        moe_w, (wr, br) = _moe_weights(moe_w_group[i], moe_b_group[i], moe_w_expert[i],
                                       moe_b_expert[i], moe_w_gate[i], moe_w_up[i], moe_w_down[i])
        yp = _wo_moe(yp, attn_p, w_o.astype(BF16), norm_ffn[i], *moe_w, norm_final, final=final)
        ys = _wo_moe_f32(ys, attn_s, w_o, norm_ffn[i], wr, br, moe_w_gate[i], moe_w_up[i],
                         moe_w_down[i], norm_final, final=final)

    stack = lambda name: jnp.stack(outs[name])
    return (yp.reshape(batch, seq_len, D_MODEL), ys.reshape(db, dq, D_MODEL),
            stack("km_p"), stack("vm_p"), stack("kd_p"), stack("vd_p"),
            stack("km_s"), stack("vm_s"), stack("kd_s"), stack("vd_s"))
```

```python
import functools
import math

import jax
import jax.numpy as jnp
import numpy as np
from jax import lax
from jax.experimental import pallas as pl
from jax.experimental.pallas import tpu as pltpu

D_MODEL = 1024
HEAD_DIM = 64
HEAD_DIM_LOG2 = 6
MOBA_HEADS = D_MODEL // HEAD_DIM
MOBA_KV_HEADS = 4
MOBA_GROUP = MOBA_HEADS // MOBA_KV_HEADS
MOBA_BLOCK = 256
MOBA_BLOCK_LOG2 = 8
MOBA_TOPK = 3
MOBA_KVW = MOBA_KV_HEADS * HEAD_DIM
DIFF_HEADS = D_MODEL // (2 * HEAD_DIM)
DIFF_HEADS_LOG2 = 3
DIFF_HW = 2 * HEAD_DIM
N_GROUPS = 4
EXPERTS_PER_GROUP = 4
N_EXPERTS = N_GROUPS * EXPERTS_PER_GROUP
D_EXPERT = 256
PAGE_SIZE = 128
NORM_EPS = 1e-6
SUBLN_EPS = 1e-5
NEG_INF = -1e30
SCALE = HEAD_DIM ** -0.5
LOG2E = math.log2(math.e)

PROMPT_TILE_BLOCKS = 4
MOBA_KV_PER_STEP = 4
DIFF_SEQS_PER_STEP = 1
DIFF_HEADS_PER_STEP = 4
LANES = 128
ROUTER_LANES = LANES
VMEM_LIMIT = 56 * 1024 * 1024

F32 = jnp.float32
BF16 = jnp.bfloat16
NT_DIMS = (((1,), (1,)), ((), ()))


def _alibi_slopes(n_heads):
    start = 2.0 ** (-8.0 / n_heads)
    return np.array([start ** (i + 1) for i in range(n_heads)], dtype=np.float32)


def _params(*sem):
    return pltpu.CompilerParams(dimension_semantics=sem, vmem_limit_bytes=VMEM_LIMIT)


def _resident(shape):
    nd = len(shape)
    return pl.BlockSpec(shape, lambda *_: (0,) * nd, pipeline_mode=pl.Buffered(1))


def _rms(x, w, eps):
    ms = jnp.mean(x * x, axis=-1, keepdims=True)
    return x * lax.rsqrt(ms + eps) * w


def _split_bf16(x):
    hi = x.astype(BF16)
    return hi, (x - hi.astype(F32)).astype(BF16)


def _nt_dot(a, b):
    return lax.dot_general(a, b, NT_DIMS, preferred_element_type=F32)


def _dot_f32(a, b):
    return jnp.dot(a, b, precision=lax.Precision.HIGHEST, preferred_element_type=F32)


def _norm_qkv_kernel(x_ref, nw_ref, w_ref, *refs, mode, kvw, tm):
    h = _rms(x_ref[...], nw_ref[...], NORM_EPS)
    if w_ref.dtype == F32:
        qkv = _dot_f32(h, w_ref[...])
    else:
        h = h.astype(BF16)
        qkv = jnp.dot(h, w_ref[...], preferred_element_type=F32)
    q = qkv[:, :D_MODEL]
    k = qkv[:, D_MODEL:D_MODEL + kvw]
    v = qkv[:, D_MODEL + kvw:]
    if mode == "plain":
        q_ref, k_ref, v_ref = refs
    elif mode == "moba":
        wkt_ref, q_ref, k_ref, v_ref, kt_ref, vh_ref, km_ref = refs
        kt = _nt_dot(wkt_ref[...], h)
        vb = v.astype(BF16)
        for jj in range(tm // MOBA_BLOCK):
            sl = slice(jj * MOBA_BLOCK, (jj + 1) * MOBA_BLOCK)
            kt_ref[0, jj] = kt[:, sl].astype(BF16)
            km_ref[jj] = jnp.mean(k[sl], axis=0, keepdims=True)
        for g in range(MOBA_KV_HEADS):
            vh_ref[0, g] = vb[:, g * HEAD_DIM:(g + 1) * HEAD_DIM]
    else:
        q_ref, k_ref, v_ref, kb_ref, vb_ref = refs
        kb_ref[...] = k.astype(BF16)
        vb_ref[...] = v.astype(BF16)
    q_ref[...] = q
    k_ref[...] = k
    v_ref[...] = v


def _norm_qkv(x, nw, w, *, mode, seq_len=None, wkt_bf=None):
    n = x.shape[0]
    e = w.shape[1]
    kvw = (e - D_MODEL) // 2
    tm = min(512, n)
    nt = n // tm
    row = lambda t: (t, 0)
    in_specs = [pl.BlockSpec((tm, D_MODEL), row), _resident((1, D_MODEL)), _resident((D_MODEL, e))]
    args = [x, nw.reshape(1, D_MODEL), w]
    out_shape = [jax.ShapeDtypeStruct((n, D_MODEL), F32),
                 jax.ShapeDtypeStruct((n, kvw), F32),
                 jax.ShapeDtypeStruct((n, kvw), F32)]
    out_specs = [pl.BlockSpec((tm, D_MODEL), row), pl.BlockSpec((tm, kvw), row),
                 pl.BlockSpec((tm, kvw), row)]
    if mode == "moba":
        b = n // seq_len
        tps = seq_len // tm
        bpt = tm // MOBA_BLOCK
        nb = seq_len // MOBA_BLOCK
        in_specs.append(_resident((kvw, D_MODEL)))
        args.append(wkt_bf)
        out_shape += [jax.ShapeDtypeStruct((b, nb, kvw, MOBA_BLOCK), BF16),
                      jax.ShapeDtypeStruct((b, MOBA_KV_HEADS, seq_len, HEAD_DIM), BF16),
                      jax.ShapeDtypeStruct((b * nb, 1, kvw), F32)]
        out_specs += [pl.BlockSpec((1, bpt, kvw, MOBA_BLOCK), lambda t: (t // tps, t % tps, 0, 0)),
                      pl.BlockSpec((1, MOBA_KV_HEADS, tm, HEAD_DIM), lambda t: (t // tps, 0, t % tps, 0)),
                      pl.BlockSpec((bpt, 1, kvw), lambda t: (t, 0, 0))]
    elif mode == "diff":
        out_shape += [jax.ShapeDtypeStruct((n, kvw), BF16), jax.ShapeDtypeStruct((n, kvw), BF16)]
        out_specs += [pl.BlockSpec((tm, kvw), row), pl.BlockSpec((tm, kvw), row)]
    return pl.pallas_call(
        functools.partial(_norm_qkv_kernel, mode=mode, kvw=kvw, tm=tm),
        grid=(nt,), in_specs=in_specs, out_specs=out_specs, out_shape=out_shape,
        compiler_params=_params("parallel"), name="norm_qkv_" + mode,
    )(*args)


def _softmax_update(s, m_ref, l_ref, rows, shift=0.0, base2=False):
    width = s.shape[1]
    exp = jnp.exp2 if base2 else jnp.exp
    m_prev = m_ref[rows, :]
    m_new = jnp.maximum(m_prev, jnp.max(s, axis=1, keepdims=True) + shift)
    alpha = exp(m_prev - m_new)
    p = exp(s - jnp.concatenate([m_new - shift] * (width // LANES), axis=1))
    l_ref[rows, :] = alpha * l_ref[rows, :] + jnp.sum(p, axis=1, keepdims=True)
    m_ref[rows, :] = m_new
    return p, alpha


def _init_softmax_state(m_ref, l_ref, acc_ref):
    m_ref[...] = jnp.full(m_ref.shape, -jnp.inf, F32)
    l_ref[...] = jnp.zeros(l_ref.shape, F32)
    acc_ref[...] = jnp.zeros(acc_ref.shape, F32)


def _widen(x, width):
    return jnp.concatenate([x] * (width // LANES), axis=1)


def _for_each_key_tile(i, attend):
    assert PROMPT_TILE_BLOCKS == 4

    def full_tile(jj, carry):
        attend(jj * PROMPT_TILE_BLOCKS, PROMPT_TILE_BLOCKS, False)
        return carry

    lax.fori_loop(0, i // PROMPT_TILE_BLOCKS, full_tile, 0)
    left = i % PROMPT_TILE_BLOCKS

    @pl.when(left >= 2)
    def _():
        attend(i - left, 2, False)

    @pl.when(left % 2 == 0)
    def _():
        attend(i, 1, True)

    @pl.when(left % 2 == 1)
    def _():
        attend(i - 1, 2, True)


def _top3_mask(gate, lane, n_valid):
    gt = jnp.where(lane < n_valid, gate, -jnp.inf)
    sel = jnp.zeros(gate.shape, F32)
    for _ in range(MOBA_TOPK):
        mx = jnp.max(gt, axis=1, keepdims=True)
        idx = jnp.min(jnp.where(gt == mx, lane, gate.shape[1]), axis=1, keepdims=True)
        hit = (lane == idx) & (mx > -jnp.inf)
        sel = jnp.where(hit, 1.0, sel)
        gt = jnp.where(hit, -jnp.inf, gt)
    return sel


def _moba_prompt_kernel(slopes_ref, q_ref, km_ref, kt_ref, v_ref, o_ref,
                        qz_ref, sel_ref, bias_ref, m_ref, l_ref, acc_ref):
    tq = MOBA_BLOCK
    gw = MOBA_GROUP * HEAD_DIM
    gp = pl.program_id(1)
    i = pl.program_id(2)
    chains = range(MOBA_KV_PER_STEP)
    lane = lax.broadcasted_iota(jnp.int32, (tq, LANES), 1)
    slope2 = [[slopes_ref[(gp * MOBA_KV_PER_STEP + c) * MOBA_GROUP + r] * LOG2E
               for r in range(MOBA_GROUP)] for c in chains]
    for c in chains:
        q = q_ref[:, c * gw:(c + 1) * gw]
        gate = _dot_f32(q, km_ref[0, c])
        sel_ref[c] = _top3_mask(gate, lane, i)
        qs = q * (SCALE * LOG2E)
        qlane = lax.shift_right_logical(lax.broadcasted_iota(jnp.int32, qs.shape, 1), HEAD_DIM_LOG2)
        for r in range(MOBA_GROUP):
            qz_ref[c, r * tq:(r + 1) * tq, :] = jnp.where(qlane == r, qs, 0.0).astype(BF16)
    _init_softmax_state(m_ref, l_ref, acc_ref)

    @pl.when(i == 0)
    def _():
        width = PROMPT_TILE_BLOCKS * tq
        rel = (lax.broadcasted_iota(jnp.int32, (tq, width), 1)
               - lax.broadcasted_iota(jnp.int32, (tq, width), 0)).astype(F32)
        for c in chains:
            for r in range(MOBA_GROUP):
                bias_ref[c, r * tq:(r + 1) * tq, :] = slope2[c][r] * rel

    def attend(first, n, own_last):
        width = n * tq
        off = ((first - i) * tq).astype(F32)
        vs = pl.ds(pl.multiple_of(first * tq, tq), width)
        for c in chains:
            hd = slice(c * HEAD_DIM, (c + 1) * HEAD_DIM)
            kt = jnp.concatenate([kt_ref[0, first + u, hd, :] for u in range(n)], axis=1)
            kt4 = jnp.concatenate([kt] * MOBA_GROUP, axis=0)
            s = jnp.dot(qz_ref[c], kt4, preferred_element_type=F32)
            oks = []
            for u in range(n):
                if own_last and u == n - 1:
                    oks.append(lax.broadcasted_iota(jnp.int32, (tq, tq), 1)
                               <= lax.broadcasted_iota(jnp.int32, (tq, tq), 0))
                else:
                    oks.append(jnp.sum(jnp.where(lane == first + u, sel_ref[c], 0.0), axis=1,
                                       keepdims=True) > 0.0)
            for r in range(MOBA_GROUP):
                rows = slice(r * tq, (r + 1) * tq)
                sr = jnp.concatenate(
                    [jnp.where(oks[u], s[rows, u * tq:(u + 1) * tq]
                               + bias_ref[c, rows, u * tq:(u + 1) * tq], NEG_INF)
                     for u in range(n)], axis=1)
                p, alpha = _softmax_update(sr, m_ref.at[c], l_ref.at[c], rows,
                                           shift=slope2[c][r] * off, base2=True)
                acc_ref[c, rows, :] = (acc_ref[c, rows, :] * alpha[:, :HEAD_DIM]
                                       + jnp.dot(p.astype(BF16), v_ref[0, c, vs, :],
                                                 preferred_element_type=F32))

    _for_each_key_tile(i, attend)
    out = [acc_ref[c, r * tq:(r + 1) * tq, :] / l_ref[c, r * tq:(r + 1) * tq, :HEAD_DIM]
           for c in chains for r in range(MOBA_GROUP)]
    o_ref[...] = jnp.concatenate(out, axis=1).astype(BF16)


def _moba_prompt(q, km_rep, kt, vh, slopes):
    b, nb = kt.shape[0], kt.shape[1]
    seq_len = nb * MOBA_BLOCK
    tq = MOBA_BLOCK
    gw = MOBA_GROUP * HEAD_DIM
    kps = MOBA_KV_PER_STEP
    rows = MOBA_GROUP * tq
    return pl.pallas_call(
        _moba_prompt_kernel,
        grid=(b, MOBA_KV_HEADS // kps, nb),
        in_specs=[pl.BlockSpec(memory_space=pltpu.SMEM),
                  pl.BlockSpec((tq, kps * gw), lambda bi, g, i: (bi * nb + i, g)),
                  pl.BlockSpec((1, kps, gw, LANES), lambda bi, g, i: (bi, g, 0, 0)),
                  pl.BlockSpec((1, nb, kps * HEAD_DIM, MOBA_BLOCK), lambda bi, g, i: (bi, 0, g, 0)),
                  pl.BlockSpec((1, kps, seq_len, HEAD_DIM), lambda bi, g, i: (bi, g, 0, 0))],
        out_specs=pl.BlockSpec((tq, kps * gw), lambda bi, g, i: (bi * nb + i, g)),
        out_shape=jax.ShapeDtypeStruct((b * seq_len, D_MODEL), BF16),
        scratch_shapes=[pltpu.VMEM((kps, rows, gw), BF16),
                        pltpu.VMEM((kps, tq, LANES), F32),
                        pltpu.VMEM((kps, rows, PROMPT_TILE_BLOCKS * tq), F32),
                        pltpu.VMEM((kps, rows, LANES), F32),
                        pltpu.VMEM((kps, rows, LANES), F32),
                        pltpu.VMEM((kps, rows, HEAD_DIM), F32)],
        compiler_params=_params("parallel", "parallel", "arbitrary"), name="moba_prompt",
    )(slopes, q, km_rep, kt, vh)


def _lambda_full(lq1_ref, lk1_ref, lq2_ref, lk2_ref, lambda_init):
    a = jnp.sum(lq1_ref[...] * lk1_ref[...], axis=1, keepdims=True)
    b = jnp.sum(lq2_ref[...] * lk2_ref[...], axis=1, keepdims=True)
    return jnp.exp(a) - jnp.exp(b) + lambda_init


def _subln(o, w, lambda_init):
    return _rms(o, w, SUBLN_EPS) * (1.0 - lambda_init)


def _diff_prompt_kernel(slopes_ref, q_ref, k_ref, v_ref, lq1_ref, lk1_ref, lq2_ref, lk2_ref,
                        sw_ref, o_ref, qz_ref, bias_ref, m_ref, l_ref, acc_ref, *, tq, lambda_init):
    hp = pl.program_id(1)
    i = pl.program_id(2)
    heads = range(DIFF_HEADS_PER_STEP)
    lanes = [slice(c * DIFF_HW, (c + 1) * DIFF_HW) for c in heads]
    slope = [slopes_ref[hp * DIFF_HEADS_PER_STEP + c] * LOG2E for c in heads]
    everything = slice(None)
    for c in heads:
        qs = q_ref[:, lanes[c]] * (SCALE * LOG2E)
        qlane = lax.shift_right_logical(lax.broadcasted_iota(jnp.int32, qs.shape, 1), HEAD_DIM_LOG2)
        for mp in range(2):
            qz_ref[c, mp * tq:(mp + 1) * tq, :] = jnp.where(qlane == mp, qs, 0.0).astype(BF16)
    _init_softmax_state(m_ref, l_ref, acc_ref)

    def in_tile_offsets(width):
        rows2 = lax.broadcasted_iota(jnp.int32, (2 * tq, width), 0)
        return (lax.broadcasted_iota(jnp.int32, (2 * tq, width), 1)
                - jnp.where(rows2 >= tq, rows2 - tq, rows2))

    @pl.when(i == 0)
    def _():
        rel = in_tile_offsets(PROMPT_TILE_BLOCKS * tq).astype(F32)
        for c in heads:
            bias_ref[c] = slope[c] * rel

    def attend(first, n, own_last):
        width = n * tq
        ks = pl.ds(pl.multiple_of(first * tq, tq), width)
        for c in heads:
            s = _nt_dot(qz_ref[c], k_ref[ks, lanes[c]]) + bias_ref[c, :, :width]
            if own_last:
                s = jnp.where(in_tile_offsets(width) <= (n - 1) * tq, s, NEG_INF)
            p, alpha = _softmax_update(s, m_ref.at[c], l_ref.at[c], everything,
                                       shift=slope[c] * ((first - i) * tq).astype(F32), base2=True)
            acc_ref[c] = acc_ref[c] * alpha + jnp.dot(p.astype(BF16), v_ref[ks, lanes[c]],
                                                      preferred_element_type=F32)

    _for_each_key_tile(i, attend)
    lam = _lambda_full(lq1_ref, lk1_ref, lq2_ref, lk2_ref, lambda_init)
    for c in heads:
        o1 = acc_ref[c, :tq, :] / l_ref[c, :tq, :]
        o2 = acc_ref[c, tq:, :] / l_ref[c, tq:, :]
        o_ref[:, lanes[c]] = _subln(o1 - lam * o2, sw_ref[...], lambda_init).astype(BF16)


def _diff_prompt(q, kb, vb, lams, subln_w, slopes, *, batch, lambda_init):
    n = q.shape[0]
    seq_len = n // batch
    tq = 256
    nq = seq_len // tq
    lam_specs = [_resident((1, HEAD_DIM))] * 4
    hps = DIFF_HEADS_PER_STEP
    gw = hps * DIFF_HW
    return pl.pallas_call(
        functools.partial(_diff_prompt_kernel, tq=tq, lambda_init=lambda_init),
        grid=(batch, DIFF_HEADS // hps, nq),
        in_specs=[pl.BlockSpec(memory_space=pltpu.SMEM),
                  pl.BlockSpec((tq, gw), lambda bi, h, i: (bi * nq + i, h)),
                  pl.BlockSpec((seq_len, gw), lambda bi, h, i: (bi, h)),
                  pl.BlockSpec((seq_len, gw), lambda bi, h, i: (bi, h)),
                  *lam_specs, _resident((1, DIFF_HW))],
        out_specs=pl.BlockSpec((tq, gw), lambda bi, h, i: (bi * nq + i, h)),
        out_shape=jax.ShapeDtypeStruct((n, D_MODEL), BF16),
        scratch_shapes=[pltpu.VMEM((hps, 2 * tq, DIFF_HW), BF16),
                        pltpu.VMEM((hps, 2 * tq, PROMPT_TILE_BLOCKS * tq), F32),
                        pltpu.VMEM((hps, 2 * tq, LANES), F32),
                        pltpu.VMEM((hps, 2 * tq, LANES), F32),
                        pltpu.VMEM((hps, 2 * tq, DIFF_HW), F32)],
        compiler_params=_params("parallel", "parallel", "arbitrary"), name="diff_prompt",
    )(slopes, q, kb, vb, *lams, subln_w)


def _moba_decode_kernel(pt_ref, qs_ref, qf_ref, slope_ref, qi_ref, knew_ref, vnew_ref,
                        *refs, pps, past_len):
    k_refs, v_refs = refs[:pps], refs[pps:2 * pps]
    o_ref, s_ref, km_ref, bmax_ref, sel_ref, m_ref, l_ref, acc_ref = refs[2 * pps:]
    phase = pl.program_id(1)
    step = pl.program_id(2)
    n_rows = qs_ref.shape[1]
    chunk = pps * PAGE_SIZE
    ppb = MOBA_BLOCK // PAGE_SIZE
    blocks_per_step = pps // ppb
    lane = lax.broadcasted_iota(jnp.int32, (n_rows, LANES), 1)
    shift = slope_ref[...] * (step * chunk - past_len).astype(F32)

    qh, ql = _split_bf16(qs_ref[0])
    q2 = jnp.concatenate([qh, ql], axis=0)

    def qk(kt32):
        kh, kl = _split_bf16(kt32)
        s2 = jnp.dot(q2, kh, preferred_element_type=F32)
        return s2[:n_rows] + s2[n_rows:] + jnp.dot(qh, kl, preferred_element_type=F32)

    def pv(p, vt32):
        ph, plo = _split_bf16(p)
        vh, vl = _split_bf16(vt32)
        o2 = _nt_dot(jnp.concatenate([ph, plo], axis=0), vh)
        return o2[:n_rows] + o2[n_rows:] + _nt_dot(ph, vl)

    def logits(kt32, n):
        tok = lax.broadcasted_iota(jnp.int32, (1, n), 1).astype(F32)
        return qk(kt32) + slope_ref[...] * (tok - qi_ref[...]), tok

    @pl.when(phase == 0)
    def _():
        @pl.when(step == 0)
        def _():
            km_ref[...] = jnp.zeros(km_ref.shape, F32)
            bmax_ref[...] = jnp.full(bmax_ref.shape, -jnp.inf, F32)

        kt = jnp.concatenate([r[0] for r in k_refs], axis=1)
        s, _ = logits(kt, chunk)
        s_ref[step] = s
        km_lane = lax.broadcasted_iota(jnp.int32, km_ref.shape, 1)
        km, bmax = km_ref[...], bmax_ref[...]
        for t in range(blocks_per_step):
            blk = step * blocks_per_step + t
            tot = sum(k_refs[t * ppb + u][0] for u in range(ppb))
            km = jnp.where(km_lane == blk, jnp.sum(tot, axis=1, keepdims=True) / MOBA_BLOCK, km)
            top = jnp.max(s[:, t * MOBA_BLOCK:(t + 1) * MOBA_BLOCK], axis=1, keepdims=True)
            bmax = jnp.where(lane == blk, top + shift, bmax)
        km_ref[...] = km
        bmax_ref[...] = bmax

    @pl.when(phase == 1)
    def _():
        @pl.when(step == 0)
        def _():
            gr = _dot_f32(qf_ref[0], km_ref[...])
            n_blocks = past_len // MOBA_BLOCK
            rpg = n_rows // MOBA_KV_HEADS
            dq = rpg // MOBA_GROUP
            for g in range(MOBA_KV_HEADS):
                gs = sum(gr[g * rpg + r * dq:g * rpg + (r + 1) * dq] for r in range(MOBA_GROUP))
                sel = _top3_mask(gs, lane[:dq], n_blocks)
                sel_ref[g * rpg:(g + 1) * rpg, :] = jnp.concatenate([sel] * MOBA_GROUP, axis=0)
            m_past = jnp.max(jnp.where(sel_ref[...] > 0.0, bmax_ref[...], -jnp.inf), axis=1,
                             keepdims=True)
            s_new, tok_new = logits(knew_ref[0], PAGE_SIZE)
            s_new = jnp.where(tok_new <= qi_ref[...], s_new, NEG_INF)
            m_fin = jnp.maximum(m_past, jnp.max(s_new, axis=1, keepdims=True))
            p_new = jnp.exp(s_new - m_fin)
            m_ref[...] = jnp.broadcast_to(m_fin, m_ref.shape)
            l_ref[...] = jnp.broadcast_to(jnp.sum(p_new, axis=1, keepdims=True), l_ref.shape)
            acc_ref[...] = pv(p_new, vnew_ref[0])

        vt = jnp.concatenate([r[0] for r in v_refs], axis=1)
        blk = lax.broadcasted_iota(jnp.int32, (LANES, chunk), 0)
        key = step * chunk + lax.broadcasted_iota(jnp.int32, (LANES, chunk), 1)
        expand = jnp.where(blk == lax.shift_right_logical(key, MOBA_BLOCK_LOG2), 1.0,
                           0.0).astype(BF16)
        ok = jnp.dot(sel_ref[...].astype(BF16), expand, preferred_element_type=F32) > 0.5
        p = jnp.where(ok, jnp.exp(s_ref[step] - _widen(m_ref[...] - shift, chunk)), 0.0)
        l_ref[...] += jnp.sum(p, axis=1, keepdims=True)
        acc_ref[...] += pv(p, vt)

        @pl.when(step == pl.num_programs(2) - 1)
        def _():
            o_ref[0] = acc_ref[...] / _widen(l_ref[...], acc_ref.shape[1])


def _moba_decode(q_scaled, q_plain, slope_rows, qi_rows, knew_t, vnew_t, k_pool_t, v_pool_t,
                 page_table, page_offset):
    db, n_pages = page_table.shape
    n_rows, width = q_scaled.shape[1], q_scaled.shape[2]
    pps = 16
    n_steps = n_pages // pps
    per_seq = lambda shape: pl.BlockSpec((1,) + shape, lambda b, ph, s, pt: (b, 0, 0))
    const = lambda shape: pl.BlockSpec(shape, lambda b, ph, s, pt: (0, 0))

    def page_spec(t, for_keys):
        def index(b, ph, s, pt):
            at = s * (1 - ph) + (n_steps - 1) * ph if for_keys else s * ph
            return (pt[b, at * pps + t] + page_offset, 0, 0)
        return pl.BlockSpec((1, width, PAGE_SIZE), index)

    grid_spec = pltpu.PrefetchScalarGridSpec(
        num_scalar_prefetch=1, grid=(db, 2, n_steps),
        in_specs=[per_seq((n_rows, width)), per_seq((n_rows, width)),
                  const((n_rows, 1)), const((n_rows, 1)),
                  per_seq((width, PAGE_SIZE)), per_seq((width, PAGE_SIZE)),
                  *[page_spec(t, True) for t in range(pps)],
                  *[page_spec(t, False) for t in range(pps)]],
        out_specs=per_seq((n_rows, width)),
        scratch_shapes=[pltpu.VMEM((n_steps, n_rows, pps * PAGE_SIZE), F32),
                        pltpu.VMEM((width, LANES), F32), pltpu.VMEM((n_rows, LANES), F32),
                        pltpu.VMEM((n_rows, LANES), F32), pltpu.VMEM((n_rows, LANES), F32),
                        pltpu.VMEM((n_rows, LANES), F32), pltpu.VMEM((n_rows, width), F32)])
    return pl.pallas_call(
        functools.partial(_moba_decode_kernel, pps=pps, past_len=n_pages * PAGE_SIZE),
        grid_spec=grid_spec, out_shape=jax.ShapeDtypeStruct((db, n_rows, width), F32),
        compiler_params=_params("parallel", "arbitrary", "arbitrary"), name="moba_decode",
    )(page_table, q_scaled, q_plain, slope_rows, qi_rows, knew_t, vnew_t,
      *([k_pool_t] * pps), *([v_pool_t] * pps))


def _diff_decode_kernel(pt_ref, qx_ref, slope_ref, qi_ref, head_ref, knew_ref, vnew_ref,
                        lq1_ref, lk1_ref, lq2_ref, lk2_ref, sw_ref, *refs, pps, past_len, lambda_init):
    n_seq = qx_ref.shape[0]
    k_refs, v_refs = refs[:n_seq * pps], refs[n_seq * pps:2 * n_seq * pps]
    o_ref, bias_ref, m_ref, l_ref, acc_ref = refs[2 * n_seq * pps:]
    step = pl.program_id(1)
    n_rows = qx_ref.shape[1]
    chunk = pps * PAGE_SIZE
    everything = slice(None)
    slope2 = slope_ref[...] * LOG2E

    def alibi_bias(n_cols):
        col = lax.broadcasted_iota(jnp.int32, (n_rows, n_cols), 1)
        tok = lax.shift_right_logical(col, DIFF_HEADS_LOG2)
        own_head = (col & (DIFF_HEADS - 1)) == head_ref[...]
        tokf = tok.astype(F32)
        return jnp.where(own_head, slope2 * (tokf - qi_ref[...]), NEG_INF), tokf

    @pl.when(step == 0)
    def _():
        _init_softmax_state(m_ref, l_ref, acc_ref)
        bias_ref[...] = alibi_bias(chunk * DIFF_HEADS)[0]

    def update(c, k32, v32, bias, shift):
        s = _nt_dot(qx_ref[c], k32.astype(BF16)) + bias
        p, alpha = _softmax_update(s, m_ref.at[c], l_ref.at[c], everything, shift=shift,
                                   base2=True)
        acc_ref[c] = acc_ref[c] * alpha + jnp.dot(p.astype(BF16), v32.astype(BF16),
                                                  preferred_element_type=F32)

    for c in range(n_seq):
        kc = jnp.concatenate([r[0] for r in k_refs[c * pps:(c + 1) * pps]], axis=0)
        vc = jnp.concatenate([r[0] for r in v_refs[c * pps:(c + 1) * pps]], axis=0)
        update(c, kc, vc, bias_ref[...], slope2 * (step * chunk - past_len).astype(F32))

    @pl.when(step == pl.num_programs(1) - 1)
    def _():
        bias_new, tokf = alibi_bias(PAGE_SIZE * DIFF_HEADS)
        bias_new = jnp.where(tokf <= qi_ref[...], bias_new, NEG_INF)
        dq = n_rows // (2 * DIFF_HEADS)
        lam = _lambda_full(lq1_ref, lk1_ref, lq2_ref, lk2_ref, lambda_init)
        for c in range(n_seq):
            update(c, knew_ref[c], vnew_ref[c], bias_new, 0.0)
            o = acc_ref[c] / l_ref[c]
            outs = []
            for hd in range(DIFF_HEADS):
                o1 = o[(2 * hd) * dq:(2 * hd + 1) * dq]
                o2 = o[(2 * hd + 1) * dq:(2 * hd + 2) * dq]
                outs.append(_subln(o1 - lam * o2, sw_ref[...], lambda_init))
            o_ref[c] = jnp.concatenate(outs, axis=1)


def _diff_decode(qx, slope_rows, qi_rows, head_rows, knew, vnew, lams, subln_w, k_pool, v_pool,
                 page_table, page_offset, *, lambda_init):
    db, n_pages = page_table.shape
    n_rows = qx.shape[1]
    sps = DIFF_SEQS_PER_STEP
    pps = 8 // sps
    page_rows = PAGE_SIZE * DIFF_HEADS
    dq = n_rows // (2 * DIFF_HEADS)
    per_seq = lambda shape: pl.BlockSpec((sps,) + shape, lambda b, s, pt: (b, 0, 0))
    const = lambda shape: pl.BlockSpec(shape, lambda b, s, pt: (0, 0))

    def page_spec(c, t):
        return pl.BlockSpec((1, page_rows, DIFF_HW),
                            lambda b, s, pt: (pt[b * sps + c, s * pps + t] + page_offset, 0, 0))

    page_specs = [page_spec(c, t) for c in range(sps) for t in range(pps)]
    grid_spec = pltpu.PrefetchScalarGridSpec(
        num_scalar_prefetch=1, grid=(db // sps, n_pages // pps),
        in_specs=[per_seq((n_rows, DIFF_HW)), const((n_rows, 1)), const((n_rows, 1)),
                  const((n_rows, 1)), per_seq((page_rows, DIFF_HW)), per_seq((page_rows, DIFF_HW)),
                  *([const((1, HEAD_DIM))] * 4), const((1, DIFF_HW)), *(page_specs * 2)],
        out_specs=per_seq((dq, D_MODEL)),
        scratch_shapes=[pltpu.VMEM((n_rows, pps * page_rows), F32),
                        pltpu.VMEM((sps, n_rows, LANES), F32), pltpu.VMEM((sps, n_rows, LANES), F32),
                        pltpu.VMEM((sps, n_rows, DIFF_HW), F32)])
    return pl.pallas_call(
        functools.partial(_diff_decode_kernel, pps=pps, past_len=n_pages * PAGE_SIZE,
                          lambda_init=lambda_init),
        grid_spec=grid_spec, out_shape=jax.ShapeDtypeStruct((db, dq, D_MODEL), F32),
        compiler_params=_params("parallel", "arbitrary"), name="diff_decode",
    )(page_table, qx, slope_rows, qi_rows, head_rows, knew, vnew, *lams, subln_w,
      *([k_pool] * (sps * pps)), *([v_pool] * (sps * pps)))


def _router_logits(t, wr_hi_ref, wr_lo_ref, br_ref):
    t_hi, t_lo = _split_bf16(t)
    return (jnp.dot(t_hi, wr_hi_ref[...], preferred_element_type=F32)
            + jnp.dot(t_lo, wr_hi_ref[...], preferred_element_type=F32)
            + jnp.dot(t_hi, wr_lo_ref[...], preferred_element_type=F32)) + br_ref[...]


def _router_combine(logits):
    lane = lax.broadcasted_iota(jnp.int32, logits.shape, 1)
    width = logits.shape[1]
    first = lambda hit: jnp.min(jnp.where(hit, lane, width), axis=1, keepdims=True)
    is_g = lane < N_GROUPS
    gl = jnp.where(is_g, logits, -jnp.inf)
    g_max = jnp.max(gl, axis=1, keepdims=True)
    g_sel = first(gl == g_max)
    g_w = 1.0 / jnp.sum(jnp.where(is_g, jnp.exp(logits - g_max), 0.0), axis=1, keepdims=True)
    e_lo = N_GROUPS + g_sel * EXPERTS_PER_GROUP
    el = jnp.where((lane >= e_lo) & (lane < e_lo + EXPERTS_PER_GROUP), logits, -jnp.inf)
    v1 = jnp.max(el, axis=1, keepdims=True)
    i1 = first(el == v1)
    el2 = jnp.where(lane == i1, -jnp.inf, el)
    v2 = jnp.max(el2, axis=1, keepdims=True)
    i2 = first(el2 == v2)
    e2 = jnp.exp(v2 - v1)
    w1 = 1.0 / (1.0 + e2)
    w2 = e2 / (1.0 + e2)
    return jnp.where(lane == i1, g_w * w1, 0.0) + jnp.where(lane == i2, g_w * w2, 0.0)


def _wo_moe_kernel(x_ref, a_ref, wo_ref, nw_ref, wr_hi_ref, wr_lo_ref, br_ref, wgu_ref, wd_ref,
                   fw_ref, o_ref, hid_ref, *, final):
    y1 = x_ref[...] + jnp.dot(a_ref[...], wo_ref[...], preferred_element_type=F32)
    t = _rms(y1, nw_ref[...], NORM_EPS)
    combine = _router_combine(_router_logits(t, wr_hi_ref, wr_lo_ref, br_ref))
    tb = t.astype(BF16)
    for e in range(N_EXPERTS):
        gu = jnp.dot(tb, wgu_ref[e], preferred_element_type=F32)
        gate, up = gu[:, :D_EXPERT], gu[:, D_EXPERT:]
        c = combine[:, N_GROUPS + e:N_GROUPS + e + 1]
        hid = gate * (1.0 / (1.0 + jnp.exp(-gate))) * up * c
        hid_ref[:, e * D_EXPERT:(e + 1) * D_EXPERT] = hid.astype(BF16)
    y2 = y1 + jnp.dot(hid_ref[...], wd_ref[...], preferred_element_type=F32)
    if final:
        y2 = _rms(y2, fw_ref[...], NORM_EPS)
    o_ref[...] = y2


def _wo_moe(x, attn, wo_bf, nw, wr_hi, wr_lo, br, wgu_bf, wd_bf, fw, *, final):
    n = x.shape[0]
    tm = min(512, n)
    row = lambda t: (t, 0)
    return pl.pallas_call(
        functools.partial(_wo_moe_kernel, final=final),
        grid=(n // tm,),
        in_specs=[pl.BlockSpec((tm, D_MODEL), row), pl.BlockSpec((tm, D_MODEL), row),
                  _resident(wo_bf.shape), _resident((1, D_MODEL)),
                  _resident(wr_hi.shape), _resident(wr_lo.shape), _resident(br.shape),
                  _resident(wgu_bf.shape), _resident(wd_bf.shape), _resident((1, D_MODEL))],
        out_specs=pl.BlockSpec((tm, D_MODEL), row),
        out_shape=jax.ShapeDtypeStruct((n, D_MODEL), F32),
        scratch_shapes=[pltpu.VMEM((tm, N_EXPERTS * D_EXPERT), BF16)],
        compiler_params=_params("parallel"), name="wo_moe",
    )(x, attn, wo_bf, nw.reshape(1, D_MODEL), wr_hi, wr_lo, br, wgu_bf, wd_bf,
      fw.reshape(1, D_MODEL))


def _wo_moe_f32_kernel(x_ref, a_ref, wo_ref, nw_ref, wr_ref, br_ref, wg_ref, wu_ref, wd_ref,
                       fw_ref, o_ref, y1_ref, t_ref, comb_ref, acc_ref, *, final):
    e = pl.program_id(0)

    @pl.when(e == 0)
    def _():
        y1 = x_ref[...] + _dot_f32(a_ref[...], wo_ref[...])
        t = _rms(y1, nw_ref[...], NORM_EPS)
        y1_ref[...] = y1
        t_ref[...] = t
        comb_ref[...] = _router_combine(_dot_f32(t, wr_ref[...]) + br_ref[...])
        acc_ref[...] = jnp.zeros(acc_ref.shape, F32)

    t = t_ref[...]
    gate = _dot_f32(t, wg_ref[0])
    up = _dot_f32(t, wu_ref[0])
    lane = lax.broadcasted_iota(jnp.int32, comb_ref.shape, 1)
    c = jnp.sum(jnp.where(lane == N_GROUPS + e, comb_ref[...], 0.0), axis=1, keepdims=True)
    hid = gate * (1.0 / (1.0 + jnp.exp(-gate))) * up * c
    acc_ref[...] += _dot_f32(hid, wd_ref[0])

    @pl.when(e == pl.num_programs(0) - 1)
    def _():
        y2 = y1_ref[...] + acc_ref[...]
        if final:
            y2 = _rms(y2, fw_ref[...], NORM_EPS)
        o_ref[...] = y2


def _wo_moe_f32(x, attn, wo, nw, wr, br, w_gate, w_up, w_down, fw, *, final):
    n = x.shape[0]
    whole = lambda shape: pl.BlockSpec(shape, lambda e: (0,) * len(shape))
    per_expert = lambda shape: pl.BlockSpec((1,) + shape, lambda e: (e, 0, 0))
    return pl.pallas_call(
        functools.partial(_wo_moe_f32_kernel, final=final),
        grid=(N_EXPERTS,),
        in_specs=[whole((n, D_MODEL)), whole((n, D_MODEL)), whole(wo.shape), whole((1, D_MODEL)),
                  whole(wr.shape), whole(br.shape), per_expert((D_MODEL, D_EXPERT)),
                  per_expert((D_MODEL, D_EXPERT)), per_expert((D_EXPERT, D_MODEL)),
                  whole((1, D_MODEL))],
        out_specs=whole((n, D_MODEL)),
        out_shape=jax.ShapeDtypeStruct((n, D_MODEL), F32),
        scratch_shapes=[pltpu.VMEM((n, D_MODEL), F32), pltpu.VMEM((n, D_MODEL), F32),
                        pltpu.VMEM((n, ROUTER_LANES), F32), pltpu.VMEM((n, D_MODEL), F32)],
        compiler_params=_params("arbitrary"), name="wo_moe_f32",
    )(x, attn, wo, nw.reshape(1, D_MODEL), wr, br, w_gate, w_up, w_down, fw.reshape(1, D_MODEL))


def _moe_weights(w_group, b_group, w_expert, b_expert, w_gate, w_up, w_down):
    n_r = N_GROUPS + N_EXPERTS
    wr = jnp.pad(jnp.concatenate([w_group, w_expert], axis=1), ((0, 0), (0, ROUTER_LANES - n_r)))
    wr_hi = wr.astype(BF16)
    wr_lo = (wr - wr_hi.astype(F32)).astype(BF16)
    br = jnp.pad(jnp.concatenate([b_group, b_expert]), (0, ROUTER_LANES - n_r)).reshape(1, -1)
    wgu = jnp.concatenate([w_gate, w_up], axis=2).astype(BF16)
    wd = w_down.reshape(N_EXPERTS * D_EXPERT, D_MODEL).astype(BF16)
    return (wr_hi, wr_lo, br, wgu, wd), (wr, br)


def _block_diag_rows(q, n_slots):
    db, _, m, dq, w = q.shape
    eye = jnp.eye(n_slots, dtype=q.dtype)
    out = q[:, :, :, :, None, :] * eye[None, :, None, None, :, None]
    return out.reshape(db, n_slots * m * dq, n_slots * w)


def _pad_tokens(x):
    return jnp.pad(x, ((0, 0), (0, PAGE_SIZE - x.shape[1])) + ((0, 0),) * (x.ndim - 2))


def kernel(x_prompt, x_sample, cache_k_moba, cache_v_moba, cache_k_diff, cache_v_diff, page_table,
           norm_attn, norm_ffn, norm_final, moba_w_qkv, moba_w_o, diff_w_qkv, diff_w_o,
           diff_lambda_q1, diff_lambda_k1, diff_lambda_q2, diff_lambda_k2, diff_subln,
           moe_w_group, moe_b_group, moe_w_expert, moe_b_expert, moe_w_gate, moe_w_up, moe_w_down):
    batch, seq_len, _ = x_prompt.shape
    db, dq, _ = x_sample.shape
    n_pool = cache_k_moba.shape[1]
    depth = norm_attn.shape[0]
    nb = seq_len // MOBA_BLOCK
    slopes_moba = _alibi_slopes(MOBA_HEADS)
    slopes_diff = _alibi_slopes(DIFF_HEADS)
    qi_new = np.arange(dq, dtype=np.float32)

    moba_pool_t = lambda c: jnp.transpose(c, (0, 1, 3, 4, 2)).reshape(-1, MOBA_KVW, PAGE_SIZE)
    diff_pool = lambda c: c.reshape(-1, PAGE_SIZE * DIFF_HEADS, DIFF_HW)
    k_moba_t, v_moba_t = moba_pool_t(cache_k_moba), moba_pool_t(cache_v_moba)
    k_diff, v_diff = diff_pool(cache_k_diff), diff_pool(cache_v_diff)

    yp = x_prompt.reshape(batch * seq_len, D_MODEL)
    ys = x_sample.reshape(db * dq, D_MODEL)
    outs = {name: [] for name in ("km_p", "vm_p", "kd_p", "vd_p", "km_s", "vm_s", "kd_s", "vd_s")}
    for i in range(depth):
        j = i // 2
        final = i == depth - 1
        if i % 2 == 0:
            w_bf = moba_w_qkv[j].astype(BF16)
            wkt_bf = moba_w_qkv[j][:, D_MODEL:D_MODEL + MOBA_KVW].T.astype(BF16)
            qp, kp, vp, kt, vh, kmean = _norm_qkv(yp, norm_attn[i], w_bf, mode="moba",
                                                  seq_len=seq_len, wkt_bf=wkt_bf)
            km = kmean.reshape(batch, nb, MOBA_KV_HEADS, HEAD_DIM).transpose(0, 2, 3, 1)
            km_rep = jnp.pad(jnp.tile(km, (1, 1, MOBA_GROUP, 1)),
                             ((0, 0), (0, 0), (0, 0), (0, LANES - nb)))
            attn_p = _moba_prompt(qp, km_rep, kt, vh, jnp.asarray(slopes_moba))

            qs, ks, vs = _norm_qkv(ys, norm_attn[i], moba_w_qkv[j], mode="plain")
            q5 = qs.reshape(db, dq, MOBA_KV_HEADS, MOBA_GROUP, HEAD_DIM).transpose(0, 2, 3, 1, 4)
            qf = _block_diag_rows(q5, MOBA_KV_HEADS)
            slope_rows = np.repeat(slopes_moba, dq).reshape(-1, 1)
            qi_rows = np.tile(qi_new, MOBA_HEADS).reshape(-1, 1)
            new_t = lambda x: _pad_tokens(x.reshape(db, dq, MOBA_KVW)).transpose(0, 2, 1)
            o_rows = _moba_decode(qf * SCALE, qf, jnp.asarray(slope_rows), jnp.asarray(qi_rows),
                                  new_t(ks), new_t(vs), k_moba_t, v_moba_t, page_table, j * n_pool)
            o6 = o_rows.reshape(db, MOBA_KV_HEADS, MOBA_GROUP, dq, MOBA_KV_HEADS, HEAD_DIM)
            o5 = jnp.stack([o6[:, g, :, :, g, :] for g in range(MOBA_KV_HEADS)], axis=1)
            attn_s = o5.transpose(0, 3, 1, 2, 4).reshape(db * dq, D_MODEL)
            w_o = moba_w_o[j]
            kvh, kvd = MOBA_KV_HEADS, HEAD_DIM
            names = ("km_p", "vm_p", "km_s", "vm_s")
        else:
            lambda_init = 0.8 - 0.6 * math.exp(-0.3 * i)
            w_bf = diff_w_qkv[j].astype(BF16)
            lams = [a[j].reshape(1, HEAD_DIM) for a in
                    (diff_lambda_q1, diff_lambda_k1, diff_lambda_q2, diff_lambda_k2)]
            subln_w = diff_subln[j].reshape(1, DIFF_HW)
            qp, kp, vp, kb, vb = _norm_qkv(yp, norm_attn[i], w_bf, mode="diff")
            attn_p = _diff_prompt(qp, kb, vb, lams, subln_w, jnp.asarray(slopes_diff),
                                  batch=batch, lambda_init=lambda_init)

            qs, ks, vs = _norm_qkv(ys, norm_attn[i], diff_w_qkv[j], mode="plain")
            q5 = qs.reshape(db, dq, DIFF_HEADS, 2, 1, HEAD_DIM).transpose(0, 2, 3, 4, 1, 5)
            qx = _block_diag_rows(q5.reshape(db * DIFF_HEADS, 2, 1, dq, HEAD_DIM), 2)
            qx = (qx.reshape(db, 2 * DIFF_HEADS * dq, DIFF_HW) * (SCALE * LOG2E)).astype(BF16)
            slope_rows = np.repeat(slopes_diff, 2 * dq).reshape(-1, 1)
            qi_rows = np.tile(qi_new, 2 * DIFF_HEADS).reshape(-1, 1)
            head_rows = np.repeat(np.arange(DIFF_HEADS, dtype=np.int32), 2 * dq).reshape(-1, 1)
            new_flat = lambda x: _pad_tokens(x.reshape(db, dq, DIFF_HEADS, DIFF_HW)).reshape(
                db, PAGE_SIZE * DIFF_HEADS, DIFF_HW)
            attn_s = _diff_decode(qx, jnp.asarray(slope_rows), jnp.asarray(qi_rows),
                                  jnp.asarray(head_rows), new_flat(ks), new_flat(vs), lams, subln_w,
                                  k_diff, v_diff, page_table, j * n_pool,
                                  lambda_init=lambda_init).reshape(db * dq, D_MODEL)
            w_o = diff_w_o[j]
            kvh, kvd = DIFF_HEADS, DIFF_HW
            names = ("kd_p", "vd_p", "kd_s", "vd_s")
        outs[names[0]].append(kp.reshape(batch, seq_len, kvh, kvd))
        outs[names[1]].append(vp.reshape(batch, seq_len, kvh, kvd))
        outs[names[2]].append(ks.reshape(db, dq, kvh, kvd))
        outs[names[3]].append(vs.reshape(db, dq, kvh, kvd))

        moe_w, (wr, br) = _moe_weights(moe_w_group[i], moe_b_group[i], moe_w_expert[i],
                                       moe_b_expert[i], moe_w_gate[i], moe_w_up[i], moe_w_down[i])
        yp = _wo_moe(yp, attn_p, w_o.astype(BF16), norm_ffn[i], *moe_w, norm_final, final=final)
        ys = _wo_moe_f32(ys, attn_s, w_o, norm_ffn[i], wr, br, moe_w_gate[i], moe_w_up[i],
                         moe_w_down[i], norm_final, final=final)

    stack = lambda name: jnp.stack(outs[name])
    return (yp.reshape(batch, seq_len, D_MODEL), ys.reshape(db, dq, D_MODEL),
            stack("km_p"), stack("vm_p"), stack("kd_p"), stack("vd_p"),
            stack("km_s"), stack("vm_s"), stack("kd_s"), stack("vd_s"))
```

```python
import functools
import math

import jax
import jax.numpy as jnp
import numpy as np
from jax import lax
from jax.experimental import pallas as pl
from jax.experimental.pallas import tpu as pltpu

D_MODEL = 1024
HEAD_DIM = 64
HEAD_DIM_LOG2 = 6
MOBA_HEADS = D_MODEL // HEAD_DIM
MOBA_KV_HEADS = 4
MOBA_GROUP = MOBA_HEADS // MOBA_KV_HEADS
MOBA_BLOCK = 256
MOBA_BLOCK_LOG2 = 8
MOBA_TOPK = 3
MOBA_KVW = MOBA_KV_HEADS * HEAD_DIM
DIFF_HEADS = D_MODEL // (2 * HEAD_DIM)
DIFF_HEADS_LOG2 = 3
DIFF_HW = 2 * HEAD_DIM
N_GROUPS = 4
EXPERTS_PER_GROUP = 4
N_EXPERTS = N_GROUPS * EXPERTS_PER_GROUP
D_EXPERT = 256
PAGE_SIZE = 128
NORM_EPS = 1e-6
SUBLN_EPS = 1e-5
NEG_INF = -1e30
SCALE = HEAD_DIM ** -0.5
LOG2E = math.log2(math.e)

PROMPT_TILE_BLOCKS = 4
MOBA_KV_PER_STEP = 4
DIFF_SEQS_PER_STEP = 1
DIFF_HEADS_PER_STEP = 4
LANES = 128
ROUTER_LANES = LANES
VMEM_LIMIT = 56 * 1024 * 1024

F32 = jnp.float32
BF16 = jnp.bfloat16
NT_DIMS = (((1,), (1,)), ((), ()))


def _alibi_slopes(n_heads):
    start = 2.0 ** (-8.0 / n_heads)
    return np.array([start ** (i + 1) for i in range(n_heads)], dtype=np.float32)


def _params(*sem):
    return pltpu.CompilerParams(dimension_semantics=sem, vmem_limit_bytes=VMEM_LIMIT)


def _resident(shape):
    nd = len(shape)
    return pl.BlockSpec(shape, lambda *_: (0,) * nd, pipeline_mode=pl.Buffered(1))


def _rms(x, w, eps):
    ms = jnp.mean(x * x, axis=-1, keepdims=True)
    return x * lax.rsqrt(ms + eps) * w


def _split_bf16(x):
    hi = x.astype(BF16)
    return hi, (x - hi.astype(F32)).astype(BF16)


def _nt_dot(a, b):
    return lax.dot_general(a, b, NT_DIMS, preferred_element_type=F32)


def _dot_f32(a, b):
    return jnp.dot(a, b, precision=lax.Precision.HIGHEST, preferred_element_type=F32)


def _norm_qkv_kernel(x_ref, nw_ref, w_ref, *refs, mode, kvw, tm):
    h = _rms(x_ref[...], nw_ref[...], NORM_EPS)
    if w_ref.dtype == F32:
        qkv = _dot_f32(h, w_ref[...])
    else:
        h = h.astype(BF16)
        qkv = jnp.dot(h, w_ref[...], preferred_element_type=F32)
    q = qkv[:, :D_MODEL]
    k = qkv[:, D_MODEL:D_MODEL + kvw]
    v = qkv[:, D_MODEL + kvw:]
    if mode == "plain":
        q_ref, k_ref, v_ref = refs
    elif mode == "moba":
        wkt_ref, q_ref, k_ref, v_ref, kt_ref, vh_ref, km_ref = refs
        kt = _nt_dot(wkt_ref[...], h)
        vb = v.astype(BF16)
        for jj in range(tm // MOBA_BLOCK):
            sl = slice(jj * MOBA_BLOCK, (jj + 1) * MOBA_BLOCK)
            kt_ref[0, jj] = kt[:, sl].astype(BF16)
            km_ref[jj] = jnp.mean(k[sl], axis=0, keepdims=True)
        for g in range(MOBA_KV_HEADS):
            vh_ref[0, g] = vb[:, g * HEAD_DIM:(g + 1) * HEAD_DIM]
    else:
        q_ref, k_ref, v_ref, kb_ref, vb_ref = refs
        kb_ref[...] = k.astype(BF16)
        vb_ref[...] = v.astype(BF16)
    q_ref[...] = q
    k_ref[...] = k
    v_ref[...] = v


def _norm_qkv(x, nw, w, *, mode, seq_len=None, wkt_bf=None):
    n = x.shape[0]
    e = w.shape[1]
    kvw = (e - D_MODEL) // 2
    tm = min(512, n)
    nt = n // tm
    row = lambda t: (t, 0)
    in_specs = [pl.BlockSpec((tm, D_MODEL), row), _resident((1, D_MODEL)), _resident((D_MODEL, e))]
    args = [x, nw.reshape(1, D_MODEL), w]
    out_shape = [jax.ShapeDtypeStruct((n, D_MODEL), F32),
                 jax.ShapeDtypeStruct((n, kvw), F32),
                 jax.ShapeDtypeStruct((n, kvw), F32)]
    out_specs = [pl.BlockSpec((tm, D_MODEL), row), pl.BlockSpec((tm, kvw), row),
                 pl.BlockSpec((tm, kvw), row)]
    if mode == "moba":
        b = n // seq_len
        tps = seq_len // tm
        bpt = tm // MOBA_BLOCK
        nb = seq_len // MOBA_BLOCK
        in_specs.append(_resident((kvw, D_MODEL)))
        args.append(wkt_bf)
        out_shape += [jax.ShapeDtypeStruct((b, nb, kvw, MOBA_BLOCK), BF16),
                      jax.ShapeDtypeStruct((b, MOBA_KV_HEADS, seq_len, HEAD_DIM), BF16),
                      jax.ShapeDtypeStruct((b * nb, 1, kvw), F32)]
        out_specs += [pl.BlockSpec((1, bpt, kvw, MOBA_BLOCK), lambda t: (t // tps, t % tps, 0, 0)),
                      pl.BlockSpec((1, MOBA_KV_HEADS, tm, HEAD_DIM), lambda t: (t // tps, 0, t % tps, 0)),
                      pl.BlockSpec((bpt, 1, kvw), lambda t: (t, 0, 0))]
    elif mode == "diff":
        out_shape += [jax.ShapeDtypeStruct((n, kvw), BF16), jax.ShapeDtypeStruct((n, kvw), BF16)]
        out_specs += [pl.BlockSpec((tm, kvw), row), pl.BlockSpec((tm, kvw), row)]
    return pl.pallas_call(
        functools.partial(_norm_qkv_kernel, mode=mode, kvw=kvw, tm=tm),
        grid=(nt,), in_specs=in_specs, out_specs=out_specs, out_shape=out_shape,
        compiler_params=_params("parallel"), name="norm_qkv_" + mode,
    )(*args)


def _softmax_update(s, m_ref, l_ref, rows, shift=0.0, base2=False):
    width = s.shape[1]
    exp = jnp.exp2 if base2 else jnp.exp
    m_prev = m_ref[rows, :]
    m_new = jnp.maximum(m_prev, jnp.max(s, axis=1, keepdims=True) + shift)
    alpha = exp(m_prev - m_new)
    p = exp(s - jnp.concatenate([m_new - shift] * (width // LANES), axis=1))
    l_ref[rows, :] = alpha * l_ref[rows, :] + jnp.sum(p, axis=1, keepdims=True)
    m_ref[rows, :] = m_new
    return p, alpha


def _init_softmax_state(m_ref, l_ref, acc_ref):
    m_ref[...] = jnp.full(m_ref.shape, -jnp.inf, F32)
    l_ref[...] = jnp.zeros(l_ref.shape, F32)
    acc_ref[...] = jnp.zeros(acc_ref.shape, F32)


def _widen(x, width):
    return jnp.concatenate([x] * (width // LANES), axis=1)


def _for_each_key_tile(i, attend):
    assert PROMPT_TILE_BLOCKS == 4

    def full_tile(jj, carry):
        attend(jj * PROMPT_TILE_BLOCKS, PROMPT_TILE_BLOCKS, False)
        return carry

    lax.fori_loop(0, i // PROMPT_TILE_BLOCKS, full_tile, 0)
    left = i % PROMPT_TILE_BLOCKS

    @pl.when(left >= 2)
    def _():
        attend(i - left, 2, False)

    @pl.when(left % 2 == 0)
    def _():
        attend(i, 1, True)

    @pl.when(left % 2 == 1)
    def _():
        attend(i - 1, 2, True)


def _top3_mask(gate, lane, n_valid):
    gt = jnp.where(lane < n_valid, gate, -jnp.inf)
    sel = jnp.zeros(gate.shape, F32)
    lane_f = lane.astype(F32)
    for _ in range(MOBA_TOPK):
        mx = jnp.max(gt, axis=1, keepdims=True)
        idx = jnp.min(jnp.where(gt == mx, lane_f, float(gate.shape[1])), axis=1, keepdims=True)
        hit = (lane_f == idx) & (mx > -jnp.inf)
        sel = jnp.where(hit, 1.0, sel)
        gt = jnp.where(hit, -jnp.inf, gt)
    return sel


def _moba_prompt_kernel(slopes_ref, q_ref, km_ref, kt_ref, v_ref, o_ref,
                        qz_ref, sel_ref, bias_ref, m_ref, l_ref, acc_ref):
    tq = MOBA_BLOCK
    gw = MOBA_GROUP * HEAD_DIM
    gp = pl.program_id(1)
    i = pl.program_id(2)
    chains = range(MOBA_KV_PER_STEP)
    lane = lax.broadcasted_iota(jnp.int32, (tq, LANES), 1)
    slope2 = [[slopes_ref[(gp * MOBA_KV_PER_STEP + c) * MOBA_GROUP + r] * LOG2E
               for r in range(MOBA_GROUP)] for c in chains]
    for c in chains:
        q = q_ref[:, c * gw:(c + 1) * gw]
        q_hi, q_lo = _split_bf16(q)
        km_hi, km_lo = _split_bf16(km_ref[0, c])
        gate = (jnp.dot(q_hi, km_hi, preferred_element_type=F32)
                + jnp.dot(q_lo, km_hi, preferred_element_type=F32)
                + jnp.dot(q_hi, km_lo, preferred_element_type=F32))
        sel_ref[c] = _top3_mask(gate, lane, i)
        qs = q * (SCALE * LOG2E)
        qlane = lax.shift_right_logical(lax.broadcasted_iota(jnp.int32, qs.shape, 1), HEAD_DIM_LOG2)
        for r in range(MOBA_GROUP):
            qz_ref[c, r * tq:(r + 1) * tq, :] = jnp.where(qlane == r, qs, 0.0).astype(BF16)
    _init_softmax_state(m_ref, l_ref, acc_ref)

    @pl.when(i == 0)
    def _():
        width = PROMPT_TILE_BLOCKS * tq
        rel = (lax.broadcasted_iota(jnp.int32, (tq, width), 1)
               - lax.broadcasted_iota(jnp.int32, (tq, width), 0)).astype(F32)
        for c in chains:
            for r in range(MOBA_GROUP):
                bias_ref[c, r * tq:(r + 1) * tq, :] = slope2[c][r] * rel

    def attend(first, n, own_last):
        width = n * tq
        off = ((first - i) * tq).astype(F32)
        vs = pl.ds(pl.multiple_of(first * tq, tq), width)
        for c in chains:
            hd = slice(c * HEAD_DIM, (c + 1) * HEAD_DIM)
            kt = jnp.concatenate([kt_ref[0, first + u, hd, :] for u in range(n)], axis=1)
            kt4 = jnp.concatenate([kt] * MOBA_GROUP, axis=0)
            s = jnp.dot(qz_ref[c], kt4, preferred_element_type=F32)
            oks = []
            for u in range(n):
                if own_last and u == n - 1:
                    oks.append(lax.broadcasted_iota(jnp.int32, (tq, tq), 1)
                               <= lax.broadcasted_iota(jnp.int32, (tq, tq), 0))
                else:
                    oks.append(jnp.sum(jnp.where(lane == first + u, sel_ref[c], 0.0), axis=1,
                                       keepdims=True) > 0.0)
            for r in range(MOBA_GROUP):
                rows = slice(r * tq, (r + 1) * tq)
                sr = jnp.concatenate(
                    [jnp.where(oks[u], s[rows, u * tq:(u + 1) * tq]
                               + bias_ref[c, rows, u * tq:(u + 1) * tq], NEG_INF)
                     for u in range(n)], axis=1)
                p, alpha = _softmax_update(sr, m_ref.at[c], l_ref.at[c], rows,
                                           shift=slope2[c][r] * off, base2=True)
                acc_ref[c, rows, :] = (acc_ref[c, rows, :] * alpha[:, :HEAD_DIM]
                                       + jnp.dot(p.astype(BF16), v_ref[0, c, vs, :],
                                                 preferred_element_type=F32))

    _for_each_key_tile(i, attend)
    out = [acc_ref[c, r * tq:(r + 1) * tq, :] / l_ref[c, r * tq:(r + 1) * tq, :HEAD_DIM]
           for c in chains for r in range(MOBA_GROUP)]
    o_ref[...] = jnp.concatenate(out, axis=1).astype(BF16)


def _moba_prompt(q, km_rep, kt, vh, slopes):
    b, nb = kt.shape[0], kt.shape[1]
    seq_len = nb * MOBA_BLOCK
    tq = MOBA_BLOCK
    gw = MOBA_GROUP * HEAD_DIM
    kps = MOBA_KV_PER_STEP
    rows = MOBA_GROUP * tq
    return pl.pallas_call(
        _moba_prompt_kernel,
        grid=(b, MOBA_KV_HEADS // kps, nb),
        in_specs=[pl.BlockSpec(memory_space=pltpu.SMEM),
                  pl.BlockSpec((tq, kps * gw), lambda bi, g, i: (bi * nb + i, g)),
                  pl.BlockSpec((1, kps, gw, LANES), lambda bi, g, i: (bi, g, 0, 0)),
                  pl.BlockSpec((1, nb, kps * HEAD_DIM, MOBA_BLOCK), lambda bi, g, i: (bi, 0, g, 0)),
                  pl.BlockSpec((1, kps, seq_len, HEAD_DIM), lambda bi, g, i: (bi, g, 0, 0))],
        out_specs=pl.BlockSpec((tq, kps * gw), lambda bi, g, i: (bi * nb + i, g)),
        out_shape=jax.ShapeDtypeStruct((b * seq_len, D_MODEL), BF16),
        scratch_shapes=[pltpu.VMEM((kps, rows, gw), BF16),
                        pltpu.VMEM((kps, tq, LANES), F32),
                        pltpu.VMEM((kps, rows, PROMPT_TILE_BLOCKS * tq), F32),
                        pltpu.VMEM((kps, rows, LANES), F32),
                        pltpu.VMEM((kps, rows, LANES), F32),
                        pltpu.VMEM((kps, rows, HEAD_DIM), F32)],
        compiler_params=_params("parallel", "parallel", "arbitrary"), name="moba_prompt",
    )(slopes, q, km_rep, kt, vh)


def _lambda_full(lq1_ref, lk1_ref, lq2_ref, lk2_ref, lambda_init):
    a = jnp.sum(lq1_ref[...] * lk1_ref[...], axis=1, keepdims=True)
    b = jnp.sum(lq2_ref[...] * lk2_ref[...], axis=1, keepdims=True)
    return jnp.exp(a) - jnp.exp(b) + lambda_init


def _subln(o, w, lambda_init):
    return _rms(o, w, SUBLN_EPS) * (1.0 - lambda_init)


def _diff_prompt_kernel(slopes_ref, q_ref, k_ref, v_ref, lq1_ref, lk1_ref, lq2_ref, lk2_ref,
                        sw_ref, o_ref, qz_ref, bias_ref, m_ref, l_ref, acc_ref, *, tq, lambda_init):
    hp = pl.program_id(1)
    i = pl.program_id(2)
    heads = range(DIFF_HEADS_PER_STEP)
    lanes = [slice(c * DIFF_HW, (c + 1) * DIFF_HW) for c in heads]
    slope = [slopes_ref[hp * DIFF_HEADS_PER_STEP + c] * LOG2E for c in heads]
    everything = slice(None)
    for c in heads:
        qs = q_ref[:, lanes[c]] * (SCALE * LOG2E)
        qlane = lax.shift_right_logical(lax.broadcasted_iota(jnp.int32, qs.shape, 1), HEAD_DIM_LOG2)
        for mp in range(2):
            qz_ref[c, mp * tq:(mp + 1) * tq, :] = jnp.where(qlane == mp, qs, 0.0).astype(BF16)
    _init_softmax_state(m_ref, l_ref, acc_ref)

    def in_tile_offsets(width):
        rows2 = lax.broadcasted_iota(jnp.int32, (2 * tq, width), 0)
        return (lax.broadcasted_iota(jnp.int32, (2 * tq, width), 1)
                - jnp.where(rows2 >= tq, rows2 - tq, rows2))

    @pl.when(i == 0)
    def _():
        rel = in_tile_offsets(PROMPT_TILE_BLOCKS * tq).astype(F32)
        for c in heads:
            bias_ref[c] = slope[c] * rel

    def attend(first, n, own_last):
        width = n * tq
        ks = pl.ds(pl.multiple_of(first * tq, tq), width)
        for c in heads:
            s = _nt_dot(qz_ref[c], k_ref[ks, lanes[c]]) + bias_ref[c, :, :width]
            if own_last:
                s = jnp.where(in_tile_offsets(width) <= (n - 1) * tq, s, NEG_INF)
            p, alpha = _softmax_update(s, m_ref.at[c], l_ref.at[c], everything,
                                       shift=slope[c] * ((first - i) * tq).astype(F32), base2=True)
            acc_ref[c] = acc_ref[c] * alpha + jnp.dot(p.astype(BF16), v_ref[ks, lanes[c]],
                                                      preferred_element_type=F32)

    _for_each_key_tile(i, attend)
    lam = _lambda_full(lq1_ref, lk1_ref, lq2_ref, lk2_ref, lambda_init)
    for c in heads:
        o1 = acc_ref[c, :tq, :] / l_ref[c, :tq, :]
        o2 = acc_ref[c, tq:, :] / l_ref[c, tq:, :]
        o_ref[:, lanes[c]] = _subln(o1 - lam * o2, sw_ref[...], lambda_init).astype(BF16)


def _diff_prompt(q, kb, vb, lams, subln_w, slopes, *, batch, lambda_init):
    n = q.shape[0]
    seq_len = n // batch
    tq = 256
    nq = seq_len // tq
    lam_specs = [_resident((1, HEAD_DIM))] * 4
    hps = DIFF_HEADS_PER_STEP
    gw = hps * DIFF_HW
    return pl.pallas_call(
        functools.partial(_diff_prompt_kernel, tq=tq, lambda_init=lambda_init),
        grid=(batch, DIFF_HEADS // hps, nq),
        in_specs=[pl.BlockSpec(memory_space=pltpu.SMEM),
                  pl.BlockSpec((tq, gw), lambda bi, h, i: (bi * nq + i, h)),
                  pl.BlockSpec((seq_len, gw), lambda bi, h, i: (bi, h)),
                  pl.BlockSpec((seq_len, gw), lambda bi, h, i: (bi, h)),
                  *lam_specs, _resident((1, DIFF_HW))],
        out_specs=pl.BlockSpec((tq, gw), lambda bi, h, i: (bi * nq + i, h)),
        out_shape=jax.ShapeDtypeStruct((n, D_MODEL), BF16),
        scratch_shapes=[pltpu.VMEM((hps, 2 * tq, DIFF_HW), BF16),
                        pltpu.VMEM((hps, 2 * tq, PROMPT_TILE_BLOCKS * tq), F32),
                        pltpu.VMEM((hps, 2 * tq, LANES), F32),
                        pltpu.VMEM((hps, 2 * tq, LANES), F32),
                        pltpu.VMEM((hps, 2 * tq, DIFF_HW), F32)],
        compiler_params=_params("parallel", "parallel", "arbitrary"), name="diff_prompt",
    )(slopes, q, kb, vb, *lams, subln_w)


def _moba_decode_kernel(pt_ref, qs_ref, qf_ref, slope_ref, qi_ref, knew_ref, vnew_ref,
                        *refs, pps, past_len):
    k_refs, v_refs = refs[:pps], refs[pps:2 * pps]
    o_ref, s_ref, km_ref, bmax_ref, sel_ref, m_ref, l_ref, acc_ref = refs[2 * pps:]
    phase = pl.program_id(1)
    step = pl.program_id(2)
    n_rows = qs_ref.shape[1]
    chunk = pps * PAGE_SIZE
    ppb = MOBA_BLOCK // PAGE_SIZE
    blocks_per_step = pps // ppb
    lane = lax.broadcasted_iota(jnp.int32, (n_rows, LANES), 1)
    shift = slope_ref[...] * (step * chunk - past_len).astype(F32)

    qh, ql = _split_bf16(qs_ref[0])
    q2 = jnp.concatenate([qh, ql], axis=0)

    def qk(kt32):
        kh, kl = _split_bf16(kt32)
        s2 = jnp.dot(q2, kh, preferred_element_type=F32)
        return s2[:n_rows] + s2[n_rows:] + jnp.dot(qh, kl, preferred_element_type=F32)

    def pv(p, vt32):
        ph, plo = _split_bf16(p)
        vh, vl = _split_bf16(vt32)
        o2 = _nt_dot(jnp.concatenate([ph, plo], axis=0), vh)
        return o2[:n_rows] + o2[n_rows:] + _nt_dot(ph, vl)

    def logits(kt32, n):
        tok = lax.broadcasted_iota(jnp.int32, (1, n), 1).astype(F32)
        return qk(kt32) + slope_ref[...] * (tok - qi_ref[...]), tok

    @pl.when(phase == 0)
    def _():
        @pl.when(step == 0)
        def _():
            km_ref[...] = jnp.zeros(km_ref.shape, F32)
            bmax_ref[...] = jnp.full(bmax_ref.shape, -jnp.inf, F32)

        kt = jnp.concatenate([r[0] for r in k_refs], axis=1)
        s, _ = logits(kt, chunk)
        s_ref[step] = s
        km_lane = lax.broadcasted_iota(jnp.int32, km_ref.shape, 1)
        km, bmax = km_ref[...], bmax_ref[...]
        for t in range(blocks_per_step):
            blk = step * blocks_per_step + t
            tot = sum(k_refs[t * ppb + u][0] for u in range(ppb))
            km = jnp.where(km_lane == blk, jnp.sum(tot, axis=1, keepdims=True) / MOBA_BLOCK, km)
            top = jnp.max(s[:, t * MOBA_BLOCK:(t + 1) * MOBA_BLOCK], axis=1, keepdims=True)
            bmax = jnp.where(lane == blk, top + shift, bmax)
        km_ref[...] = km
        bmax_ref[...] = bmax

    @pl.when(phase == 1)
    def _():
        @pl.when(step == 0)
        def _():
            gr = _dot_f32(qf_ref[0], km_ref[...])
            n_blocks = past_len // MOBA_BLOCK
            rpg = n_rows // MOBA_KV_HEADS
            dq = rpg // MOBA_GROUP
            for g in range(MOBA_KV_HEADS):
                gs = sum(gr[g * rpg + r * dq:g * rpg + (r + 1) * dq] for r in range(MOBA_GROUP))
                sel = _top3_mask(gs, lane[:dq], n_blocks)
                sel_ref[g * rpg:(g + 1) * rpg, :] = jnp.concatenate([sel] * MOBA_GROUP, axis=0)
            m_past = jnp.max(jnp.where(sel_ref[...] > 0.0, bmax_ref[...], -jnp.inf), axis=1,
                             keepdims=True)
            s_new, tok_new = logits(knew_ref[0], PAGE_SIZE)
            s_new = jnp.where(tok_new <= qi_ref[...], s_new, NEG_INF)
            m_fin = jnp.maximum(m_past, jnp.max(s_new, axis=1, keepdims=True))
            p_new = jnp.exp(s_new - m_fin)
            m_ref[...] = jnp.broadcast_to(m_fin, m_ref.shape)
            l_ref[...] = jnp.broadcast_to(jnp.sum(p_new, axis=1, keepdims=True), l_ref.shape)
            acc_ref[...] = pv(p_new, vnew_ref[0])

        vt = jnp.concatenate([r[0] for r in v_refs], axis=1)
        blk = lax.broadcasted_iota(jnp.int32, (LANES, chunk), 0)
        key = step * chunk + lax.broadcasted_iota(jnp.int32, (LANES, chunk), 1)
        expand = jnp.where(blk == lax.shift_right_logical(key, MOBA_BLOCK_LOG2), 1.0,
                           0.0).astype(BF16)
        ok = jnp.dot(sel_ref[...].astype(BF16), expand, preferred_element_type=F32) > 0.5
        p = jnp.where(ok, jnp.exp(s_ref[step] - _widen(m_ref[...] - shift, chunk)), 0.0)
        l_ref[...] += jnp.sum(p, axis=1, keepdims=True)
        acc_ref[...] += pv(p, vt)

        @pl.when(step == pl.num_programs(2) - 1)
        def _():
            o_ref[0] = acc_ref[...] / _widen(l_ref[...], acc_ref.shape[1])


def _moba_decode(q_scaled, q_plain, slope_rows, qi_rows, knew_t, vnew_t, k_pool_t, v_pool_t,
                 page_table, page_offset):
    db, n_pages = page_table.shape
    n_rows, width = q_scaled.shape[1], q_scaled.shape[2]
    pps = 16
    n_steps = n_pages // pps
    per_seq = lambda shape: pl.BlockSpec((1,) + shape, lambda b, ph, s, pt: (b, 0, 0))
    const = lambda shape: pl.BlockSpec(shape, lambda b, ph, s, pt: (0, 0))

    def page_spec(t, for_keys):
        def index(b, ph, s, pt):
            at = s * (1 - ph) + (n_steps - 1) * ph if for_keys else s * ph
            return (pt[b, at * pps + t] + page_offset, 0, 0)
        return pl.BlockSpec((1, width, PAGE_SIZE), index)

    grid_spec = pltpu.PrefetchScalarGridSpec(
        num_scalar_prefetch=1, grid=(db, 2, n_steps),
        in_specs=[per_seq((n_rows, width)), per_seq((n_rows, width)),
                  const((n_rows, 1)), const((n_rows, 1)),
                  per_seq((width, PAGE_SIZE)), per_seq((width, PAGE_SIZE)),
                  *[page_spec(t, True) for t in range(pps)],
                  *[page_spec(t, False) for t in range(pps)]],
        out_specs=per_seq((n_rows, width)),
        scratch_shapes=[pltpu.VMEM((n_steps, n_rows, pps * PAGE_SIZE), F32),
                        pltpu.VMEM((width, LANES), F32), pltpu.VMEM((n_rows, LANES), F32),
                        pltpu.VMEM((n_rows, LANES), F32), pltpu.VMEM((n_rows, LANES), F32),
                        pltpu.VMEM((n_rows, LANES), F32), pltpu.VMEM((n_rows, width), F32)])
    return pl.pallas_call(
        functools.partial(_moba_decode_kernel, pps=pps, past_len=n_pages * PAGE_SIZE),
        grid_spec=grid_spec, out_shape=jax.ShapeDtypeStruct((db, n_rows, width), F32),
        compiler_params=_params("parallel", "arbitrary", "arbitrary"), name="moba_decode",
    )(page_table, q_scaled, q_plain, slope_rows, qi_rows, knew_t, vnew_t,
      *([k_pool_t] * pps), *([v_pool_t] * pps))


def _diff_decode_kernel(pt_ref, qx_ref, slope_ref, qi_ref, head_ref, knew_ref, vnew_ref,
                        lq1_ref, lk1_ref, lq2_ref, lk2_ref, sw_ref, *refs, pps, past_len, lambda_init):
    n_seq = qx_ref.shape[0]
    k_refs, v_refs = refs[:n_seq * pps], refs[n_seq * pps:2 * n_seq * pps]
    o_ref, bias_ref, m_ref, l_ref, acc_ref = refs[2 * n_seq * pps:]
    step = pl.program_id(1)
    n_rows = qx_ref.shape[1]
    chunk = pps * PAGE_SIZE
    everything = slice(None)
    slope2 = slope_ref[...] * LOG2E

    def alibi_bias(n_cols):
        col = lax.broadcasted_iota(jnp.int32, (n_rows, n_cols), 1)
        tok = lax.shift_right_logical(col, DIFF_HEADS_LOG2)
        own_head = (col & (DIFF_HEADS - 1)) == head_ref[...]
        tokf = tok.astype(F32)
        return jnp.where(own_head, slope2 * (tokf - qi_ref[...]), NEG_INF), tokf

    @pl.when(step == 0)
    def _():
        _init_softmax_state(m_ref, l_ref, acc_ref)
        bias_ref[...] = alibi_bias(chunk * DIFF_HEADS)[0]

    def update(c, k32, v32, bias, shift):
        s = _nt_dot(qx_ref[c], k32.astype(BF16)) + bias
        p, alpha = _softmax_update(s, m_ref.at[c], l_ref.at[c], everything, shift=shift,
                                   base2=True)
        acc_ref[c] = acc_ref[c] * alpha + jnp.dot(p.astype(BF16), v32.astype(BF16),
                                                  preferred_element_type=F32)

    for c in range(n_seq):
        kc = jnp.concatenate([r[0] for r in k_refs[c * pps:(c + 1) * pps]], axis=0)
        vc = jnp.concatenate([r[0] for r in v_refs[c * pps:(c + 1) * pps]], axis=0)
        update(c, kc, vc, bias_ref[...], slope2 * (step * chunk - past_len).astype(F32))

    @pl.when(step == pl.num_programs(1) - 1)
    def _():
        bias_new, tokf = alibi_bias(PAGE_SIZE * DIFF_HEADS)
        bias_new = jnp.where(tokf <= qi_ref[...], bias_new, NEG_INF)
        dq = n_rows // (2 * DIFF_HEADS)
        lam = _lambda_full(lq1_ref, lk1_ref, lq2_ref, lk2_ref, lambda_init)
        for c in range(n_seq):
            update(c, knew_ref[c], vnew_ref[c], bias_new, 0.0)
            o = acc_ref[c] / l_ref[c]
            outs = []
            for hd in range(DIFF_HEADS):
                o1 = o[(2 * hd) * dq:(2 * hd + 1) * dq]
                o2 = o[(2 * hd + 1) * dq:(2 * hd + 2) * dq]
                outs.append(_subln(o1 - lam * o2, sw_ref[...], lambda_init))
            o_ref[c] = jnp.concatenate(outs, axis=1)


def _diff_decode(qx, slope_rows, qi_rows, head_rows, knew, vnew, lams, subln_w, k_pool, v_pool,
                 page_table, page_offset, *, lambda_init):
    db, n_pages = page_table.shape
    n_rows = qx.shape[1]
    sps = DIFF_SEQS_PER_STEP
    pps = 8 // sps
    page_rows = PAGE_SIZE * DIFF_HEADS
    dq = n_rows // (2 * DIFF_HEADS)
    per_seq = lambda shape: pl.BlockSpec((sps,) + shape, lambda b, s, pt: (b, 0, 0))
    const = lambda shape: pl.BlockSpec(shape, lambda b, s, pt: (0, 0))

    def page_spec(c, t):
        return pl.BlockSpec((1, page_rows, DIFF_HW),
                            lambda b, s, pt: (pt[b * sps + c, s * pps + t] + page_offset, 0, 0))

    page_specs = [page_spec(c, t) for c in range(sps) for t in range(pps)]
    grid_spec = pltpu.PrefetchScalarGridSpec(
        num_scalar_prefetch=1, grid=(db // sps, n_pages // pps),
        in_specs=[per_seq((n_rows, DIFF_HW)), const((n_rows, 1)), const((n_rows, 1)),
                  const((n_rows, 1)), per_seq((page_rows, DIFF_HW)), per_seq((page_rows, DIFF_HW)),
                  *([const((1, HEAD_DIM))] * 4), const((1, DIFF_HW)), *(page_specs * 2)],
        out_specs=per_seq((dq, D_MODEL)),
        scratch_shapes=[pltpu.VMEM((n_rows, pps * page_rows), F32),
                        pltpu.VMEM((sps, n_rows, LANES), F32), pltpu.VMEM((sps, n_rows, LANES), F32),
                        pltpu.VMEM((sps, n_rows, DIFF_HW), F32)])
    return pl.pallas_call(
        functools.partial(_diff_decode_kernel, pps=pps, past_len=n_pages * PAGE_SIZE,
                          lambda_init=lambda_init),
        grid_spec=grid_spec, out_shape=jax.ShapeDtypeStruct((db, dq, D_MODEL), F32),
        compiler_params=_params("parallel", "arbitrary"), name="diff_decode",
    )(page_table, qx, slope_rows, qi_rows, head_rows, knew, vnew, *lams, subln_w,
      *([k_pool] * (sps * pps)), *([v_pool] * (sps * pps)))


def _router_logits(t, wr_hi_ref, wr_lo_ref, br_ref):
    t_hi, t_lo = _split_bf16(t)
    return (jnp.dot(t_hi, wr_hi_ref[...], preferred_element_type=F32)
            + jnp.dot(t_lo, wr_hi_ref[...], preferred_element_type=F32)
            + jnp.dot(t_hi, wr_lo_ref[...], preferred_element_type=F32)) + br_ref[...]


def _router_combine(logits):
    lane = lax.broadcasted_iota(jnp.int32, logits.shape, 1)
    width = logits.shape[1]
    first = lambda hit: jnp.min(jnp.where(hit, lane, width), axis=1, keepdims=True)
    is_g = lane < N_GROUPS
    gl = jnp.where(is_g, logits, -jnp.inf)
    g_max = jnp.max(gl, axis=1, keepdims=True)
    g_sel = first(gl == g_max)
    g_w = 1.0 / jnp.sum(jnp.where(is_g, jnp.exp(logits - g_max), 0.0), axis=1, keepdims=True)
    e_lo = N_GROUPS + g_sel * EXPERTS_PER_GROUP
    el = jnp.where((lane >= e_lo) & (lane < e_lo + EXPERTS_PER_GROUP), logits, -jnp.inf)
    v1 = jnp.max(el, axis=1, keepdims=True)
    i1 = first(el == v1)
    el2 = jnp.where(lane == i1, -jnp.inf, el)
    v2 = jnp.max(el2, axis=1, keepdims=True)
    i2 = first(el2 == v2)
    e2 = jnp.exp(v2 - v1)
    w1 = 1.0 / (1.0 + e2)
    w2 = e2 / (1.0 + e2)
    return jnp.where(lane == i1, g_w * w1, 0.0) + jnp.where(lane == i2, g_w * w2, 0.0)


def _wo_moe_kernel(x_ref, a_ref, wo_ref, nw_ref, wr_hi_ref, wr_lo_ref, br_ref, wgu_ref, wd_ref,
                   fw_ref, o_ref, hid_ref, *, final):
    y1 = x_ref[...] + jnp.dot(a_ref[...], wo_ref[...], preferred_element_type=F32)
    t = _rms(y1, nw_ref[...], NORM_EPS)
    combine = _router_combine(_router_logits(t, wr_hi_ref, wr_lo_ref, br_ref))
    tb = t.astype(BF16)
    for e in range(N_EXPERTS):
        gu = jnp.dot(tb, wgu_ref[e], preferred_element_type=F32)
        gate, up = gu[:, :D_EXPERT], gu[:, D_EXPERT:]
        c = combine[:, N_GROUPS + e:N_GROUPS + e + 1]
        hid = gate * (1.0 / (1.0 + jnp.exp(-gate))) * up * c
        hid_ref[:, e * D_EXPERT:(e + 1) * D_EXPERT] = hid.astype(BF16)
    y2 = y1 + jnp.dot(hid_ref[...], wd_ref[...], preferred_element_type=F32)
    if final:
        y2 = _rms(y2, fw_ref[...], NORM_EPS)
    o_ref[...] = y2


def _wo_moe(x, attn, wo_bf, nw, wr_hi, wr_lo, br, wgu_bf, wd_bf, fw, *, final):
    n = x.shape[0]
    tm = min(512, n)
    row = lambda t: (t, 0)
    return pl.pallas_call(
        functools.partial(_wo_moe_kernel, final=final),
        grid=(n // tm,),
        in_specs=[pl.BlockSpec((tm, D_MODEL), row), pl.BlockSpec((tm, D_MODEL), row),
                  _resident(wo_bf.shape), _resident((1, D_MODEL)),
                  _resident(wr_hi.shape), _resident(wr_lo.shape), _resident(br.shape),
                  _resident(wgu_bf.shape), _resident(wd_bf.shape), _resident((1, D_MODEL))],
        out_specs=pl.BlockSpec((tm, D_MODEL), row),
        out_shape=jax.ShapeDtypeStruct((n, D_MODEL), F32),
        scratch_shapes=[pltpu.VMEM((tm, N_EXPERTS * D_EXPERT), BF16)],
        compiler_params=_params("parallel"), name="wo_moe",
    )(x, attn, wo_bf, nw.reshape(1, D_MODEL), wr_hi, wr_lo, br, wgu_bf, wd_bf,
      fw.reshape(1, D_MODEL))


def _wo_moe_f32_kernel(x_ref, a_ref, wo_ref, nw_ref, wr_ref, br_ref, wg_ref, wu_ref, wd_ref,
                       fw_ref, o_ref, y1_ref, t_ref, comb_ref, acc_ref, *, final):
    e = pl.program_id(0)

    @pl.when(e == 0)
    def _():
        y1 = x_ref[...] + _dot_f32(a_ref[...], wo_ref[...])
        t = _rms(y1, nw_ref[...], NORM_EPS)
        y1_ref[...] = y1
        t_ref[...] = t
        comb_ref[...] = _router_combine(_dot_f32(t, wr_ref[...]) + br_ref[...])
        acc_ref[...] = jnp.zeros(acc_ref.shape, F32)

    t = t_ref[...]
    gate = _dot_f32(t, wg_ref[0])
    up = _dot_f32(t, wu_ref[0])
    lane = lax.broadcasted_iota(jnp.int32, comb_ref.shape, 1)
    c = jnp.sum(jnp.where(lane == N_GROUPS + e, comb_ref[...], 0.0), axis=1, keepdims=True)
    hid = gate * (1.0 / (1.0 + jnp.exp(-gate))) * up * c
    acc_ref[...] += _dot_f32(hid, wd_ref[0])

    @pl.when(e == pl.num_programs(0) - 1)
    def _():
        y2 = y1_ref[...] + acc_ref[...]
        if final:
            y2 = _rms(y2, fw_ref[...], NORM_EPS)
        o_ref[...] = y2


def _wo_moe_f32(x, attn, wo, nw, wr, br, w_gate, w_up, w_down, fw, *, final):
    n = x.shape[0]
    whole = lambda shape: pl.BlockSpec(shape, lambda e: (0,) * len(shape))
    per_expert = lambda shape: pl.BlockSpec((1,) + shape, lambda e: (e, 0, 0))
    return pl.pallas_call(
        functools.partial(_wo_moe_f32_kernel, final=final),
        grid=(N_EXPERTS,),
        in_specs=[whole((n, D_MODEL)), whole((n, D_MODEL)), whole(wo.shape), whole((1, D_MODEL)),
                  whole(wr.shape), whole(br.shape), per_expert((D_MODEL, D_EXPERT)),
                  per_expert((D_MODEL, D_EXPERT)), per_expert((D_EXPERT, D_MODEL)),
                  whole((1, D_MODEL))],
        out_specs=whole((n, D_MODEL)),
        out_shape=jax.ShapeDtypeStruct((n, D_MODEL), F32),
        scratch_shapes=[pltpu.VMEM((n, D_MODEL), F32), pltpu.VMEM((n, D_MODEL), F32),
                        pltpu.VMEM((n, ROUTER_LANES), F32), pltpu.VMEM((n, D_MODEL), F32)],
        compiler_params=_params("arbitrary"), name="wo_moe_f32",
    )(x, attn, wo, nw.reshape(1, D_MODEL), wr, br, w_gate, w_up, w_down, fw.reshape(1, D_MODEL))


def _moe_weights(w_group, b_group, w_expert, b_expert, w_gate, w_up, w_down):
    n_r = N_GROUPS + N_EXPERTS
    wr = jnp.pad(jnp.concatenate([w_group, w_expert], axis=1), ((0, 0), (0, ROUTER_LANES - n_r)))
    wr_hi = wr.astype(BF16)
    wr_lo = (wr - wr_hi.astype(F32)).astype(BF16)
    br = jnp.pad(jnp.concatenate([b_group, b_expert]), (0, ROUTER_LANES - n_r)).reshape(1, -1)
    wgu = jnp.concatenate([w_gate, w_up], axis=2).astype(BF16)
    wd = w_down.reshape(N_EXPERTS * D_EXPERT, D_MODEL).astype(BF16)
    return (wr_hi, wr_lo, br, wgu, wd), (wr, br)


def _block_diag_rows(q, n_slots):
    db, _, m, dq, w = q.shape
    eye = jnp.eye(n_slots, dtype=q.dtype)
    out = q[:, :, :, :, None, :] * eye[None, :, None, None, :, None]
    return out.reshape(db, n_slots * m * dq, n_slots * w)


def _pad_tokens(x):
    return jnp.pad(x, ((0, 0), (0, PAGE_SIZE - x.shape[1])) + ((0, 0),) * (x.ndim - 2))


def kernel(x_prompt, x_sample, cache_k_moba, cache_v_moba, cache_k_diff, cache_v_diff, page_table,
           norm_attn, norm_ffn, norm_final, moba_w_qkv, moba_w_o, diff_w_qkv, diff_w_o,
           diff_lambda_q1, diff_lambda_k1, diff_lambda_q2, diff_lambda_k2, diff_subln,
           moe_w_group, moe_b_group, moe_w_expert, moe_b_expert, moe_w_gate, moe_w_up, moe_w_down):
    batch, seq_len, _ = x_prompt.shape
    db, dq, _ = x_sample.shape
    n_pool = cache_k_moba.shape[1]
    depth = norm_attn.shape[0]
    nb = seq_len // MOBA_BLOCK
    slopes_moba = _alibi_slopes(MOBA_HEADS)
    slopes_diff = _alibi_slopes(DIFF_HEADS)
    qi_new = np.arange(dq, dtype=np.float32)

    moba_pool_t = lambda c: jnp.transpose(c, (0, 1, 3, 4, 2)).reshape(-1, MOBA_KVW, PAGE_SIZE)
    diff_pool = lambda c: c.reshape(-1, PAGE_SIZE * DIFF_HEADS, DIFF_HW)
    k_moba_t, v_moba_t = moba_pool_t(cache_k_moba), moba_pool_t(cache_v_moba)
    k_diff, v_diff = diff_pool(cache_k_diff), diff_pool(cache_v_diff)

    yp = x_prompt.reshape(batch * seq_len, D_MODEL)
    ys = x_sample.reshape(db * dq, D_MODEL)
    outs = {name: [] for name in ("km_p", "vm_p", "kd_p", "vd_p", "km_s", "vm_s", "kd_s", "vd_s")}
    for i in range(depth):
        j = i // 2
        final = i == depth - 1
        if i % 2 == 0:
            w_bf = moba_w_qkv[j].astype(BF16)
            wkt_bf = moba_w_qkv[j][:, D_MODEL:D_MODEL + MOBA_KVW].T.astype(BF16)
            qp, kp, vp, kt, vh, kmean = _norm_qkv(yp, norm_attn[i], w_bf, mode="moba",
                                                  seq_len=seq_len, wkt_bf=wkt_bf)
            km = kmean.reshape(batch, nb, MOBA_KV_HEADS, HEAD_DIM).transpose(0, 2, 3, 1)
            km_rep = jnp.pad(jnp.tile(km, (1, 1, MOBA_GROUP, 1)),
                             ((0, 0), (0, 0), (0, 0), (0, LANES - nb)))
            attn_p = _moba_prompt(qp, km_rep, kt, vh, jnp.asarray(slopes_moba))

            qs, ks, vs = _norm_qkv(ys, norm_attn[i], moba_w_qkv[j], mode="plain")
            q5 = qs.reshape(db, dq, MOBA_KV_HEADS, MOBA_GROUP, HEAD_DIM).transpose(0, 2, 3, 1, 4)
            qf = _block_diag_rows(q5, MOBA_KV_HEADS)
            slope_rows = np.repeat(slopes_moba, dq).reshape(-1, 1)
            qi_rows = np.tile(qi_new, MOBA_HEADS).reshape(-1, 1)
            new_t = lambda x: _pad_tokens(x.reshape(db, dq, MOBA_KVW)).transpose(0, 2, 1)
            o_rows = _moba_decode(qf * SCALE, qf, jnp.asarray(slope_rows), jnp.asarray(qi_rows),
                                  new_t(ks), new_t(vs), k_moba_t, v_moba_t, page_table, j * n_pool)
            o6 = o_rows.reshape(db, MOBA_KV_HEADS, MOBA_GROUP, dq, MOBA_KV_HEADS, HEAD_DIM)
            o5 = jnp.stack([o6[:, g, :, :, g, :] for g in range(MOBA_KV_HEADS)], axis=1)
            attn_s = o5.transpose(0, 3, 1, 2, 4).reshape(db * dq, D_MODEL)
            w_o = moba_w_o[j]
            kvh, kvd = MOBA_KV_HEADS, HEAD_DIM
            names = ("km_p", "vm_p", "km_s", "vm_s")
        else:
            lambda_init = 0.8 - 0.6 * math.exp(-0.3 * i)
            w_bf = diff_w_qkv[j].astype(BF16)
            lams = [a[j].reshape(1, HEAD_DIM) for a in
                    (diff_lambda_q1, diff_lambda_k1, diff_lambda_q2, diff_lambda_k2)]
            subln_w = diff_subln[j].reshape(1, DIFF_HW)
            qp, kp, vp, kb, vb = _norm_qkv(yp, norm_attn[i], w_bf, mode="diff")
            attn_p = _diff_prompt(qp, kb, vb, lams, subln_w, jnp.asarray(slopes_diff),
                                  batch=batch, lambda_init=lambda_init)

            qs, ks, vs = _norm_qkv(ys, norm_attn[i], diff_w_qkv[j], mode="plain")
            q5 = qs.reshape(db, dq, DIFF_HEADS, 2, 1, HEAD_DIM).transpose(0, 2, 3, 4, 1, 5)
            qx = _block_diag_rows(q5.reshape(db * DIFF_HEADS, 2, 1, dq, HEAD_DIM), 2)
            qx = (qx.reshape(db, 2 * DIFF_HEADS * dq, DIFF_HW) * (SCALE * LOG2E)).astype(BF16)
            slope_rows = np.repeat(slopes_diff, 2 * dq).reshape(-1, 1)
            qi_rows = np.tile(qi_new, 2 * DIFF_HEADS).reshape(-1, 1)
            head_rows = np.repeat(np.arange(DIFF_HEADS, dtype=np.int32), 2 * dq).reshape(-1, 1)
            new_flat = lambda x: _pad_tokens(x.reshape(db, dq, DIFF_HEADS, DIFF_HW)).reshape(
                db, PAGE_SIZE * DIFF_HEADS, DIFF_HW)
            attn_s = _diff_decode(qx, jnp.asarray(slope_rows), jnp.asarray(qi_rows),
                                  jnp.asarray(head_rows), new_flat(ks), new_flat(vs), lams, subln_w,
                                  k_diff, v_diff, page_table, j * n_pool,
                                  lambda_init=lambda_init).reshape(db * dq, D_MODEL)
            w_o = diff_w_o[j]
            kvh, kvd = DIFF_HEADS, DIFF_HW
            names = ("kd_p", "vd_p", "kd_s", "vd_s")
        outs[names[0]].append(kp.reshape(batch, seq_len, kvh, kvd))
        outs[names[1]].append(vp.reshape(batch, seq_len, kvh, kvd))
        outs[names[2]].append(ks.reshape(db, dq, kvh, kvd))
        outs[names[3]].append(vs.reshape(db, dq, kvh, kvd))

        moe_w, (wr, br) = _moe_weights(moe_w_group[i], moe_b_group[i], moe_w_expert[i],
                                       moe_b_expert[i], moe_w_gate[i], moe_w_up[i], moe_w_down[i])
        yp = _wo_moe(yp, attn_p, w_o.astype(BF16), norm_ffn[i], *moe_w, norm_final, final=final)
        ys = _wo_moe_f32(ys, attn_s, w_o, norm_ffn[i], wr, br, moe_w_gate[i], moe_w_up[i],
                         moe_w_down[i], norm_final, final=final)

    stack = lambda name: jnp.stack(outs[name])
    return (yp.reshape(batch, seq_len, D_MODEL), ys.reshape(db, dq, D_MODEL),
            stack("km_p"), stack("vm_p"), stack("kd_p"), stack("vd_p"),
            stack("km_s"), stack("vm_s"), stack("kd_s"), stack("vd_s"))
```
